```python
import math
import jax, jax.numpy as jnp
from jax import lax
import numpy as np

D_MODEL = 4096
BATCH = 2
SEQ = 4096
DEPTH = 4
DEC_BATCH = 32
DEC_SEQ = 64
PAST_LEN = 1024

CHUNK = 64
N_META = 16
N_A_LAYERS = DEPTH // 2
N_B_LAYERS = DEPTH - N_A_LAYERS
GDN_HEADS = 32
GDN_HEAD_DIM = 128
GDN_WIDTH = GDN_HEADS * GDN_HEAD_DIM
CONV_WIDTH = 4
MLA_HEADS = 32
Q_LORA = 1024
KV_LORA = 512
NOPE_DIM = 128
ROPE_DIM = 64
V_DIM = 128
ROPE_THETA = 10000.0
MLA_SCALE = (NOPE_DIM + ROPE_DIM) ** -0.5
Q_BLOCK = 128
D_FF = 4 * D_MODEL
DN_ALPHA = (2 * DEPTH) ** 0.25
DN_BETA = (8 * DEPTH) ** -0.25
NORM_EPS = 1e-6
NEG_INF = -1e30

kernel_name = 'yoco_gdn_mla_streaming_encoder_step'


def layer_norm(x, g, b):
    xf = x.astype(jnp.float32)
    mu = jnp.mean(xf, -1, keepdims=True)
    var = jnp.mean(jnp.square(xf - mu), -1, keepdims=True)
    y = (xf - mu) * lax.rsqrt(var + NORM_EPS) * g.astype(jnp.float32) + b.astype(jnp.float32)
    return y.astype(x.dtype)


def rms_norm(x, g):
    xf = x.astype(jnp.float32)
    y = xf * lax.rsqrt(jnp.mean(xf * xf, -1, keepdims=True) + NORM_EPS) * g.astype(jnp.float32)
    return y.astype(x.dtype)


def l2_normalize(x):
    xf = x.astype(jnp.float32)
    return xf * lax.rsqrt(jnp.sum(xf * xf, -1, keepdims=True) + NORM_EPS)


def apply_rope(x, pos):
    half = ROPE_DIM // 2
    inv_freq = 1.0 / (ROPE_THETA ** (jnp.arange(half, dtype=jnp.float32) / half))
    ang = pos[:, None] * inv_freq[None, :]
    cos = jnp.cos(ang)[:, None, :]
    sin = jnp.sin(ang)[:, None, :]
    xf = x.astype(jnp.float32)
    x1, x2 = xf[..., :half], xf[..., half:]
    return jnp.concatenate([x1 * cos - x2 * sin, x2 * cos + x1 * sin], -1).astype(x.dtype)


def causal_conv_silu(u, conv_state, w):
    T = u.shape[1]
    up = jnp.concatenate([conv_state.astype(u.dtype), u], axis=1)
    y = up[:, 0:T] * w[0]
    for i in range(1, CONV_WIDTH):
        y = y + up[:, i:i + T] * w[i]
    return jax.nn.silu(y), up[:, -(CONV_WIDTH - 1):]


def gated_delta_rule(q, k, v, g, beta, s0, block):
    B, T, H, DK = q.shape
    DV = v.shape[-1]
    pad = (-T) % block
    if pad:
        def fp(a):
            return jnp.pad(a, [(0, 0), (pad, 0)] + [(0, 0)] * (a.ndim - 2))
        q, k, v, g, beta = fp(q), fp(k), fp(v), fp(g), fp(beta)
    N = (T + pad) // block

    def rs(a):
        return a.reshape((B, N, block) + a.shape[2:])
    q, k, v, g, beta = rs(q), rs(k), rs(v), rs(g), rs(beta)
    G_t = jnp.cumsum(g, axis=2)
    G = jnp.transpose(G_t, (0, 1, 3, 2))
    idx = jnp.arange(block)
    incl = idx[:, None] >= idx[None, :]
    strict = idx[:, None] > idx[None, :]
    decay = jnp.exp(jnp.where(incl, G[..., :, None] - G[..., None, :], -jnp.inf))
    kb = k * beta[..., None]
    a_mat = jnp.where(strict, jnp.einsum('bnihd,bnjhd->bnhij', kb, k) * decay, 0.0)
    eye = jnp.eye(block, dtype=jnp.float32)
    t_mat = lax.linalg.triangular_solve(a_mat + eye, jnp.broadcast_to(eye, a_mat.shape),
                                        left_side=True, lower=True, unit_diagonal=True)
    eG = jnp.exp(G_t)
    value = jnp.einsum('bnhij,bnjhd->bnihd', t_mat, v * beta[..., None])
    kcd = jnp.einsum('bnhij,bnjhd->bnihd', t_mat, kb * eG[..., None])
    qk = jnp.einsum('bnihd,bnjhd->bnhij', q, k) * decay
    q_dec = q * eG[..., None]
    g_last = G[..., -1]
    k_dec = k * jnp.exp(g_last[:, :, None, :] - G_t)[..., None]

    def step(S, xs):
        value_n, kcd_n, qk_n, qdec_n, kdec_n, glast_n = xs
        u = value_n - jnp.einsum('bchk,bhkv->bchv', kcd_n, S)
        o = jnp.einsum('bchk,bhkv->bchv', qdec_n, S) + jnp.einsum('bhij,bjhv->bihv', qk_n, u)
        S = S * jnp.exp(glast_n)[..., None, None] + jnp.einsum('bchk,bchv->bhkv', kdec_n, u)
        return S, o

    xs = tuple(jnp.moveaxis(a, 1, 0) for a in (value, kcd, qk, q_dec, k_dec, g_last))
    S, o = lax.scan(step, s0, xs)
    o = jnp.moveaxis(o, 0, 1).reshape(B, N * block, H, DV)[:, pad:]
    return o, S


def gdn_mixer(x, conv_state, s0, w_in, conv_w, a_log, dt_bias, o_gain, w_o, block):
    B, T, _ = x.shape
    W, H, DK = GDN_WIDTH, GDN_HEADS, GDN_HEAD_DIM
    proj = x @ w_in
    qkv, new_conv = causal_conv_silu(proj[..., :3 * W], conv_state, conv_w)
    z = proj[..., 3 * W:4 * W].reshape(B, T, H, DK)
    a = proj[..., 4 * W:4 * W + H]
    b = proj[..., 4 * W + H:]
    q = l2_normalize(qkv[..., :W].reshape(B, T, H, DK)) * (DK ** -0.5)
    k = l2_normalize(qkv[..., W:2 * W].reshape(B, T, H, DK))
    v = qkv[..., 2 * W:].reshape(B, T, H, DK).astype(jnp.float32)
    g = -jnp.exp(a_log.astype(jnp.float32)) * jax.nn.softplus(a.astype(jnp.float32) + dt_bias.astype(jnp.float32))
    beta = jax.nn.sigmoid(b.astype(jnp.float32))
    o, s_new = gated_delta_rule(q, k, v, g, beta, s0.astype(jnp.float32), block)
    o = rms_norm(o, o_gain) * jax.nn.silu(z.astype(jnp.float32))
    out = o.astype(x.dtype).reshape(B, T, W) @ w_o
    return out, new_conv, s_new.astype(x.dtype)


def mla_shared_kv(x, w_dkv, kv_gain, pos):
    lat = x @ w_dkv
    ckv = rms_norm(lat[..., :KV_LORA], kv_gain)
    kr = apply_rope(lat[..., KV_LORA:][:, :, None, :], pos)[:, :, 0, :]
    return ckv, kr


def mla_queries(x, w_dq, q_gain, w_uq, pos):
    B, T, _ = x.shape
    q = (rms_norm(x @ w_dq, q_gain) @ w_uq).reshape(B, T, MLA_HEADS, NOPE_DIM + ROPE_DIM)
    return q[..., :NOPE_DIM], apply_rope(q[..., NOPE_DIM:], pos)


def mla_attend(qn, qr, ckv, kr, w_uk, w_uv, mask):
    q_lat = jnp.einsum('bqhn,chn->bqhc', qn, w_uk)
    s = (jnp.einsum('bqhc,bkc->bhqk', q_lat, ckv, preferred_element_type=jnp.float32)
         + jnp.einsum('bqhr,bkr->bhqk', qr, kr, preferred_element_type=jnp.float32)) * MLA_SCALE
    if mask is not None:
        s = jnp.where(mask, s, NEG_INF)
    p = jax.nn.softmax(s, axis=-1).astype(ckv.dtype)
    o_lat = jnp.einsum('bhqk,bkc->bqhc', p, ckv)
    return jnp.einsum('bqhc,chv->bqhv', o_lat, w_uv)


def mla_attend_chunk_causal(qn, qr, ckv, kr, w_uk, w_uv, cid):
    B, L = qn.shape[:2]
    nqb = -(-L // Q_BLOCK)
    Lp = nqb * Q_BLOCK

    def padq(a):
        a = jnp.pad(a, [(0, 0), (0, Lp - L), (0, 0), (0, 0)])
        return jnp.swapaxes(a.reshape((B, nqb, Q_BLOCK) + a.shape[2:]), 0, 1)
    q_cid = jnp.pad(cid, (0, Lp - L), constant_values=2 ** 30).reshape(nqb, Q_BLOCK)

    def block_fn(args):
        qn_b, qr_b, qc = args
        mask = cid[None, :] <= qc[:, None]
        return mla_attend(qn_b, qr_b, ckv, kr, w_uk, w_uv, mask)

    o = lax.map(block_fn, (padq(qn), padq(qr), q_cid))
    return jnp.swapaxes(o, 0, 1).reshape(B, Lp, MLA_HEADS, V_DIM)[:, :L]


def sq_relu_mlp(x, w_up, w_down):
    return jnp.square(jax.nn.relu(x @ w_up)) @ w_down


def trunk(x, pos, conv0, s0, past_ckv, past_kr, cid, p, block):
    is_prompt = past_ckv is None
    B, T, _ = x.shape
    new_conv, new_s = [], []
    ckv = kr = key_ckv = key_kr = None
    for layer in range(DEPTH):
        if layer < N_A_LAYERS:
            h, c, s = gdn_mixer(x, conv0[layer], s0[layer], p['a_w_in'][layer], p['a_conv_w'][layer],
                                p['a_a_log'][layer], p['a_dt_bias'][layer], p['a_o_gain'][layer],
                                p['a_w_o'][layer], block)
            new_conv.append(c)
            new_s.append(s)
        else:
            j = layer - N_A_LAYERS
            qn, qr = mla_queries(x, p['b_w_dq'][j], p['b_q_gain'][j], p['b_w_uq'][j], pos)
            if is_prompt:
                o = mla_attend_chunk_causal(qn, qr, key_ckv, key_kr, p['kv_w_uk'], p['kv_w_uv'], cid)
            else:
                o = mla_attend(qn, qr, key_ckv, key_kr, p['kv_w_uk'], p['kv_w_uv'], None)
            h = o.reshape(B, T, MLA_HEADS * V_DIM) @ p['b_w_o'][j]
        x = layer_norm(DN_ALPHA * x + h, p['ln_gain'][layer, 0], p['ln_bias'][layer, 0])
        x = layer_norm(DN_ALPHA * x + sq_relu_mlp(x, p['mlp_w_up'][layer], p['mlp_w_down'][layer]),
                       p['ln_gain'][layer, 1], p['ln_bias'][layer, 1])
        if layer == N_A_LAYERS - 1:
            ckv, kr = mla_shared_kv(x, p['kv_w_dkv'], p['kv_gain'], pos)
            if is_prompt:
                key_ckv, key_kr = ckv, kr
            else:
                key_ckv = jnp.concatenate([past_ckv.astype(ckv.dtype), ckv], axis=1)
                key_kr = jnp.concatenate([past_kr.astype(kr.dtype), kr], axis=1)
    return x, jnp.stack(new_conv), jnp.stack(new_s), ckv, kr


def setup_inputs(seed: int = 0) -> dict:
    key = jax.random.key(seed)
    ks = jax.random.split(key, 26)
    f32 = jnp.float32

    def nrm(k, shape, scale):
        return jax.random.normal(k, shape, f32) * scale

    W, H, DK = GDN_WIDTH, GDN_HEADS, GDN_HEAD_DIM
    dt = jnp.exp(jax.random.uniform(ks[10], (N_A_LAYERS, H), f32, math.log(1e-3), math.log(1e-1)))
    return {
        'x_prompt': nrm(ks[0], (BATCH, SEQ, D_MODEL), 1.0),
        'x_sample': nrm(ks[1], (DEC_BATCH, DEC_SEQ, D_MODEL), 1.0),
        'state_conv': nrm(ks[2], (N_A_LAYERS, DEC_BATCH, CONV_WIDTH - 1, 3 * W), 1.0),
        'state_delta': nrm(ks[3], (N_A_LAYERS, DEC_BATCH, H, DK, DK), 0.05),
        'cache_ckv': nrm(ks[4], (DEC_BATCH, PAST_LEN, KV_LORA), 1.0),
        'cache_krope': nrm(ks[5], (DEC_BATCH, PAST_LEN, ROPE_DIM), 1.0),
        'meta_tokens': nrm(ks[6], (N_META, D_MODEL), 1.0),
        'a_w_in': nrm(ks[7], (N_A_LAYERS, D_MODEL, 4 * W + 2 * H), D_MODEL ** -0.5),
        'a_conv_w': nrm(ks[8], (N_A_LAYERS, CONV_WIDTH, 3 * W), CONV_WIDTH ** -0.5),
        'a_a_log': jnp.log(jax.random.uniform(ks[9], (N_A_LAYERS, H), f32, 1.0, 16.0)),
        'a_dt_bias': dt + jnp.log(-jnp.expm1(-dt)),
        'a_o_gain': 1.0 + nrm(ks[11], (N_A_LAYERS, DK), 0.02),
        'a_w_o': nrm(ks[12], (N_A_LAYERS, W, D_MODEL), W ** -0.5 * DN_BETA),
        'b_w_dq': nrm(ks[13], (N_B_LAYERS, D_MODEL, Q_LORA), D_MODEL ** -0.5),
        'b_q_gain': 1.0 + nrm(ks[14], (N_B_LAYERS, Q_LORA), 0.02),
        'b_w_uq': nrm(ks[15], (N_B_LAYERS, Q_LORA, MLA_HEADS * (NOPE_DIM + ROPE_DIM)), Q_LORA ** -0.5),
        'b_w_o': nrm(ks[16], (N_B_LAYERS, MLA_HEADS * V_DIM, D_MODEL), (MLA_HEADS * V_DIM) ** -0.5 * DN_BETA),
        'kv_w_dkv': nrm(ks[17], (D_MODEL, KV_LORA + ROPE_DIM), D_MODEL ** -0.5),
        'kv_gain': 1.0 + nrm(ks[18], (KV_LORA,), 0.02),
        'kv_w_uk': nrm(ks[19], (KV_LORA, MLA_HEADS, NOPE_DIM), KV_LORA ** -0.5),
        'kv_w_uv': nrm(ks[20], (KV_LORA, MLA_HEADS, V_DIM), KV_LORA ** -0.5),
        'mlp_w_up': nrm(ks[21], (DEPTH, D_MODEL, D_FF), D_MODEL ** -0.5),
        'mlp_w_down': nrm(ks[22], (DEPTH, D_FF, D_MODEL), D_FF ** -0.5 * DN_BETA),
        'ln_gain': 1.0 + nrm(ks[23], (DEPTH, 2, D_MODEL), 0.02),
        'ln_bias': nrm(ks[24], (DEPTH, 2, D_MODEL), 0.02),
    }


def reference(x_prompt, x_sample, state_conv, state_delta, cache_ckv, cache_krope, meta_tokens,
              a_w_in, a_conv_w, a_a_log, a_dt_bias, a_o_gain, a_w_o,
              b_w_dq, b_q_gain, b_w_uq, b_w_o,
              kv_w_dkv, kv_gain, kv_w_uk, kv_w_uv,
              mlp_w_up, mlp_w_down, ln_gain, ln_bias):
    p = {
        'a_w_in': a_w_in, 'a_conv_w': a_conv_w, 'a_a_log': a_a_log, 'a_dt_bias': a_dt_bias,
        'a_o_gain': a_o_gain, 'a_w_o': a_w_o,
        'b_w_dq': b_w_dq, 'b_q_gain': b_q_gain, 'b_w_uq': b_w_uq, 'b_w_o': b_w_o,
        'kv_w_dkv': kv_w_dkv, 'kv_gain': kv_gain, 'kv_w_uk': kv_w_uk, 'kv_w_uv': kv_w_uv,
        'mlp_w_up': mlp_w_up, 'mlp_w_down': mlp_w_down, 'ln_gain': ln_gain, 'ln_bias': ln_bias,
    }
    B = x_prompt.shape[0]
    meta = jnp.broadcast_to(meta_tokens.astype(x_prompt.dtype)[None], (B, N_META, D_MODEL))
    xp = jnp.concatenate([meta, x_prompt], axis=1)
    L = xp.shape[1]
    pos_p = jnp.arange(L, dtype=jnp.float32)
    i = jnp.arange(L)
    cid = jnp.where(i < N_META, 0, 1 + (i - N_META) // CHUNK)
    conv0 = jnp.zeros((N_A_LAYERS, B, CONV_WIDTH - 1, 3 * GDN_WIDTH), xp.dtype)
    s0 = jnp.zeros((N_A_LAYERS, B, GDN_HEADS, GDN_HEAD_DIM, GDN_HEAD_DIM), xp.dtype)
    yp, conv_p, delta_p, ckv_p, kr_p = trunk(xp, pos_p, conv0, s0, None, None, cid, p, CHUNK)
    y_prompt = yp[:, N_META:]
    Ts = x_sample.shape[1]
    pos_s = cache_ckv.shape[1] + jnp.arange(Ts, dtype=jnp.float32)
    y_sample, conv_s, delta_s, ckv_s, kr_s = trunk(x_sample, pos_s, state_conv, state_delta,
                                                   cache_ckv, cache_krope, None, p, Ts)
    return (y_prompt, y_sample, conv_p, delta_p, ckv_p, kr_p, conv_s, delta_s, ckv_s, kr_s)
```

```python
import functools
import math

import jax
import jax.numpy as jnp
from jax import lax
from jax.experimental import pallas as pl
from jax.experimental.pallas import tpu as pltpu

CHUNK = 64
ROPE_THETA = 10000.0
NORM_EPS = 1e-6
NEG_INF = -1e30
LANES = 128
SUBLANES = 8
VMEM_LIMIT_BYTES = 56 * 1024 * 1024

F32 = jnp.float32
BF16 = jnp.bfloat16
_HI = lax.Precision.HIGHEST


def _round_up(x, m):
    return -(-x // m) * m


def _pick_block(dim, target, align):
    best = None
    for d in range(align, min(dim, target) + 1, align):
        if dim % d == 0:
            best = d
    return best if best is not None else dim


def _dot(a, b):
    return jnp.dot(a, b, preferred_element_type=F32)


def _dot_nt(a, b):
    return lax.dot_general(a, b, (((1,), (1,)), ((), ())), preferred_element_type=F32)


def _dot_tn(a, b):
    return lax.dot_general(a, b, (((0,), (0,)), ((), ())), preferred_element_type=F32)


def _sigmoid(x):
    return 1.0 / (1.0 + jnp.exp(-x))


def _params(sem):
    return pltpu.CompilerParams(dimension_semantics=sem, vmem_limit_bytes=VMEM_LIMIT_BYTES)


def _mm_kernel(*refs, nk, epilogue):
    if epilogue == "rms":
        a_ref, b_ref, g_ref, o_ref = refs[:4]
        rest = refs[4:]
    else:
        a_ref, b_ref, o_ref = refs[:3]
        g_ref = None
        rest = refs[3:]

    def finish(acc):
        if epilogue == "relu2":
            r = jnp.maximum(acc, 0.0)
            acc = r * r
        elif epilogue == "rms":
            ms = jnp.mean(acc * acc, axis=-1, keepdims=True)
            acc = acc * lax.rsqrt(ms + NORM_EPS) * g_ref[...]
        o_ref[...] = acc.astype(o_ref.dtype)

    if nk == 1:
        finish(_dot(a_ref[...], b_ref[...]))
    else:
        acc_ref = rest[0]
        k = pl.program_id(2)

        @pl.when(k == 0)
        def _():
            acc_ref[...] = jnp.zeros_like(acc_ref)

        acc_ref[...] += _dot(a_ref[...], b_ref[...])

        @pl.when(k == nk - 1)
        def _():
            finish(acc_ref[...])


def _matmul(a, b, out_dtype, *, epilogue="none", gain=None, bm=1312, bn=512, bk=4096):
    m, kdim = a.shape
    n = b.shape[1]
    bm = _pick_block(m, bm, 16)
    bn = n if epilogue == "rms" else _pick_block(n, bn, LANES)
    bk = _pick_block(kdim, bk, LANES)
    nk = kdim // bk
    in_specs = [pl.BlockSpec((bm, bk), lambda i, j, k: (i, k)),
                pl.BlockSpec((bk, bn), lambda i, j, k: (k, j))]
    args = [a, b]
    if epilogue == "rms":
        in_specs.append(pl.BlockSpec((1, bn), lambda i, j, k: (0, j)))
        args.append(gain.reshape(1, n).astype(F32))
    scratch = [pltpu.VMEM((bm, bn), F32)] if nk > 1 else []
    return pl.pallas_call(
        functools.partial(_mm_kernel, nk=nk, epilogue=epilogue),
        out_shape=jax.ShapeDtypeStruct((m, n), out_dtype),
        grid=(m // bm, n // bn, nk),
        in_specs=in_specs,
        out_specs=pl.BlockSpec((bm, bn), lambda i, j, k: (i, j)),
        scratch_shapes=scratch,
        compiler_params=_params(("parallel", "parallel", "arbitrary")),
        name="matmul_" + epilogue,
    )(*args)


def _ln_kernel(x_ref, h_ref, g_ref, b_ref, of_ref, ob_ref, *, alpha):
    y = alpha * x_ref[...] + h_ref[...]
    mu = jnp.mean(y, axis=-1, keepdims=True)
    d = y - mu
    var = jnp.mean(d * d, axis=-1, keepdims=True)
    out = d * lax.rsqrt(var + NORM_EPS) * g_ref[...] + b_ref[...]
    of_ref[...] = out
    ob_ref[...] = out.astype(BF16)


def _deepnorm(x, h, gain, bias, alpha):
    m, d = x.shape
    br = _pick_block(m, 256, 16)
    row = pl.BlockSpec((br, d), lambda i: (i, 0))
    vec = pl.BlockSpec((1, d), lambda i: (0, 0))
    return pl.pallas_call(
        functools.partial(_ln_kernel, alpha=alpha),
        out_shape=(jax.ShapeDtypeStruct((m, d), F32), jax.ShapeDtypeStruct((m, d), BF16)),
        grid=(m // br,),
        in_specs=[row, row, vec, vec],
        out_specs=(row, row),
        compiler_params=_params(("parallel",)),
        name="deepnorm",
    )(x, h, gain.reshape(1, d), bias.reshape(1, d))


def _kv_post_kernel(lat_ref, g_ref, cos_ref, sin_ref, o_ref, *, kv):
    c = lat_ref[:, :kv]
    ms = jnp.mean(c * c, axis=-1, keepdims=True)
    o_ref[:, :kv] = c * lax.rsqrt(ms + NORM_EPS) * g_ref[...]
    r = lat_ref[:, kv:]
    o_ref[:, kv:] = r * cos_ref[...] + pltpu.roll(r, LANES // 2, axis=1) * sin_ref[...]


def _kv_post(lat, gain, cos_t, sin_t, kv):
    m, w = lat.shape
    br = _pick_block(m, 256, 8)
    return pl.pallas_call(
        functools.partial(_kv_post_kernel, kv=kv),
        out_shape=jax.ShapeDtypeStruct((m, w), F32),
        grid=(m // br,),
        in_specs=[pl.BlockSpec((br, w), lambda i: (i, 0)),
                  pl.BlockSpec((1, kv), lambda i: (0, 0)),
                  pl.BlockSpec((br, LANES), lambda i: (i, 0)),
                  pl.BlockSpec((br, LANES), lambda i: (i, 0))],
        out_specs=pl.BlockSpec((br, w), lambda i: (i, 0)),
        compiler_params=_params(("parallel",)),
        name="kv_post",
    )(lat, gain.reshape(1, kv), cos_t, sin_t)


def _attn_kernel(q_ref, cos_ref, sin_ref, k_ref, v_ref, o_ref, *, bq, bk, nope, scale, causal,
                 n_keys, n_meta, lk_pad):
    qi = pl.program_id(2)
    q = q_ref[...].astype(F32)
    qr = q[:, nope:]
    qr = qr * cos_ref[...] + pltpu.roll(qr, LANES // 2, axis=1) * sin_ref[...]
    qf = (jnp.concatenate([q[:, :nope], qr], axis=1) * scale).astype(BF16)

    row = qi * bq + lax.broadcasted_iota(jnp.int32, (bq, 1), 0)
    if causal:
        vis = jnp.where(row < n_meta, n_meta, ((row - n_meta) // CHUNK + 1) * CHUNK + n_meta)
        last = qi * bq + bq - 1
        vis_max = jnp.where(last < n_meta, n_meta, ((last - n_meta) // CHUNK + 1) * CHUNK + n_meta)
        nkb = (jnp.minimum(vis_max, lk_pad) + bk - 1) // bk
    else:
        vis = jnp.full((bq, 1), n_keys, jnp.int32)
        nkb = lk_pad // bk

    def body(j, carry):
        m, l, acc = carry
        start = pl.multiple_of(j * bk, bk)
        s = _dot_nt(qf, k_ref[pl.ds(start, bk), :])
        kpos = start + lax.broadcasted_iota(jnp.int32, (1, bk), 1)
        s = jnp.where(kpos < vis, s, NEG_INF)
        m_new = jnp.maximum(m, jnp.max(s, axis=-1, keepdims=True))
        a = jnp.exp(m - m_new)
        p = jnp.exp(s - m_new)
        l = a * l + jnp.sum(p, axis=-1, keepdims=True)
        acc = a * acc + _dot(p.astype(BF16), v_ref[pl.ds(start, bk), :])
        return m_new, l, acc

    vd = v_ref.shape[1]
    init = (jnp.full((bq, 1), NEG_INF, F32), jnp.zeros((bq, 1), F32), jnp.zeros((bq, vd), F32))
    _, l, acc = lax.fori_loop(0, nkb, body, init)
    o_ref[...] = (acc / l).astype(o_ref.dtype)


def _attention(q, cos_t, sin_t, kf, vf, *, nb, lq, lk_pad, row0, heads, nope, vd, bq, bk, scale,
               causal, n_keys, n_meta):
    dqk = nope + LANES
    nq = lq // bq
    blk0 = row0 // bq
    qmap = lambda b, h, i: (blk0 + b * nq + i, h)
    tmap = lambda b, h, i: (blk0 + b * nq + i, 0)
    return pl.pallas_call(
        functools.partial(_attn_kernel, bq=bq, bk=bk, nope=nope, scale=scale, causal=causal,
                          n_keys=n_keys, n_meta=n_meta, lk_pad=lk_pad),
        out_shape=jax.ShapeDtypeStruct((nb * lq, heads * vd), BF16),
        grid=(nb, heads, nq),
        in_specs=[pl.BlockSpec((bq, dqk), qmap),
                  pl.BlockSpec((bq, LANES), tmap),
                  pl.BlockSpec((bq, LANES), tmap),
                  pl.BlockSpec((lk_pad, dqk), lambda b, h, i: (b, h)),
                  pl.BlockSpec((lk_pad, vd), lambda b, h, i: (b, h))],
        out_specs=pl.BlockSpec((bq, vd), lambda b, h, i: (b * nq + i, h)),
        compiler_params=_params(("parallel", "parallel", "arbitrary")),
        name="mla_attention_causal" if causal else "mla_attention_full",
    )(q, cos_t, sin_t, kf, vf)


def _item_info(i, n_p, cps, cs, l_valid):
    is_p = i < n_p
    j = i - n_p
    seq = jnp.where(is_p, i // cps, n_p // cps + j // cs)
    c = jnp.where(is_p, i % cps, j % cs)
    nc = jnp.where(is_p, cps, cs)
    nvalid = jnp.where(is_p, jnp.clip(l_valid - c * CHUNK, 0, CHUNK), CHUNK)
    return seq, c, nc, nvalid


def _tri_inverse(a):
    n = a.shape[0]
    r = lax.broadcasted_iota(jnp.int32, (n, n), 0)
    c = lax.broadcasted_iota(jnp.int32, (n, n), 1)
    p = jnp.where(r == c, 1.0, 0.0) - a
    ak = a
    power = 2
    while power < n:
        ak = jnp.dot(ak, ak, precision=_HI, preferred_element_type=F32)
        p = p + jnp.dot(p, ak, precision=_HI, preferred_element_type=F32)
        power *= 2
    return p


def _gdn_pre_kernel(qc_ref, kc_ref, vc_ref, qp_ref, kp_ref, vp_ref, q0_ref, k0_ref, v0_ref,
                    wq_ref, wk_ref, wv_ref, ab_ref, alog_ref, dtb_ref,
                    val_ref, kcd_ref, qdec_ref, kdec_ref, qk_ref, eg_ref,
                    bufq, bufk, bufv, *, hg, dk, n_p, cps, cs, l_valid, width):
    C = CHUNK
    i = pl.program_id(1)
    _, c, _, nvalid = _item_info(i, n_p, cps, cs, l_valid)
    first = c == 0

    def conv_silu(cur_ref, prev_ref, init_ref, w_ref, buf):
        buf[0:SUBLANES, :] = jnp.where(first, init_ref[0], prev_ref[...])
        buf[SUBLANES:SUBLANES + C, :] = cur_ref[...]
        base = SUBLANES - (width - 1)
        y = buf[base:base + C, :] * w_ref[0:1, :]
        for t in range(1, width):
            y = y + buf[base + t:base + t + C, :] * w_ref[t:t + 1, :]
        return y * _sigmoid(y)

    qs = conv_silu(qc_ref, qp_ref, q0_ref, wq_ref, bufq)
    ks = conv_silu(kc_ref, kp_ref, k0_ref, wk_ref, bufk)
    vs = conv_silu(vc_ref, vp_ref, v0_ref, wv_ref, bufv)

    ab = ab_ref[...]
    rows = lax.broadcasted_iota(jnp.int32, (C, LANES), 0)
    valid = rows < nvalid
    xs = ab + dtb_ref[0]
    softplus = jnp.maximum(xs, 0.0) + jnp.log(1.0 + jnp.exp(-jnp.abs(xs)))
    g = jnp.where(valid, -jnp.exp(alog_ref[0]) * softplus, 0.0)
    beta = jnp.where(valid, _sigmoid(ab), 0.0)
    r = lax.broadcasted_iota(jnp.int32, (C, C), 0)
    cc = lax.broadcasted_iota(jnp.int32, (C, C), 1)
    incl = r >= cc
    strict = r > cc
    gcum = jnp.dot(jnp.where(incl, 1.0, 0.0), g, precision=_HI, preferred_element_type=F32)
    gcum_t = gcum.T

    qk_parts = []
    for h in range(hg):
        sl = slice(h * dk, (h + 1) * dk)
        qh, kh, vh = qs[:, sl], ks[:, sl], vs[:, sl]
        qn = qh * lax.rsqrt(jnp.sum(qh * qh, axis=-1, keepdims=True) + NORM_EPS) * (dk ** -0.5)
        kn = kh * lax.rsqrt(jnp.sum(kh * kh, axis=-1, keepdims=True) + NORM_EPS)
        bcol = beta[:, hg + h:hg + h + 1]
        gcol = gcum[:, h:h + 1]
        grow = gcum_t[h:h + 1, :]
        glast = gcum[C - 1:C, h:h + 1]
        eg = jnp.exp(gcol)
        decay = jnp.where(incl, jnp.exp(jnp.where(incl, gcol - grow, 0.0)), 0.0)
        kb = kn * bcol
        knb = kn.astype(BF16)
        a_mat = jnp.where(strict, _dot_nt(kb.astype(BF16), knb) * decay, 0.0)
        t_mat = _tri_inverse(a_mat)
        rhs = jnp.concatenate([vh * bcol, kb * eg], axis=1).astype(BF16)
        tv = _dot(t_mat.astype(BF16), rhs)
        val_ref[:, sl] = tv[:, :dk]
        kcd_ref[:, sl] = tv[:, dk:].astype(BF16)
        qdec_ref[:, sl] = (qn * eg).astype(BF16)
        kdec_ref[:, sl] = (kn * jnp.exp(glast - gcol)).astype(BF16)
        qk_parts.append(_dot_nt(qn.astype(BF16), knb) * decay)
        eg_ref[0, 0, h:h + 1, :] = jnp.broadcast_to(jnp.exp(glast), (1, LANES))
    qk_ref[...] = jnp.concatenate(qk_parts, axis=1)


def _gdn_scan_kernel(val_ref, kcd_ref, qdec_ref, kdec_ref, qk_ref, eg_ref, z_ref, gain_ref, s0_ref,
                     o_ref, sout_ref, s_ref, *, hg, dk, n_p, cps, cs, l_valid):
    C = CHUNK
    i = pl.program_id(1)
    _, c, nc, _ = _item_info(i, n_p, cps, cs, l_valid)

    @pl.when(c == 0)
    def _():
        s_ref[...] = s0_ref[0]

    qk_all = qk_ref[...]
    gain = gain_ref[...]
    for h in range(hg):
        sl = slice(h * dk, (h + 1) * dk)
        s_h = s_ref[h]
        sb = s_h.astype(BF16)
        u = val_ref[:, sl] - _dot(kcd_ref[:, sl], sb)
        ub = u.astype(BF16)
        o = _dot(qdec_ref[:, sl], sb) + _dot(qk_all[:, h * C:(h + 1) * C].astype(BF16), ub)
        s_ref[h] = s_h * eg_ref[0, 0, h:h + 1, :] + _dot_tn(kdec_ref[:, sl], ub)
        z = z_ref[:, sl]
        ms = jnp.mean(o * o, axis=-1, keepdims=True)
        o_ref[:, sl] = (o * lax.rsqrt(ms + NORM_EPS) * gain * (z * _sigmoid(z))).astype(o_ref.dtype)

    @pl.when(c == nc - 1)
    def _():
        sout_ref[0] = s_ref[...]


def _gdn_mixer(proj, conv0, s0, conv_w, a_log, dt_bias, o_gain, *, heads, dk, hg, n_p, cps, cs, l_valid):
    m = proj.shape[0]
    C = CHUNK
    w = heads * dk
    n_hg = heads // hg
    gw = hg * dk
    n_items = m // C
    n_seq = s0.shape[0]
    width = conv_w.shape[0]
    info = dict(n_p=n_p, cps=cps, cs=cs, l_valid=l_valid)

    def seq_of(i):
        return _item_info(i, n_p, cps, cs, l_valid)[0]

    cur = lambda off: pl.BlockSpec((C, gw), lambda g, i: (i, off * n_hg + g))
    prev = lambda off: pl.BlockSpec(
        (SUBLANES, gw), lambda g, i: (jnp.maximum(i * (C // SUBLANES) - 1, 0), off * n_hg + g))
    init = lambda off: pl.BlockSpec((1, SUBLANES, gw), lambda g, i: (seq_of(i), 0, off * n_hg + g))
    wspec = lambda off: pl.BlockSpec((width, gw), lambda g, i: (0, off * n_hg + g))
    gate_vec = pl.BlockSpec((1, 1, LANES), lambda g, i: (g, 0, 0))
    tok = pl.BlockSpec((C, gw), lambda g, i: (i, g))

    alog_g = jnp.pad(a_log.reshape(n_hg, 1, hg).astype(F32), ((0, 0), (0, 0), (0, LANES - hg)))
    dtb_g = jnp.pad(dt_bias.reshape(n_hg, 1, hg).astype(F32), ((0, 0), (0, 0), (0, LANES - hg)))

    val, kcd, qdec, kdec, qk, eg = pl.pallas_call(
        functools.partial(_gdn_pre_kernel, hg=hg, dk=dk, width=width, **info),
        out_shape=(jax.ShapeDtypeStruct((m, w), F32),
                   jax.ShapeDtypeStruct((m, w), BF16),
                   jax.ShapeDtypeStruct((m, w), BF16),
                   jax.ShapeDtypeStruct((m, w), BF16),
                   jax.ShapeDtypeStruct((m, heads * C), F32),
                   jax.ShapeDtypeStruct((n_items, n_hg, hg, LANES), F32)),
        grid=(n_hg, n_items),
        in_specs=[cur(0), cur(1), cur(2), prev(0), prev(1), prev(2), init(0), init(1), init(2),
                  wspec(0), wspec(1), wspec(2),
                  pl.BlockSpec((C, LANES), lambda g, i: (i, 4 * w // LANES + g)),
                  gate_vec, gate_vec],
        out_specs=(tok, tok, tok, tok,
                   pl.BlockSpec((C, hg * C), lambda g, i: (i, g)),
                   pl.BlockSpec((1, 1, hg, LANES), lambda g, i: (i, g, 0, 0))),
        scratch_shapes=[pltpu.VMEM((SUBLANES + C, gw), F32)] * 3,
        compiler_params=_params(("parallel", "parallel")),
        name="gdn_prepare",
    )(proj, proj, proj, proj, proj, proj, conv0, conv0, conv0, conv_w, conv_w, conv_w,
      proj, alog_g, dtb_g)

    o, s_new = pl.pallas_call(
        functools.partial(_gdn_scan_kernel, hg=hg, dk=dk, **info),
        out_shape=(jax.ShapeDtypeStruct((m, w), BF16),
                   jax.ShapeDtypeStruct((n_seq, heads, dk, dk), F32)),
        grid=(n_hg, n_items),
        in_specs=[tok, tok, tok, tok,
                  pl.BlockSpec((C, hg * C), lambda g, i: (i, g)),
                  pl.BlockSpec((1, 1, hg, LANES), lambda g, i: (i, g, 0, 0)),
                  pl.BlockSpec((C, gw), lambda g, i: (i, 3 * n_hg + g)),
                  pl.BlockSpec((1, dk), lambda g, i: (0, 0)),
                  pl.BlockSpec((1, hg, dk, dk), lambda g, i: (seq_of(i), g, 0, 0))],
        out_specs=(tok, pl.BlockSpec((1, hg, dk, dk), lambda g, i: (seq_of(i), g, 0, 0))),
        scratch_shapes=[pltpu.VMEM((hg, dk, dk), F32)],
        compiler_params=_params(("parallel", "arbitrary")),
        name="gdn_scan",
    )(val, kcd, qdec, kdec, qk, eg, proj, o_gain.reshape(1, dk).astype(F32), s0)
    return o, s_new


def _split_half_layout(x, half):
    pad = [(0, 0)] * (x.ndim - 1) + [(0, LANES // 2 - half)]
    return jnp.concatenate([jnp.pad(x[..., :half], pad), jnp.pad(x[..., half:], pad)], axis=-1)


def _rope_tables(pos, half):
    inv_freq = 1.0 / (ROPE_THETA ** (jnp.arange(half, dtype=F32) / half))
    ang = pos[:, None] * inv_freq[None, :]
    cos, sin = jnp.cos(ang), jnp.sin(ang)
    cos_t = _split_half_layout(jnp.concatenate([cos, cos], -1), half)
    sin_t = _split_half_layout(jnp.concatenate([-sin, sin], -1), half)
    return cos_t, sin_t


def kernel(x_prompt, x_sample, state_conv, state_delta, cache_ckv, cache_krope, meta_tokens,
           a_w_in, a_conv_w, a_a_log, a_dt_bias, a_o_gain, a_w_o,
           b_w_dq, b_q_gain, b_w_uq, b_w_o,
           kv_w_dkv, kv_gain, kv_w_uk, kv_w_uv,
           mlp_w_up, mlp_w_down, ln_gain, ln_bias):
    B, seq, D = x_prompt.shape
    DB, ts, _ = x_sample.shape
    n_meta = meta_tokens.shape[0]
    depth = ln_gain.shape[0]
    n_a = a_w_in.shape[0]
    H = a_a_log.shape[1]
    dk = a_o_gain.shape[1]
    W = H * dk
    width = a_conv_w.shape[1]
    past = cache_ckv.shape[1]
    KV, MH, nope = kv_w_uk.shape
    vd = kv_w_uv.shape[2]
    rope = cache_krope.shape[2]
    half = rope // 2
    alpha = (2 * depth) ** 0.25
    scale = (nope + rope) ** -0.5
    assert ts % CHUNK == 0 and ts >= width - 1 and half <= LANES // 2
    assert nope % LANES == 0 and vd % LANES == 0 and KV % LANES == 0 and dk % LANES == 0

    L = n_meta + seq
    LP = _round_up(L, LANES)
    n_prompt_rows = B * LP
    M = n_prompt_rows + DB * ts
    cps, cs = LP // CHUNK, ts // CHUNK
    n_p = B * cps
    hg = min(4, H)
    n_hg = H // hg

    meta = jnp.broadcast_to(meta_tokens.astype(F32)[None], (B, n_meta, D))
    xp = jnp.concatenate([meta, x_prompt, jnp.zeros((B, LP - L, D), F32)], axis=1)
    x = jnp.concatenate([xp.reshape(n_prompt_rows, D), x_sample.reshape(DB * ts, D)], axis=0)
    xb = x.astype(BF16)

    pos = jnp.concatenate([jnp.tile(jnp.arange(LP, dtype=F32), B),
                           jnp.tile(past + jnp.arange(ts, dtype=F32), DB)])
    cos_t, sin_t = _rope_tables(pos, half)

    conv0 = jnp.concatenate([jnp.zeros((n_a, B, width - 1, 3 * W), F32), state_conv.astype(F32)], axis=1)
    conv0 = jnp.pad(conv0, ((0, 0), (0, 0), (SUBLANES - (width - 1), 0), (0, 0)))
    s0 = jnp.concatenate([jnp.zeros((n_a, B, H, dk, dk), F32), state_delta.astype(F32)], axis=1)

    new_conv_p, new_conv_s, new_s = [], [], []
    k_slabs = None
    lat = None
    for layer in range(depth):
        if layer < n_a:
            w_in = a_w_in[layer]
            gate_w = jnp.concatenate(
                [w_in[:, 4 * W:4 * W + H].reshape(D, n_hg, hg), w_in[:, 4 * W + H:].reshape(D, n_hg, hg),
                 jnp.zeros((D, n_hg, LANES - 2 * hg), w_in.dtype)], axis=-1).reshape(D, n_hg * LANES)
            w_ext = jnp.concatenate([w_in[:, :4 * W], gate_w], axis=1).astype(BF16)
            proj = _matmul(xb, w_ext, F32)
            o, s_fin = _gdn_mixer(proj, conv0[layer], s0[layer], a_conv_w[layer].astype(F32),
                                  a_a_log[layer], a_dt_bias[layer], a_o_gain[layer],
                                  heads=H, dk=dk, hg=hg, n_p=n_p, cps=cps, cs=cs, l_valid=L)
            h = _matmul(o, a_w_o[layer].astype(BF16), F32)
            pre = proj[:, :3 * W]
            new_conv_p.append(pre[:n_prompt_rows].reshape(B, LP, 3 * W)[:, L - (width - 1):L])
            new_conv_s.append(pre[n_prompt_rows:].reshape(DB, ts, 3 * W)[:, ts - (width - 1):])
            new_s.append(s_fin)
        else:
            j = layer - n_a
            ql = _matmul(xb, b_w_dq[j].astype(BF16), BF16, epilogue="rms", gain=b_q_gain[j])
            w_uq = b_w_uq[j].reshape(-1, MH, nope + rope)
            w_uq = jnp.concatenate([w_uq[..., :nope], _split_half_layout(w_uq[..., nope:], half)], axis=-1)
            q = _matmul(ql, w_uq.reshape(-1, MH * (nope + LANES)).astype(BF16), BF16)
            kp, vp, ks, vs, lkp, lks = k_slabs
            o_p = _attention(q, cos_t, sin_t, kp, vp, nb=B, lq=LP, lk_pad=lkp, row0=0, heads=MH,
                             nope=nope, vd=vd, bq=_pick_block(LP, 384, LANES), bk=512, scale=scale,
                             causal=True, n_keys=L, n_meta=n_meta)
            o_s = _attention(q, cos_t, sin_t, ks, vs, nb=DB, lq=ts, lk_pad=lks, row0=n_prompt_rows,
                             heads=MH, nope=nope, vd=vd, bq=ts, bk=lks, scale=scale,
                             causal=False, n_keys=past + ts, n_meta=n_meta)
            h = _matmul(jnp.concatenate([o_p, o_s], axis=0), b_w_o[j].astype(BF16), F32)

        x, xb = _deepnorm(x, h, ln_gain[layer, 0], ln_bias[layer, 0], alpha)
        hid = _matmul(xb, mlp_w_up[layer].astype(BF16), BF16, epilogue="relu2")
        h = _matmul(hid, mlp_w_down[layer].astype(BF16), F32, bn=1024, bk=2048)
        x, xb = _deepnorm(x, h, ln_gain[layer, 1], ln_bias[layer, 1], alpha)

        if layer == n_a - 1:
            w_dkv = jnp.concatenate([kv_w_dkv[:, :KV], _split_half_layout(kv_w_dkv[:, KV:], half)], axis=1)
            lat = _kv_post(_matmul(xb, w_dkv.astype(BF16), F32), kv_gain.astype(F32), cos_t, sin_t, KV)
            eye = jnp.eye(LANES, dtype=F32)
            w_k = jnp.concatenate(
                [jnp.concatenate([kv_w_uk.astype(F32), jnp.zeros((KV, MH, LANES), F32)], axis=-1),
                 jnp.concatenate([jnp.zeros((LANES, MH, nope), F32),
                                  jnp.broadcast_to(eye[:, None, :], (LANES, MH, LANES))], axis=-1)],
                axis=0).reshape(KV + LANES, MH * (nope + LANES)).astype(BF16)
            w_v = jnp.concatenate([kv_w_uv.reshape(KV, MH * vd).astype(F32),
                                   jnp.zeros((LANES, MH * vd), F32)], axis=0).astype(BF16)
            lkp = _round_up(L, 512)
            lat_p = lat[:n_prompt_rows].reshape(B, LP, KV + LANES)
            if lkp >= LP:
                lat_p = jnp.pad(lat_p, ((0, 0), (0, lkp - LP), (0, 0)))
            else:
                lat_p = lat_p[:, :lkp]
            lat_p = lat_p.reshape(B * lkp, KV + LANES).astype(BF16)
            lks = _round_up(past + ts, LANES)
            cache = jnp.concatenate([cache_ckv.astype(F32), _split_half_layout(cache_krope.astype(F32), half)],
                                    axis=-1)
            lat_s = jnp.concatenate([cache, lat[n_prompt_rows:].reshape(DB, ts, KV + LANES),
                                     jnp.zeros((DB, lks - past - ts, KV + LANES), F32)], axis=1)
            lat_s = lat_s.reshape(DB * lks, KV + LANES).astype(BF16)
            k_slabs = (_matmul(lat_p, w_k, BF16, bm=1024), _matmul(lat_p, w_v, BF16, bm=1024),
                       _matmul(lat_s, w_k, BF16, bm=1024), _matmul(lat_s, w_v, BF16, bm=1024), lkp, lks)

    def unsplit(r):
        return jnp.concatenate([r[..., :half], r[..., LANES // 2:LANES // 2 + half]], axis=-1)

    lat_p = lat[:n_prompt_rows].reshape(B, LP, KV + LANES)[:, :L]
    lat_s = lat[n_prompt_rows:].reshape(DB, ts, KV + LANES)
    y_prompt = x[:n_prompt_rows].reshape(B, LP, D)[:, n_meta:L]
    y_sample = x[n_prompt_rows:].reshape(DB, ts, D)
    s_all = jnp.stack(new_s)
    return (y_prompt, y_sample,
            jnp.stack(new_conv_p), s_all[:, :B],
            lat_p[..., :KV], unsplit(lat_p[..., KV:]),
            jnp.stack(new_conv_s), s_all[:, B:],
            lat_s[..., :KV], unsplit(lat_s[..., KV:]))
```

```python
import functools
import math

import jax
import jax.numpy as jnp
from jax import lax
from jax.experimental import pallas as pl
from jax.experimental.pallas import tpu as pltpu

CHUNK = 64
ROPE_THETA = 10000.0
NORM_EPS = 1e-6
NEG_INF = -1e30
LANES = 128
SUBLANES = 8
VMEM_LIMIT_BYTES = 56 * 1024 * 1024

F32 = jnp.float32
BF16 = jnp.bfloat16
_HI = lax.Precision.HIGHEST


def _round_up(x, m):
    return -(-x // m) * m


def _pick_block(dim, target, align):
    best = None
    for d in range(align, min(dim, target) + 1, align):
        if dim % d == 0:
            best = d
    return best if best is not None else dim


def _dot(a, b):
    return jnp.dot(a, b, preferred_element_type=F32)


def _dot_nt(a, b):
    return lax.dot_general(a, b, (((1,), (1,)), ((), ())), preferred_element_type=F32)


def _dot_tn(a, b):
    return lax.dot_general(a, b, (((0,), (0,)), ((), ())), preferred_element_type=F32)


def _sigmoid(x):
    return 1.0 / (1.0 + jnp.exp(-x))


def _params(sem):
    return pltpu.CompilerParams(dimension_semantics=sem, vmem_limit_bytes=VMEM_LIMIT_BYTES)


def _mm_kernel(*refs, nk, epilogue):
    if epilogue == "rms":
        a_ref, b_ref, g_ref, o_ref = refs[:4]
        rest = refs[4:]
    else:
        a_ref, b_ref, o_ref = refs[:3]
        g_ref = None
        rest = refs[3:]

    def finish(acc):
        if epilogue == "relu2":
            r = jnp.maximum(acc, 0.0)
            acc = r * r
        elif epilogue == "rms":
            ms = jnp.mean(acc * acc, axis=-1, keepdims=True)
            acc = acc * lax.rsqrt(ms + NORM_EPS) * g_ref[...]
        o_ref[...] = acc.astype(o_ref.dtype)

    def product():
        return _dot(a_ref[...], b_ref[...].astype(BF16))

    if nk == 1:
        finish(product())
    else:
        acc_ref = rest[0]
        k = pl.program_id(2)

        @pl.when(k == 0)
        def _():
            acc_ref[...] = jnp.zeros_like(acc_ref)

        acc_ref[...] += product()

        @pl.when(k == nk - 1)
        def _():
            finish(acc_ref[...])


def _matmul(a, b, out_dtype, *, layer=None, n_use=None, epilogue="none", gain=None, bm=1312, bn=512, bk=4096):
    m, kdim = a.shape
    n = n_use if n_use is not None else b.shape[-1]
    bm = _pick_block(m, bm, 16)
    bn = n if epilogue == "rms" else _pick_block(n, bn, LANES)
    bk = _pick_block(kdim, bk, LANES)
    nk = kdim // bk
    a_mode = dict(pipeline_mode=pl.Buffered(1)) if nk == 1 else {}
    if b.ndim == 3:
        b_spec = pl.BlockSpec((None, bk, bn), lambda i, j, k: (layer, k, j))
    else:
        b_spec = pl.BlockSpec((bk, bn), lambda i, j, k: (k, j))
    in_specs = [pl.BlockSpec((bm, bk), lambda i, j, k: (i, k), **a_mode), b_spec]
    args = [a, b]
    if epilogue == "rms":
        in_specs.append(pl.BlockSpec((1, bn), lambda i, j, k: (0, j)))
        args.append(gain.reshape(1, n).astype(F32))
    scratch = [pltpu.VMEM((bm, bn), F32)] if nk > 1 else []
    return pl.pallas_call(
        functools.partial(_mm_kernel, nk=nk, epilogue=epilogue),
        out_shape=jax.ShapeDtypeStruct((m, n), out_dtype),
        grid=(m // bm, n // bn, nk),
        in_specs=in_specs,
        out_specs=pl.BlockSpec((bm, bn), lambda i, j, k: (i, j)),
        scratch_shapes=scratch,
        compiler_params=_params(("parallel", "parallel", "arbitrary")),
        name="matmul_" + epilogue,
    )(*args)


def _ln_kernel(x_ref, h_ref, g_ref, b_ref, of_ref, ob_ref, *, alpha):
    y = alpha * x_ref[...] + h_ref[...]
    mu = jnp.mean(y, axis=-1, keepdims=True)
    d = y - mu
    var = jnp.mean(d * d, axis=-1, keepdims=True)
    out = d * lax.rsqrt(var + NORM_EPS) * g_ref[...] + b_ref[...]
    of_ref[...] = out
    ob_ref[...] = out.astype(BF16)


def _deepnorm(x, h, gain, bias, alpha):
    m, d = x.shape
    br = _pick_block(m, 256, 16)
    row = pl.BlockSpec((br, d), lambda i: (i, 0))
    vec = pl.BlockSpec((1, d), lambda i: (0, 0))
    return pl.pallas_call(
        functools.partial(_ln_kernel, alpha=alpha),
        out_shape=(jax.ShapeDtypeStruct((m, d), F32), jax.ShapeDtypeStruct((m, d), BF16)),
        grid=(m // br,),
        in_specs=[row, row, vec, vec],
        out_specs=(row, row),
        compiler_params=_params(("parallel",)),
        name="deepnorm",
    )(x, h, gain.reshape(1, d), bias.reshape(1, d))


def _kv_post_kernel(lat_ref, g_ref, cos_ref, sin_ref, o_ref, *, kv):
    c = lat_ref[:, :kv]
    ms = jnp.mean(c * c, axis=-1, keepdims=True)
    o_ref[:, :kv] = c * lax.rsqrt(ms + NORM_EPS) * g_ref[...]
    r = lat_ref[:, kv:]
    o_ref[:, kv:] = r * cos_ref[...] + pltpu.roll(r, LANES // 2, axis=1) * sin_ref[...]


def _kv_post(lat, gain, cos_t, sin_t, kv):
    m, w = lat.shape
    br = _pick_block(m, 256, 8)
    return pl.pallas_call(
        functools.partial(_kv_post_kernel, kv=kv),
        out_shape=jax.ShapeDtypeStruct((m, w), F32),
        grid=(m // br,),
        in_specs=[pl.BlockSpec((br, w), lambda i: (i, 0)),
                  pl.BlockSpec((1, kv), lambda i: (0, 0)),
                  pl.BlockSpec((br, LANES), lambda i: (i, 0)),
                  pl.BlockSpec((br, LANES), lambda i: (i, 0))],
        out_specs=pl.BlockSpec((br, w), lambda i: (i, 0)),
        compiler_params=_params(("parallel",)),
        name="kv_post",
    )(lat, gain.reshape(1, kv), cos_t, sin_t)


def _attn_kernel(q_ref, cos_ref, sin_ref, k_ref, v_ref, o_ref, *, bq, bk, nope, vd, scale, causal,
                 n_keys, n_meta, lk_pad, n_split):
    qi = pl.program_id(2)
    q = q_ref[...].astype(F32)
    qr = q[:, nope:]
    qr = qr * cos_ref[...] + pltpu.roll(qr, LANES // 2, axis=1) * sin_ref[...]
    qf = (jnp.concatenate([q[:, :nope], qr], axis=1) * (scale * math.log2(math.e))).astype(BF16)

    def visible_end(r):
        return jnp.where(r < n_meta, n_meta, ((r - n_meta) // CHUNK + 1) * CHUNK + n_meta)

    row = qi * bq + lax.broadcasted_iota(jnp.int32, (bq, 1), 0)
    if causal:
        vis = visible_end(row)
        n_full = visible_end(qi * bq) // bk
        nkb = (jnp.minimum(visible_end(qi * bq + bq - 1), lk_pad) + bk - 1) // bk
    else:
        vis = jnp.full((bq, 1), n_keys, jnp.int32)
        n_full = n_keys // bk
        nkb = (n_keys + bk - 1) // bk

    rs = bq // n_split
    groups = [slice(g * rs, (g + 1) * rs) for g in range(n_split)]
    qg = [qf[sl] for sl in groups]
    visg = [vis[sl] for sl in groups]

    def step(j, carry, masked):
        ms, accs = carry
        start = pl.multiple_of(j * bk, bk)
        kb = k_ref[pl.ds(start, bk), :]
        vb = v_ref[pl.ds(start, bk), :]
        ss = [_dot_nt(q, kb) for q in qg]
        if masked:
            kpos = start + lax.broadcasted_iota(jnp.int32, (1, bk), 1)
            ss = [jnp.where(kpos < v, s, NEG_INF) for s, v in zip(ss, visg)]
        m_new = [jnp.maximum(m, jnp.max(s, axis=-1, keepdims=True)) for m, s in zip(ms, ss)]
        ps = [jnp.exp2(s - m).astype(BF16) for s, m in zip(ss, m_new)]
        pv = [_dot(p, vb) for p in ps]
        accs = [jnp.exp2(m - mn) * acc + o for m, mn, acc, o in zip(ms, m_new, accs, pv)]
        return tuple(m_new), tuple(accs)

    init = (tuple(jnp.full((rs, 1), NEG_INF, F32) for _ in groups),
            tuple(jnp.zeros((rs, v_ref.shape[1]), F32) for _ in groups))
    carry = lax.fori_loop(0, n_full, functools.partial(step, masked=False), init)
    _, accs = lax.fori_loop(n_full, nkb, functools.partial(step, masked=True), carry)
    for sl, acc in zip(groups, accs):
        o_ref[sl, :] = (acc[:, :vd] / acc[:, vd:vd + 1]).astype(o_ref.dtype)


def _attention(q, cos_t, sin_t, kf, vf, *, nb, lq, lk_pad, row0, heads, nope, vd, bq, bk, scale,
               causal, n_keys, n_meta):
    dqk = nope + LANES
    nq = lq // bq
    blk0 = row0 // bq
    qmap = lambda b, h, i: (blk0 + b * nq + i, h)
    tmap = lambda b, h, i: (blk0 + b * nq + i, 0)
    return pl.pallas_call(
        functools.partial(_attn_kernel, bq=bq, bk=bk, nope=nope, vd=vd, scale=scale, causal=causal,
                          n_keys=n_keys, n_meta=n_meta, lk_pad=lk_pad,
                          n_split=2 if bq % 32 == 0 else 1),
        out_shape=jax.ShapeDtypeStruct((nb * lq, heads * vd), BF16),
        grid=(nb, heads, nq),
        in_specs=[pl.BlockSpec((bq, dqk), qmap),
                  pl.BlockSpec((bq, LANES), tmap),
                  pl.BlockSpec((bq, LANES), tmap),
                  pl.BlockSpec((lk_pad, dqk), lambda b, h, i: (b, h)),
                  pl.BlockSpec((lk_pad, vd + LANES), lambda b, h, i: (b, h))],
        out_specs=pl.BlockSpec((bq, vd), lambda b, h, i: (b * nq + i, h)),
        compiler_params=_params(("parallel", "parallel", "arbitrary")),
        name="mla_attention_causal" if causal else "mla_attention_full",
    )(q, cos_t, sin_t, kf, vf)


def _item_info(i, n_p, cps, cs, l_valid):
    is_p = i < n_p
    j = i - n_p
    seq = jnp.where(is_p, i // cps, n_p // cps + j // cs)
    c = jnp.where(is_p, i % cps, j % cs)
    nc = jnp.where(is_p, cps, cs)
    nvalid = jnp.where(is_p, jnp.clip(l_valid - c * CHUNK, 0, CHUNK), CHUNK)
    return seq, c, nc, nvalid


def _tri_inverse_many(mats):
    n = mats[0].shape[0]
    r = lax.broadcasted_iota(jnp.int32, (n, n), 0)
    c = lax.broadcasted_iota(jnp.int32, (n, n), 1)
    eye = jnp.where(r == c, 1.0, 0.0)
    zero = jnp.zeros((n, n), F32)

    def split(x):
        hi = x.astype(BF16).astype(F32)
        return hi, x - hi

    def lhs_of(x):
        hi, lo = split(x)
        return jnp.concatenate([hi, lo, hi, zero], axis=1).astype(BF16)

    def rhs_of(x):
        hi, lo = split(x)
        return jnp.concatenate([hi, hi, lo, zero], axis=0).astype(BF16)

    ps = [eye - a for a in mats]
    aks = list(mats)
    rhs = [rhs_of(a) for a in mats]
    power = 2
    while power < n:
        aks = [_dot(lhs_of(ak), rk) for ak, rk in zip(aks, rhs)]
        rhs = [rhs_of(ak) for ak in aks]
        ps = [p + _dot(lhs_of(p), rk) for p, rk in zip(ps, rhs)]
        power *= 2
    return ps


def _gdn_pre_kernel(qc_ref, kc_ref, vc_ref, qp_ref, kp_ref, vp_ref, q0_ref, k0_ref, v0_ref,
                    wq_ref, wk_ref, wv_ref, ab_ref, alog_ref, dtb_ref,
                    val_ref, kcd_ref, qdec_ref, kdec_ref, qk_ref, eg_ref,
                    bufq, bufk, bufv, *, hg, dk, n_p, cps, cs, l_valid, width):
    C = CHUNK
    i = pl.program_id(1)
    _, c, _, nvalid = _item_info(i, n_p, cps, cs, l_valid)
    first = c == 0

    def conv_silu(cur_ref, prev_ref, init_ref, w_ref, buf):
        buf[0:SUBLANES, :] = jnp.where(first, init_ref[0], prev_ref[...])
        buf[SUBLANES:SUBLANES + C, :] = cur_ref[...]
        base = SUBLANES - (width - 1)
        y = buf[base:base + C, :] * w_ref[0:1, :]
        for t in range(1, width):
            y = y + buf[base + t:base + t + C, :] * w_ref[t:t + 1, :]
        return y * _sigmoid(y)

    qs = conv_silu(qc_ref, qp_ref, q0_ref, wq_ref, bufq)
    ks = conv_silu(kc_ref, kp_ref, k0_ref, wk_ref, bufk)
    vs = conv_silu(vc_ref, vp_ref, v0_ref, wv_ref, bufv)

    ab = ab_ref[...]
    rows = lax.broadcasted_iota(jnp.int32, (C, LANES), 0)
    valid = rows < nvalid
    xs = ab + dtb_ref[0]
    softplus = jnp.maximum(xs, 0.0) + jnp.log(1.0 + jnp.exp(-jnp.abs(xs)))
    g = jnp.where(valid, -jnp.exp(alog_ref[0]) * softplus, 0.0)
    beta = jnp.where(valid, _sigmoid(ab), 0.0)
    r = lax.broadcasted_iota(jnp.int32, (C, C), 0)
    cc = lax.broadcasted_iota(jnp.int32, (C, C), 1)
    incl = r >= cc
    strict = r > cc
    gcum = jnp.dot(jnp.where(incl, 1.0, 0.0), g, precision=_HI, preferred_element_type=F32)
    gcum_t = gcum.T

    heads = range(hg)
    sls = [slice(h * dk, (h + 1) * dk) for h in heads]
    qn = [qs[:, sl] * lax.rsqrt(jnp.sum(qs[:, sl] * qs[:, sl], axis=-1, keepdims=True) + NORM_EPS)
          * (dk ** -0.5) for sl in sls]
    kn = [ks[:, sl] * lax.rsqrt(jnp.sum(ks[:, sl] * ks[:, sl], axis=-1, keepdims=True) + NORM_EPS)
          for sl in sls]
    bcol = [beta[:, hg + h:hg + h + 1] for h in heads]
    gcol = [gcum[:, h:h + 1] for h in heads]
    glast = [gcum[C - 1:C, h:h + 1] for h in heads]
    eg = [jnp.exp(gcol[h]) for h in heads]
    decay = [jnp.where(incl, jnp.exp(jnp.where(incl, gcol[h] - gcum_t[h:h + 1, :], 0.0)), 0.0) for h in heads]
    kb = [kn[h] * bcol[h] for h in heads]
    knb = [kn[h].astype(BF16) for h in heads]
    a_mat = [jnp.where(strict, _dot_nt(kb[h].astype(BF16), knb[h]) * decay[h], 0.0) for h in heads]
    t_mat = _tri_inverse_many(a_mat)
    rhs = [jnp.concatenate([vs[:, sls[h]] * bcol[h], kb[h] * eg[h]], axis=1).astype(BF16) for h in heads]
    tv = [_dot(t_mat[h].astype(BF16), rhs[h]) for h in heads]
    qk = [_dot_nt(qn[h].astype(BF16), knb[h]) * decay[h] for h in heads]
    for h in heads:
        val_ref[:, sls[h]] = tv[h][:, :dk]
        kcd_ref[:, sls[h]] = tv[h][:, dk:].astype(BF16)
        qdec_ref[:, sls[h]] = (qn[h] * eg[h]).astype(BF16)
        kdec_ref[:, sls[h]] = (kn[h] * jnp.exp(glast[h] - gcol[h])).astype(BF16)
        eg_ref[0, 0, h:h + 1, :] = jnp.broadcast_to(jnp.exp(glast[h]), (1, LANES))
    qk_ref[...] = jnp.concatenate(qk, axis=1)


def _gdn_scan_kernel(val_ref, kcd_ref, qdec_ref, kdec_ref, qk_ref, eg_ref, z_ref, gain_ref, s0_ref,
                     o_ref, sout_ref, s_ref, *, hg, dk, n_p, cps, cs, l_valid):
    C = CHUNK
    i = pl.program_id(1)
    _, c, nc, _ = _item_info(i, n_p, cps, cs, l_valid)

    @pl.when(c == 0)
    def _():
        s_ref[...] = s0_ref[0]

    qk_all = qk_ref[...]
    gain = gain_ref[...]
    for h in range(hg):
        sl = slice(h * dk, (h + 1) * dk)
        s_h = s_ref[h]
        sb = s_h.astype(BF16)
        u = val_ref[:, sl] - _dot(kcd_ref[:, sl], sb)
        ub = u.astype(BF16)
        o = _dot(qdec_ref[:, sl], sb) + _dot(qk_all[:, h * C:(h + 1) * C].astype(BF16), ub)
        s_ref[h] = s_h * eg_ref[0, 0, h:h + 1, :] + _dot_tn(kdec_ref[:, sl], ub)
        z = z_ref[:, sl]
        ms = jnp.mean(o * o, axis=-1, keepdims=True)
        o_ref[:, sl] = (o * lax.rsqrt(ms + NORM_EPS) * gain * (z * _sigmoid(z))).astype(o_ref.dtype)

    @pl.when(c == nc - 1)
    def _():
        sout_ref[0] = s_ref[...]


def _gdn_mixer(proj, gates, conv0, s0, conv_w, a_log, dt_bias, o_gain, *, heads, dk, hg, n_p, cps, cs, l_valid):
    m = proj.shape[0]
    C = CHUNK
    w = heads * dk
    n_hg = heads // hg
    gw = hg * dk
    n_items = m // C
    n_seq = s0.shape[0]
    width = conv_w.shape[0]
    info = dict(n_p=n_p, cps=cps, cs=cs, l_valid=l_valid)

    def seq_of(i):
        return _item_info(i, n_p, cps, cs, l_valid)[0]

    cur = lambda off: pl.BlockSpec((C, gw), lambda g, i: (i, off * n_hg + g))
    prev = lambda off: pl.BlockSpec(
        (SUBLANES, gw), lambda g, i: (jnp.maximum(i * (C // SUBLANES) - 1, 0), off * n_hg + g))
    init = lambda off: pl.BlockSpec((1, SUBLANES, gw), lambda g, i: (seq_of(i), 0, off * n_hg + g))
    wspec = lambda off: pl.BlockSpec((width, gw), lambda g, i: (0, off * n_hg + g))
    gate_vec = pl.BlockSpec((1, 1, LANES), lambda g, i: (g, 0, 0))
    tok = pl.BlockSpec((C, gw), lambda g, i: (i, g))

    alog_g = jnp.pad(a_log.reshape(n_hg, 1, hg).astype(F32), ((0, 0), (0, 0), (0, LANES - hg)))
    dtb_g = jnp.pad(dt_bias.reshape(n_hg, 1, hg).astype(F32), ((0, 0), (0, 0), (0, LANES - hg)))

    val, kcd, qdec, kdec, qk, eg = pl.pallas_call(
        functools.partial(_gdn_pre_kernel, hg=hg, dk=dk, width=width, **info),
        out_shape=(jax.ShapeDtypeStruct((m, w), F32),
                   jax.ShapeDtypeStruct((m, w), BF16),
                   jax.ShapeDtypeStruct((m, w), BF16),
                   jax.ShapeDtypeStruct((m, w), BF16),
                   jax.ShapeDtypeStruct((m, heads * C), F32),
                   jax.ShapeDtypeStruct((n_items, n_hg, hg, LANES), F32)),
        grid=(n_hg, n_items),
        in_specs=[cur(0), cur(1), cur(2), prev(0), prev(1), prev(2), init(0), init(1), init(2),
                  wspec(0), wspec(1), wspec(2),
                  pl.BlockSpec((C, LANES), lambda g, i: (i, g)),
                  gate_vec, gate_vec],
        out_specs=(tok, tok, tok, tok,
                   pl.BlockSpec((C, hg * C), lambda g, i: (i, g)),
                   pl.BlockSpec((1, 1, hg, LANES), lambda g, i: (i, g, 0, 0))),
        scratch_shapes=[pltpu.VMEM((SUBLANES + C, gw), F32)] * 3,
        compiler_params=_params(("parallel", "parallel")),
        name="gdn_prepare",
    )(proj, proj, proj, proj, proj, proj, conv0, conv0, conv0, conv_w, conv_w, conv_w,
      gates, alog_g, dtb_g)

    o, s_new = pl.pallas_call(
        functools.partial(_gdn_scan_kernel, hg=hg, dk=dk, **info),
        out_shape=(jax.ShapeDtypeStruct((m, w), BF16),
                   jax.ShapeDtypeStruct((n_seq, heads, dk, dk), F32)),
        grid=(n_hg, n_items),
        in_specs=[tok, tok, tok, tok,
                  pl.BlockSpec((C, hg * C), lambda g, i: (i, g)),
                  pl.BlockSpec((1, 1, hg, LANES), lambda g, i: (i, g, 0, 0)),
                  pl.BlockSpec((C, gw), lambda g, i: (i, 3 * n_hg + g)),
                  pl.BlockSpec((1, dk), lambda g, i: (0, 0)),
                  pl.BlockSpec((1, hg, dk, dk), lambda g, i: (seq_of(i), g, 0, 0))],
        out_specs=(tok, pl.BlockSpec((1, hg, dk, dk), lambda g, i: (seq_of(i), g, 0, 0))),
        scratch_shapes=[pltpu.VMEM((hg, dk, dk), F32)],
        compiler_params=_params(("parallel", "arbitrary")),
        name="gdn_scan",
    )(val, kcd, qdec, kdec, qk, eg, proj, o_gain.reshape(1, dk).astype(F32), s0)
    return o, s_new


def _split_half_layout(x, half):
    pad = [(0, 0)] * (x.ndim - 1) + [(0, LANES // 2 - half)]
    return jnp.concatenate([jnp.pad(x[..., :half], pad), jnp.pad(x[..., half:], pad)], axis=-1)


def _rope_tables(pos, half):
    inv_freq = 1.0 / (ROPE_THETA ** (jnp.arange(half, dtype=F32) / half))
    ang = pos[:, None] * inv_freq[None, :]
    cos, sin = jnp.cos(ang), jnp.sin(ang)
    cos_t = _split_half_layout(jnp.concatenate([cos, cos], -1), half)
    sin_t = _split_half_layout(jnp.concatenate([-sin, sin], -1), half)
    return cos_t, sin_t


def kernel(x_prompt, x_sample, state_conv, state_delta, cache_ckv, cache_krope, meta_tokens,
           a_w_in, a_conv_w, a_a_log, a_dt_bias, a_o_gain, a_w_o,
           b_w_dq, b_q_gain, b_w_uq, b_w_o,
           kv_w_dkv, kv_gain, kv_w_uk, kv_w_uv,
           mlp_w_up, mlp_w_down, ln_gain, ln_bias):
    B, seq, D = x_prompt.shape
    DB, ts, _ = x_sample.shape
    n_meta = meta_tokens.shape[0]
    depth = ln_gain.shape[0]
    n_a = a_w_in.shape[0]
    H = a_a_log.shape[1]
    dk = a_o_gain.shape[1]
    W = H * dk
    width = a_conv_w.shape[1]
    past = cache_ckv.shape[1]
    KV, MH, nope = kv_w_uk.shape
    vd = kv_w_uv.shape[2]
    rope = cache_krope.shape[2]
    half = rope // 2
    alpha = (2 * depth) ** 0.25
    scale = (nope + rope) ** -0.5
    assert ts % CHUNK == 0 and ts >= width - 1 and half < LANES // 2
    assert nope % LANES == 0 and vd % LANES == 0 and KV % LANES == 0 and dk % LANES == 0

    L = n_meta + seq
    LP = _round_up(L, LANES)
    n_prompt_rows = B * LP
    M = n_prompt_rows + DB * ts
    cps, cs = LP // CHUNK, ts // CHUNK
    n_p = B * cps
    hg = min(8, H)
    n_hg = H // hg

    meta = jnp.broadcast_to(meta_tokens.astype(F32)[None], (B, n_meta, D))
    xp = jnp.concatenate([meta, x_prompt, jnp.zeros((B, LP - L, D), F32)], axis=1)
    x = jnp.concatenate([xp.reshape(n_prompt_rows, D), x_sample.reshape(DB * ts, D)], axis=0)
    xb = x.astype(BF16)

    pos = jnp.concatenate([jnp.tile(jnp.arange(LP, dtype=F32), B),
                           jnp.tile(past + jnp.arange(ts, dtype=F32), DB)])
    cos_t, sin_t = _rope_tables(pos, half)

    conv0 = jnp.concatenate([jnp.zeros((n_a, B, width - 1, 3 * W), F32), state_conv.astype(F32)], axis=1)
    conv0 = jnp.pad(conv0, ((0, 0), (0, 0), (SUBLANES - (width - 1), 0), (0, 0)))
    s0 = jnp.concatenate([jnp.zeros((n_a, B, H, dk, dk), F32), state_delta.astype(F32)], axis=1)

    tail = jnp.arange(width - 1)
    conv_rows_p = (jnp.arange(B)[:, None] * LP + L - (width - 1) + tail[None, :]).reshape(-1)
    conv_rows_s = (n_prompt_rows + jnp.arange(DB)[:, None] * ts + ts - (width - 1) + tail[None, :]).reshape(-1)

    new_conv_p, new_conv_s, new_s = [], [], []
    k_slabs = None
    lat = None
    for layer in range(depth):
        if layer < n_a:
            w_ab = a_w_in[layer, :, 4 * W:]
            gate_w = jnp.concatenate(
                [w_ab[:, :H].reshape(D, n_hg, hg), w_ab[:, H:].reshape(D, n_hg, hg),
                 jnp.zeros((D, n_hg, LANES - 2 * hg), w_ab.dtype)], axis=-1).reshape(D, n_hg * LANES)
            proj = _matmul(xb, a_w_in, F32, layer=layer, n_use=4 * W)
            gates = _matmul(xb, gate_w.astype(BF16), F32)
            o, s_fin = _gdn_mixer(proj, gates, conv0[layer], s0[layer], a_conv_w[layer].astype(F32),
                                  a_a_log[layer], a_dt_bias[layer], a_o_gain[layer],
                                  heads=H, dk=dk, hg=hg, n_p=n_p, cps=cps, cs=cs, l_valid=L)
            h = _matmul(o, a_w_o, F32, layer=layer)
            new_conv_p.append(proj[conv_rows_p, :3 * W].reshape(B, width - 1, 3 * W))
            new_conv_s.append(proj[conv_rows_s, :3 * W].reshape(DB, width - 1, 3 * W))
            new_s.append(s_fin)
        else:
            j = layer - n_a
            ql = _matmul(xb, b_w_dq[j].astype(BF16), BF16, epilogue="rms", gain=b_q_gain[j])
            w_uq = b_w_uq[j].reshape(-1, MH, nope + rope)
            w_uq = jnp.concatenate([w_uq[..., :nope], _split_half_layout(w_uq[..., nope:], half)], axis=-1)
            q = _matmul(ql, w_uq.reshape(-1, MH * (nope + LANES)).astype(BF16), BF16)
            kp, vp, ks, vs, lkp, lks = k_slabs
            o_p = _attention(q, cos_t, sin_t, kp, vp, nb=B, lq=LP, lk_pad=lkp, row0=0, heads=MH,
                             nope=nope, vd=vd, bq=_pick_block(LP, 384, LANES), bk=512, scale=scale,
                             causal=True, n_keys=L, n_meta=n_meta)
            o_s = _attention(q, cos_t, sin_t, ks, vs, nb=DB, lq=ts, lk_pad=lks, row0=n_prompt_rows,
                             heads=MH, nope=nope, vd=vd, bq=ts, bk=lks, scale=scale,
                             causal=False, n_keys=past + ts, n_meta=n_meta)
            h = _matmul(jnp.concatenate([o_p, o_s], axis=0), b_w_o, F32, layer=j)

        x, xb = _deepnorm(x, h, ln_gain[layer, 0], ln_bias[layer, 0], alpha)
        hid = _matmul(xb, mlp_w_up, BF16, layer=layer, epilogue="relu2")
        h = _matmul(hid, mlp_w_down, F32, layer=layer, bn=1024, bk=2048)
        x, xb = _deepnorm(x, h, ln_gain[layer, 1], ln_bias[layer, 1], alpha)

        if layer == n_a - 1:
            w_dkv = jnp.concatenate([kv_w_dkv[:, :KV], _split_half_layout(kv_w_dkv[:, KV:], half)], axis=1)
            lat = _kv_post(_matmul(xb, w_dkv.astype(BF16), F32), kv_gain.astype(F32), cos_t, sin_t, KV)
            lane_ids = jnp.arange(LANES)
            eye = jnp.where((lane_ids[:, None] == lane_ids[None, :]) & (lane_ids[:, None] != half), 1.0, 0.0)
            ones_col = jnp.where((lane_ids[:, None] == half) & (lane_ids[None, :] == 0), 1.0, 0.0)
            w_k = jnp.concatenate(
                [jnp.concatenate([kv_w_uk.astype(F32), jnp.zeros((KV, MH, LANES), F32)], axis=-1),
                 jnp.concatenate([jnp.zeros((LANES, MH, nope), F32),
                                  jnp.broadcast_to(eye[:, None, :], (LANES, MH, LANES))], axis=-1)],
                axis=0).reshape(KV + LANES, MH * (nope + LANES)).astype(BF16)
            w_v = jnp.concatenate(
                [jnp.concatenate([kv_w_uv.astype(F32), jnp.zeros((KV, MH, LANES), F32)], axis=-1),
                 jnp.concatenate([jnp.zeros((LANES, MH, vd), F32),
                                  jnp.broadcast_to(ones_col[:, None, :], (LANES, MH, LANES))], axis=-1)],
                axis=0).reshape(KV + LANES, MH * (vd + LANES)).astype(BF16)
            one_lane = jnp.where(jnp.arange(KV + LANES) == KV + half, 1.0, 0.0)
            lkp = _round_up(L, 512)
            lat_p = lat[:n_prompt_rows].reshape(B, LP, KV + LANES)
            if lkp >= LP:
                lat_p = jnp.pad(lat_p, ((0, 0), (0, lkp - LP), (0, 0)))
            else:
                lat_p = lat_p[:, :lkp]
            lat_p = (lat_p.reshape(B * lkp, KV + LANES) + one_lane).astype(BF16)
            lks = _round_up(past + ts, LANES)
            cache = jnp.concatenate([cache_ckv.astype(F32), _split_half_layout(cache_krope.astype(F32), half)],
                                    axis=-1)
            lat_s = jnp.concatenate([cache, lat[n_prompt_rows:].reshape(DB, ts, KV + LANES),
                                     jnp.zeros((DB, lks - past - ts, KV + LANES), F32)], axis=1)
            lat_s = (lat_s.reshape(DB * lks, KV + LANES) + one_lane).astype(BF16)
            k_slabs = (_matmul(lat_p, w_k, BF16, bm=1024), _matmul(lat_p, w_v, BF16, bm=1024),
                       _matmul(lat_s, w_k, BF16, bm=1024), _matmul(lat_s, w_v, BF16, bm=1024), lkp, lks)

    def unsplit(r):
        return jnp.concatenate([r[..., :half], r[..., LANES // 2:LANES // 2 + half]], axis=-1)

    lat_p = lat[:n_prompt_rows].reshape(B, LP, KV + LANES)[:, :L]
    lat_s = lat[n_prompt_rows:].reshape(DB, ts, KV + LANES)
    y_prompt = jnp.stack([x[b * LP + n_meta:b * LP + L] for b in range(B)])
    y_sample = x[n_prompt_rows:].reshape(DB, ts, D)
    s_all = jnp.stack(new_s)
    return (y_prompt, y_sample,
            jnp.stack(new_conv_p), s_all[:, :B],
            lat_p[..., :KV], unsplit(lat_p[..., KV:]),
            jnp.stack(new_conv_s), s_all[:, B:],
            lat_s[..., :KV], unsplit(lat_s[..., KV:]))
```

```python
import functools
import math

import jax
import jax.numpy as jnp
from jax import lax
from jax.experimental import pallas as pl
from jax.experimental.pallas import tpu as pltpu

CHUNK = 64
ROPE_THETA = 10000.0
NORM_EPS = 1e-6
NEG_INF = -1e30
LANES = 128
SUBLANES = 8
VMEM_LIMIT_BYTES = 56 * 1024 * 1024
_CHUNK_SHIFT = CHUNK.bit_length() - 1
assert 1 << _CHUNK_SHIFT == CHUNK

F32 = jnp.float32
BF16 = jnp.bfloat16
_HI = lax.Precision.HIGHEST


def _round_up(x, m):
    return -(-x // m) * m


def _pick_block(dim, target, align):
    best = None
    for d in range(align, min(dim, target) + 1, align):
        if dim % d == 0:
            best = d
    return best if best is not None else dim


def _dot(a, b):
    return jnp.dot(a, b, preferred_element_type=F32)


def _dot_nt(a, b):
    return lax.dot_general(a, b, (((1,), (1,)), ((), ())), preferred_element_type=F32)


def _dot_tn(a, b):
    return lax.dot_general(a, b, (((0,), (0,)), ((), ())), preferred_element_type=F32)


def _sigmoid(x):
    return 1.0 / (1.0 + jnp.exp(-x))


def _params(sem):
    return pltpu.CompilerParams(dimension_semantics=sem, vmem_limit_bytes=VMEM_LIMIT_BYTES)


def _mm_kernel(*refs, nk, epilogue):
    if epilogue == "rms":
        a_ref, b_ref, g_ref, o_ref = refs[:4]
        rest = refs[4:]
    else:
        a_ref, b_ref, o_ref = refs[:3]
        g_ref = None
        rest = refs[3:]

    def finish(acc):
        if epilogue == "relu2":
            r = jnp.maximum(acc, 0.0)
            acc = r * r
        elif epilogue == "rms":
            ms = jnp.mean(acc * acc, axis=-1, keepdims=True)
            acc = acc * lax.rsqrt(ms + NORM_EPS) * g_ref[...]
        o_ref[...] = acc.astype(o_ref.dtype)

    def product():
        return _dot(a_ref[...], b_ref[...].astype(BF16))

    if nk == 1:
        finish(product())
    else:
        acc_ref = rest[0]
        k = pl.program_id(2)

        @pl.when(k == 0)
        def _():
            acc_ref[...] = jnp.zeros_like(acc_ref)

        acc_ref[...] += product()

        @pl.when(k == nk - 1)
        def _():
            finish(acc_ref[...])


def _matmul(a, b, out_dtype, *, layer=None, n_use=None, epilogue="none", gain=None, bm=1312, bn=512, bk=4096):
    m, kdim = a.shape
    n = n_use if n_use is not None else b.shape[-1]
    bm = _pick_block(m, bm, 16)
    bn = n if epilogue == "rms" else _pick_block(n, bn, LANES)
    bk = _pick_block(kdim, bk, LANES)
    nk = kdim // bk
    a_mode = dict(pipeline_mode=pl.Buffered(1)) if nk == 1 else {}
    if b.ndim == 3:
        b_spec = pl.BlockSpec((None, bk, bn), lambda i, j, k: (layer, k, j))
    else:
        b_spec = pl.BlockSpec((bk, bn), lambda i, j, k: (k, j))
    in_specs = [pl.BlockSpec((bm, bk), lambda i, j, k: (i, k), **a_mode), b_spec]
    args = [a, b]
    if epilogue == "rms":
        in_specs.append(pl.BlockSpec((1, bn), lambda i, j, k: (0, j)))
        args.append(gain.reshape(1, n).astype(F32))
    scratch = [pltpu.VMEM((bm, bn), F32)] if nk > 1 else []
    return pl.pallas_call(
        functools.partial(_mm_kernel, nk=nk, epilogue=epilogue),
        out_shape=jax.ShapeDtypeStruct((m, n), out_dtype),
        grid=(m // bm, n // bn, nk),
        in_specs=in_specs,
        out_specs=pl.BlockSpec((bm, bn), lambda i, j, k: (i, j)),
        scratch_shapes=scratch,
        compiler_params=_params(("parallel", "parallel", "arbitrary")),
        name="matmul_" + epilogue,
    )(*args)


def _ln_kernel(x_ref, h_ref, g_ref, b_ref, of_ref, ob_ref, *, alpha):
    y = alpha * x_ref[...] + h_ref[...]
    mu = jnp.mean(y, axis=-1, keepdims=True)
    d = y - mu
    var = jnp.mean(d * d, axis=-1, keepdims=True)
    out = d * lax.rsqrt(var + NORM_EPS) * g_ref[...] + b_ref[...]
    of_ref[...] = out
    ob_ref[...] = out.astype(BF16)


def _deepnorm(x, h, gain, bias, alpha):
    m, d = x.shape
    br = _pick_block(m, 256, 16)
    row = pl.BlockSpec((br, d), lambda i: (i, 0))
    vec = pl.BlockSpec((1, d), lambda i: (0, 0))
    return pl.pallas_call(
        functools.partial(_ln_kernel, alpha=alpha),
        out_shape=(jax.ShapeDtypeStruct((m, d), F32), jax.ShapeDtypeStruct((m, d), BF16)),
        grid=(m // br,),
        in_specs=[row, row, vec, vec],
        out_specs=(row, row),
        compiler_params=_params(("parallel",)),
        name="deepnorm",
    )(x, h, gain.reshape(1, d), bias.reshape(1, d))


def _kv_post_kernel(lat_ref, g_ref, cos_ref, sin_ref, o_ref, *, kv):
    c = lat_ref[:, :kv]
    ms = jnp.mean(c * c, axis=-1, keepdims=True)
    o_ref[:, :kv] = c * lax.rsqrt(ms + NORM_EPS) * g_ref[...]
    r = lat_ref[:, kv:]
    o_ref[:, kv:] = r * cos_ref[...] + pltpu.roll(r, LANES // 2, axis=1) * sin_ref[...]


def _kv_post(lat, gain, cos_t, sin_t, kv):
    m, w = lat.shape
    br = _pick_block(m, 256, 8)
    return pl.pallas_call(
        functools.partial(_kv_post_kernel, kv=kv),
        out_shape=jax.ShapeDtypeStruct((m, w), F32),
        grid=(m // br,),
        in_specs=[pl.BlockSpec((br, w), lambda i: (i, 0)),
                  pl.BlockSpec((1, kv), lambda i: (0, 0)),
                  pl.BlockSpec((br, LANES), lambda i: (i, 0)),
                  pl.BlockSpec((br, LANES), lambda i: (i, 0))],
        out_specs=pl.BlockSpec((br, w), lambda i: (i, 0)),
        compiler_params=_params(("parallel",)),
        name="kv_post",
    )(lat, gain.reshape(1, kv), cos_t, sin_t)


def _rope_query(q, cos, sin, nope, scale):
    q = q.astype(F32)
    qr = q[:, nope:]
    qr = qr * cos + pltpu.roll(qr, LANES // 2, axis=1) * sin
    return (jnp.concatenate([q[:, :nope], qr], axis=1) * (scale * math.log2(math.e))).astype(BF16)


def _attn_causal_kernel(q_ref, cos_ref, sin_ref, kt_ref, v_ref, o_ref,
                        q_scr, s_scr, p_scr, m_scr, a_scr, acc_scr, *, bq, bk, rt, nope, vd, scale,
                        n_meta, lk_pad):
    qi = pl.program_id(2)
    tiles = [slice(t * rt, (t + 1) * rt) for t in range(bq // rt)]
    lane_chunks = [slice(c * LANES, (c + 1) * LANES) for c in range(bk // LANES)]

    def visible_end(r):
        chunk_id = lax.shift_right_arithmetic(r - n_meta, _CHUNK_SHIFT) + 1
        return jnp.where(r < n_meta, n_meta, chunk_id * CHUNK + n_meta)

    n_full = visible_end(qi * bq) // bk
    nkb = (jnp.minimum(visible_end(qi * bq + bq - 1), lk_pad) + bk - 1) // bk

    for rows in tiles:
        q_scr[rows, :] = _rope_query(q_ref[rows, :], cos_ref[rows, :], sin_ref[rows, :], nope, scale)
    m_scr[...] = jnp.full(m_scr.shape, NEG_INF, F32)
    acc_scr[...] = jnp.zeros(acc_scr.shape, F32)

    def step(j, carry, masked):
        start = pl.multiple_of(j * bk, bk)
        s_scr[...] = _dot(q_scr[...], kt_ref[:, pl.ds(start, bk)])
        for t, rows in enumerate(tiles):
            s = s_scr[rows, :]
            if masked:
                kpos = start + lax.broadcasted_iota(jnp.int32, (1, bk), 1)
                vis = visible_end(qi * bq + t * rt + lax.broadcasted_iota(jnp.int32, (rt, 1), 0))
                s = jnp.where(kpos < vis, s, NEG_INF)
            m_old = m_scr[rows, :]
            m_new = jnp.maximum(m_old, jnp.max(s, axis=-1, keepdims=True))
            for lc in lane_chunks:
                p_scr[rows, lc] = jnp.exp2(s[:, lc] - m_new).astype(BF16)
            a_scr[rows, :] = jnp.exp2(m_old - m_new)
            m_scr[rows, :] = m_new
        pv = _dot(p_scr[...], v_ref[pl.ds(start, bk), :])
        a = a_scr[...]
        acc_scr[...] = jnp.concatenate([a] * (acc_scr.shape[1] // LANES), axis=1) * acc_scr[...] + pv
        return carry

    lax.fori_loop(0, n_full, functools.partial(step, masked=False), 0)
    lax.fori_loop(n_full, nkb, functools.partial(step, masked=True), 0)
    for rows in tiles:
        o_ref[rows, :] = (acc_scr[rows, :vd] / acc_scr[rows, vd:vd + 1]).astype(o_ref.dtype)


def _attn_full_kernel(q_ref, cos_ref, sin_ref, kt_ref, v_ref, o_ref, *, hb, nope, vd, scale, n_keys):
    dqk, dv = nope + LANES, vd + LANES
    heads = range(hb)
    cos, sin = cos_ref[...], sin_ref[...]
    kpos = lax.broadcasted_iota(jnp.int32, (1, kt_ref.shape[1]), 1)
    qs = [_rope_query(q_ref[:, h * dqk:(h + 1) * dqk], cos, sin, nope, scale) for h in heads]
    ss = [jnp.where(kpos < n_keys, _dot(qs[h], kt_ref[h * dqk:(h + 1) * dqk, :]), NEG_INF) for h in heads]
    ps = [jnp.exp2(s - jnp.max(s, axis=-1, keepdims=True)).astype(BF16) for s in ss]
    pv = [_dot(ps[h], v_ref[:, h * dv:(h + 1) * dv]) for h in heads]
    for h in heads:
        o_ref[:, h * vd:(h + 1) * vd] = (pv[h][:, :vd] / pv[h][:, vd:vd + 1]).astype(o_ref.dtype)


def _attention_causal(q, cos_t, sin_t, kt, vf, *, nb, lq, lk_pad, heads, nope, vd, bq, bk, scale, n_meta):
    dqk = nope + LANES
    nq = lq // bq
    rt = 32 if bq % 32 == 0 else bq
    return pl.pallas_call(
        functools.partial(_attn_causal_kernel, bq=bq, bk=bk, rt=rt, nope=nope, vd=vd, scale=scale,
                          n_meta=n_meta, lk_pad=lk_pad),
        out_shape=jax.ShapeDtypeStruct((nb * lq, heads * vd), BF16),
        grid=(nb, heads, nq),
        in_specs=[pl.BlockSpec((bq, dqk), lambda b, h, i: (b * nq + i, h)),
                  pl.BlockSpec((bq, LANES), lambda b, h, i: (b * nq + i, 0)),
                  pl.BlockSpec((bq, LANES), lambda b, h, i: (b * nq + i, 0)),
                  pl.BlockSpec((dqk, lk_pad), lambda b, h, i: (h, b)),
                  pl.BlockSpec((lk_pad, vd + LANES), lambda b, h, i: (b, h))],
        out_specs=pl.BlockSpec((bq, vd), lambda b, h, i: (b * nq + i, h)),
        scratch_shapes=[pltpu.VMEM((bq, dqk), BF16), pltpu.VMEM((bq, bk), F32), pltpu.VMEM((bq, bk), BF16),
                        pltpu.VMEM((bq, LANES), F32), pltpu.VMEM((bq, LANES), F32),
                        pltpu.VMEM((bq, vd + LANES), F32)],
        compiler_params=_params(("parallel", "parallel", "arbitrary")),
        name="mla_attention_causal",
    )(q, cos_t, sin_t, kt, vf)


def _attention_full(q, cos_t, sin_t, kt, vf, *, nb, lq, lk_pad, row0, heads, hb, nope, vd, scale, n_keys):
    dqk = nope + LANES
    blk0 = row0 // lq
    return pl.pallas_call(
        functools.partial(_attn_full_kernel, hb=hb, nope=nope, vd=vd, scale=scale, n_keys=n_keys),
        out_shape=jax.ShapeDtypeStruct((nb * lq, heads * vd), BF16),
        grid=(nb, heads // hb),
        in_specs=[pl.BlockSpec((lq, hb * dqk), lambda b, g: (blk0 + b, g)),
                  pl.BlockSpec((lq, LANES), lambda b, g: (blk0 + b, 0)),
                  pl.BlockSpec((lq, LANES), lambda b, g: (blk0 + b, 0)),
                  pl.BlockSpec((hb * dqk, lk_pad), lambda b, g: (g, b)),
                  pl.BlockSpec((lk_pad, hb * (vd + LANES)), lambda b, g: (b, g))],
        out_specs=pl.BlockSpec((lq, hb * vd), lambda b, g: (b, g)),
        compiler_params=_params(("parallel", "parallel")),
        name="mla_attention_full",
    )(q, cos_t, sin_t, kt, vf)


def _item_info(i, n_p, cps, cs, l_valid):
    is_p = i < n_p
    j = i - n_p
    seq = jnp.where(is_p, i // cps, n_p // cps + j // cs)
    c = jnp.where(is_p, i % cps, j % cs)
    nc = jnp.where(is_p, cps, cs)
    nvalid = jnp.where(is_p, jnp.clip(l_valid - c * CHUNK, 0, CHUNK), CHUNK)
    return seq, c, nc, nvalid


def _tri_inverse_many(mats):
    n = mats[0].shape[0]
    r = lax.broadcasted_iota(jnp.int32, (n, n), 0)
    c = lax.broadcasted_iota(jnp.int32, (n, n), 1)
    eye = jnp.where(r == c, 1.0, 0.0)
    zero = jnp.zeros((n, n), F32)

    def split(x):
        hi = x.astype(BF16).astype(F32)
        return hi, x - hi

    def lhs_of(x):
        hi, lo = split(x)
        return jnp.concatenate([hi, lo, hi, zero], axis=1).astype(BF16)

    def rhs_of(x):
        hi, lo = split(x)
        return jnp.concatenate([hi, hi, lo, zero], axis=0).astype(BF16)

    ps = [eye - a for a in mats]
    aks = list(mats)
    rhs = [rhs_of(a) for a in mats]
    power = 2
    while power < n:
        aks = [_dot(lhs_of(ak), rk) for ak, rk in zip(aks, rhs)]
        rhs = [rhs_of(ak) for ak in aks]
        ps = [p + _dot(lhs_of(p), rk) for p, rk in zip(ps, rhs)]
        power *= 2
    return ps


def _gdn_pre_kernel(qc_ref, kc_ref, vc_ref, qp_ref, kp_ref, vp_ref, q0_ref, k0_ref, v0_ref,
                    wq_ref, wk_ref, wv_ref, ab_ref, alog_ref, dtb_ref,
                    val_ref, kcd_ref, qdec_ref, kdec_ref, qk_ref, eg_ref,
                    bufq, bufk, bufv, *, hg, dk, n_p, cps, cs, l_valid, width):
    C = CHUNK
    i = pl.program_id(1)
    _, c, _, nvalid = _item_info(i, n_p, cps, cs, l_valid)
    first = c == 0

    def conv_silu(cur_ref, prev_ref, init_ref, w_ref, buf):
        buf[0:SUBLANES, :] = jnp.where(first, init_ref[0], prev_ref[...])
        buf[SUBLANES:SUBLANES + C, :] = cur_ref[...]
        base = SUBLANES - (width - 1)
        y = buf[base:base + C, :] * w_ref[0:1, :]
        for t in range(1, width):
            y = y + buf[base + t:base + t + C, :] * w_ref[t:t + 1, :]
        return y * _sigmoid(y)

    qs = conv_silu(qc_ref, qp_ref, q0_ref, wq_ref, bufq)
    ks = conv_silu(kc_ref, kp_ref, k0_ref, wk_ref, bufk)
    vs = conv_silu(vc_ref, vp_ref, v0_ref, wv_ref, bufv)

    ab = ab_ref[...]
    rows = lax.broadcasted_iota(jnp.int32, (C, LANES), 0)
    valid = rows < nvalid
    xs = ab + dtb_ref[0]
    softplus = jnp.maximum(xs, 0.0) + jnp.log(1.0 + jnp.exp(-jnp.abs(xs)))
    g = jnp.where(valid, -jnp.exp(alog_ref[0]) * softplus, 0.0)
    beta = jnp.where(valid, _sigmoid(ab), 0.0)
    r = lax.broadcasted_iota(jnp.int32, (C, C), 0)
    cc = lax.broadcasted_iota(jnp.int32, (C, C), 1)
    incl = r >= cc
    strict = r > cc
    gcum = jnp.dot(jnp.where(incl, 1.0, 0.0), g, precision=_HI, preferred_element_type=F32)
    gcum_t = gcum.T

    heads = range(hg)
    sls = [slice(h * dk, (h + 1) * dk) for h in heads]
    qn = [qs[:, sl] * lax.rsqrt(jnp.sum(qs[:, sl] * qs[:, sl], axis=-1, keepdims=True) + NORM_EPS)
          * (dk ** -0.5) for sl in sls]
    kn = [ks[:, sl] * lax.rsqrt(jnp.sum(ks[:, sl] * ks[:, sl], axis=-1, keepdims=True) + NORM_EPS)
          for sl in sls]
    bcol = [beta[:, hg + h:hg + h + 1] for h in heads]
    gcol = [gcum[:, h:h + 1] for h in heads]
    glast = [gcum[C - 1:C, h:h + 1] for h in heads]
    eg = [jnp.exp(gcol[h]) for h in heads]
    decay = [jnp.where(incl, jnp.exp(jnp.where(incl, gcol[h] - gcum_t[h:h + 1, :], 0.0)), 0.0) for h in heads]
    kb = [kn[h] * bcol[h] for h in heads]
    knb = [kn[h].astype(BF16) for h in heads]
    a_mat = [jnp.where(strict, _dot_nt(kb[h].astype(BF16), knb[h]) * decay[h], 0.0) for h in heads]
    t_mat = _tri_inverse_many(a_mat)
    rhs = [jnp.concatenate([vs[:, sls[h]] * bcol[h], kb[h] * eg[h]], axis=1).astype(BF16) for h in heads]
    tv = [_dot(t_mat[h].astype(BF16), rhs[h]) for h in heads]
    qk = [_dot_nt(qn[h].astype(BF16), knb[h]) * decay[h] for h in heads]
    for h in heads:
        val_ref[:, sls[h]] = tv[h][:, :dk]
        kcd_ref[:, sls[h]] = tv[h][:, dk:].astype(BF16)
        qdec_ref[:, sls[h]] = (qn[h] * eg[h]).astype(BF16)
        kdec_ref[:, sls[h]] = (kn[h] * jnp.exp(glast[h] - gcol[h])).astype(BF16)
        eg_ref[0, 0, h:h + 1, :] = jnp.broadcast_to(jnp.exp(glast[h]), (1, LANES))
    qk_ref[...] = jnp.concatenate(qk, axis=1)


def _gdn_scan_kernel(val_ref, kcd_ref, qdec_ref, kdec_ref, qk_ref, eg_ref, z_ref, gain_ref, s0_ref,
                     o_ref, sout_ref, s_ref, *, hg, dk, n_p, cps, cs, l_valid):
    C = CHUNK
    i = pl.program_id(1)
    _, c, nc, _ = _item_info(i, n_p, cps, cs, l_valid)

    @pl.when(c == 0)
    def _():
        s_ref[...] = s0_ref[0]

    qk_all = qk_ref[...]
    gain = gain_ref[...]
    heads = range(hg)
    sls = [slice(h * dk, (h + 1) * dk) for h in heads]
    s_old = [s_ref[h] for h in heads]
    sb = [s.astype(BF16) for s in s_old]
    ks = [_dot(kcd_ref[:, sls[h]], sb[h]) for h in heads]
    qs = [_dot(qdec_ref[:, sls[h]], sb[h]) for h in heads]
    ub = [(val_ref[:, sls[h]] - ks[h]).astype(BF16) for h in heads]
    ds = [_dot_tn(kdec_ref[:, sls[h]], ub[h]) for h in heads]
    os_ = [qs[h] + _dot(qk_all[:, h * C:(h + 1) * C].astype(BF16), ub[h]) for h in heads]
    for h in heads:
        s_ref[h] = s_old[h] * eg_ref[0, 0, h:h + 1, :] + ds[h]
    for h in heads:
        o = os_[h]
        z = z_ref[:, sls[h]]
        ms = jnp.mean(o * o, axis=-1, keepdims=True)
        o_ref[:, sls[h]] = (o * lax.rsqrt(ms + NORM_EPS) * gain * (z * _sigmoid(z))).astype(o_ref.dtype)

    @pl.when(c == nc - 1)
    def _():
        sout_ref[0] = s_ref[...]


def _gdn_mixer(proj, gates, conv0, s0, conv_w, a_log, dt_bias, o_gain, *, heads, dk, hg, n_p, cps, cs, l_valid):
    m = proj.shape[0]
    C = CHUNK
    w = heads * dk
    n_hg = heads // hg
    gw = hg * dk
    n_items = m // C
    n_seq = s0.shape[0]
    width = conv_w.shape[0]
    info = dict(n_p=n_p, cps=cps, cs=cs, l_valid=l_valid)

    def seq_of(i):
        return _item_info(i, n_p, cps, cs, l_valid)[0]

    cur = lambda off: pl.BlockSpec((C, gw), lambda g, i: (i, off * n_hg + g))
    prev = lambda off: pl.BlockSpec(
        (SUBLANES, gw), lambda g, i: (jnp.maximum(i * (C // SUBLANES) - 1, 0), off * n_hg + g))
    init = lambda off: pl.BlockSpec((1, SUBLANES, gw), lambda g, i: (seq_of(i), 0, off * n_hg + g))
    wspec = lambda off: pl.BlockSpec((width, gw), lambda g, i: (0, off * n_hg + g))
    gate_vec = pl.BlockSpec((1, 1, LANES), lambda g, i: (g, 0, 0))
    tok = pl.BlockSpec((C, gw), lambda g, i: (i, g))

    alog_g = jnp.pad(a_log.reshape(n_hg, 1, hg).astype(F32), ((0, 0), (0, 0), (0, LANES - hg)))
    dtb_g = jnp.pad(dt_bias.reshape(n_hg, 1, hg).astype(F32), ((0, 0), (0, 0), (0, LANES - hg)))

    val, kcd, qdec, kdec, qk, eg = pl.pallas_call(
        functools.partial(_gdn_pre_kernel, hg=hg, dk=dk, width=width, **info),
        out_shape=(jax.ShapeDtypeStruct((m, w), F32),
                   jax.ShapeDtypeStruct((m, w), BF16),
                   jax.ShapeDtypeStruct((m, w), BF16),
                   jax.ShapeDtypeStruct((m, w), BF16),
                   jax.ShapeDtypeStruct((m, heads * C), F32),
                   jax.ShapeDtypeStruct((n_items, n_hg, hg, LANES), F32)),
        grid=(n_hg, n_items),
        in_specs=[cur(0), cur(1), cur(2), prev(0), prev(1), prev(2), init(0), init(1), init(2),
                  wspec(0), wspec(1), wspec(2),
                  pl.BlockSpec((C, LANES), lambda g, i: (i, g)),
                  gate_vec, gate_vec],
        out_specs=(tok, tok, tok, tok,
                   pl.BlockSpec((C, hg * C), lambda g, i: (i, g)),
                   pl.BlockSpec((1, 1, hg, LANES), lambda g, i: (i, g, 0, 0))),
        scratch_shapes=[pltpu.VMEM((SUBLANES + C, gw), F32)] * 3,
        compiler_params=_params(("parallel", "parallel")),
        name="gdn_prepare",
    )(proj, proj, proj, proj, proj, proj, conv0, conv0, conv0, conv_w, conv_w, conv_w,
      gates, alog_g, dtb_g)

    o, s_new = pl.pallas_call(
        functools.partial(_gdn_scan_kernel, hg=hg, dk=dk, **info),
        out_shape=(jax.ShapeDtypeStruct((m, w), BF16),
                   jax.ShapeDtypeStruct((n_seq, heads, dk, dk), F32)),
        grid=(n_hg, n_items),
        in_specs=[tok, tok, tok, tok,
                  pl.BlockSpec((C, hg * C), lambda g, i: (i, g)),
                  pl.BlockSpec((1, 1, hg, LANES), lambda g, i: (i, g, 0, 0)),
                  pl.BlockSpec((C, gw), lambda g, i: (i, 3 * n_hg + g)),
                  pl.BlockSpec((1, dk), lambda g, i: (0, 0)),
                  pl.BlockSpec((1, hg, dk, dk), lambda g, i: (seq_of(i), g, 0, 0))],
        out_specs=(tok, pl.BlockSpec((1, hg, dk, dk), lambda g, i: (seq_of(i), g, 0, 0))),
        scratch_shapes=[pltpu.VMEM((hg, dk, dk), F32)],
        compiler_params=_params(("parallel", "arbitrary")),
        name="gdn_scan",
    )(val, kcd, qdec, kdec, qk, eg, proj, o_gain.reshape(1, dk).astype(F32), s0)
    return o, s_new


def _split_half_layout(x, half):
    pad = [(0, 0)] * (x.ndim - 1) + [(0, LANES // 2 - half)]
    return jnp.concatenate([jnp.pad(x[..., :half], pad), jnp.pad(x[..., half:], pad)], axis=-1)


def _rope_tables(pos, half):
    inv_freq = 1.0 / (ROPE_THETA ** (jnp.arange(half, dtype=F32) / half))
    ang = pos[:, None] * inv_freq[None, :]
    cos, sin = jnp.cos(ang), jnp.sin(ang)
    cos_t = _split_half_layout(jnp.concatenate([cos, cos], -1), half)
    sin_t = _split_half_layout(jnp.concatenate([-sin, sin], -1), half)
    return cos_t, sin_t


def kernel(x_prompt, x_sample, state_conv, state_delta, cache_ckv, cache_krope, meta_tokens,
           a_w_in, a_conv_w, a_a_log, a_dt_bias, a_o_gain, a_w_o,
           b_w_dq, b_q_gain, b_w_uq, b_w_o,
           kv_w_dkv, kv_gain, kv_w_uk, kv_w_uv,
           mlp_w_up, mlp_w_down, ln_gain, ln_bias):
    B, seq, D = x_prompt.shape
    DB, ts, _ = x_sample.shape
    n_meta = meta_tokens.shape[0]
    depth = ln_gain.shape[0]
    n_a = a_w_in.shape[0]
    H = a_a_log.shape[1]
    dk = a_o_gain.shape[1]
    W = H * dk
    width = a_conv_w.shape[1]
    past = cache_ckv.shape[1]
    KV, MH, nope = kv_w_uk.shape
    vd = kv_w_uv.shape[2]
    rope = cache_krope.shape[2]
    half = rope // 2
    alpha = (2 * depth) ** 0.25
    scale = (nope + rope) ** -0.5
    assert ts % CHUNK == 0 and ts >= width - 1 and half < LANES // 2
    assert nope % LANES == 0 and vd % LANES == 0 and KV % LANES == 0 and dk % LANES == 0

    L = n_meta + seq
    LP = _round_up(L, LANES)
    n_prompt_rows = B * LP
    M = n_prompt_rows + DB * ts
    cps, cs = LP // CHUNK, ts // CHUNK
    n_p = B * cps
    hg = min(8, H)
    n_hg = H // hg

    meta = jnp.broadcast_to(meta_tokens.astype(F32)[None], (B, n_meta, D))
    xp = jnp.concatenate([meta, x_prompt, jnp.zeros((B, LP - L, D), F32)], axis=1)
    x = jnp.concatenate([xp.reshape(n_prompt_rows, D), x_sample.reshape(DB * ts, D)], axis=0)
    xb = x.astype(BF16)

    pos = jnp.concatenate([jnp.tile(jnp.arange(LP, dtype=F32), B),
                           jnp.tile(past + jnp.arange(ts, dtype=F32), DB)])
    cos_t, sin_t = _rope_tables(pos, half)

    conv0 = jnp.concatenate([jnp.zeros((n_a, B, width - 1, 3 * W), F32), state_conv.astype(F32)], axis=1)
    conv0 = jnp.pad(conv0, ((0, 0), (0, 0), (SUBLANES - (width - 1), 0), (0, 0)))
    s0 = jnp.concatenate([jnp.zeros((n_a, B, H, dk, dk), F32), state_delta.astype(F32)], axis=1)

    new_conv_p, new_conv_s, new_s = [], [], []
    k_slabs = None
    lat = None
    for layer in range(depth):
        if layer < n_a:
            w_ab = a_w_in[layer, :, 4 * W:]
            gate_w = jnp.concatenate(
                [w_ab[:, :H].reshape(D, n_hg, hg), w_ab[:, H:].reshape(D, n_hg, hg),
                 jnp.zeros((D, n_hg, LANES - 2 * hg), w_ab.dtype)], axis=-1).reshape(D, n_hg * LANES)
            proj = _matmul(xb, a_w_in, F32, layer=layer, n_use=4 * W)
            gates = _matmul(xb, gate_w.astype(BF16), F32)
            o, s_fin = _gdn_mixer(proj, gates, conv0[layer], s0[layer], a_conv_w[layer].astype(F32),
                                  a_a_log[layer], a_dt_bias[layer], a_o_gain[layer],
                                  heads=H, dk=dk, hg=hg, n_p=n_p, cps=cps, cs=cs, l_valid=L)
            h = _matmul(o, a_w_o, F32, layer=layer)
            new_conv_p.append(jnp.stack(
                [proj[b * LP + L - (width - 1):b * LP + L, :3 * W] for b in range(B)]))
            new_conv_s.append(jnp.stack(
                [proj[n_prompt_rows + ts - (width - 1) + t::ts, :3 * W] for t in range(width - 1)], axis=1))
            new_s.append(s_fin)
        else:
            j = layer - n_a
            ql = _matmul(xb, b_w_dq[j].astype(BF16), BF16, epilogue="rms", gain=b_q_gain[j])
            w_uq = b_w_uq[j].reshape(-1, MH, nope + rope)
            w_uq = jnp.concatenate([w_uq[..., :nope], _split_half_layout(w_uq[..., nope:], half)], axis=-1)
            q = _matmul(ql, w_uq.reshape(-1, MH * (nope + LANES)).astype(BF16), BF16)
            kp, vp, ks, vs, lkp, lks = k_slabs
            o_p = _attention_causal(q, cos_t, sin_t, kp, vp, nb=B, lq=LP, lk_pad=lkp, heads=MH, nope=nope,
                                    vd=vd, bq=_pick_block(LP, 1408, LANES), bk=512, scale=scale, n_meta=n_meta)
            o_s = _attention_full(q, cos_t, sin_t, ks, vs, nb=DB, lq=ts, lk_pad=lks, row0=n_prompt_rows,
                                  heads=MH, hb=min(4, MH), nope=nope, vd=vd, scale=scale, n_keys=past + ts)
            h = _matmul(jnp.concatenate([o_p, o_s], axis=0), b_w_o, F32, layer=j)

        x, xb = _deepnorm(x, h, ln_gain[layer, 0], ln_bias[layer, 0], alpha)
        hid = _matmul(xb, mlp_w_up, BF16, layer=layer, epilogue="relu2")
        h = _matmul(hid, mlp_w_down, F32, layer=layer, bn=1024, bk=2048)
        x, xb = _deepnorm(x, h, ln_gain[layer, 1], ln_bias[layer, 1], alpha)

        if layer == n_a - 1:
            w_dkv = jnp.concatenate([kv_w_dkv[:, :KV], _split_half_layout(kv_w_dkv[:, KV:], half)], axis=1)
            lat = _kv_post(_matmul(xb, w_dkv.astype(BF16), F32), kv_gain.astype(F32), cos_t, sin_t, KV)
            lane_ids = jnp.arange(LANES)
            eye = jnp.where((lane_ids[:, None] == lane_ids[None, :]) & (lane_ids[:, None] != half), 1.0, 0.0)
            ones_col = jnp.where((lane_ids[:, None] == half) & (lane_ids[None, :] == 0), 1.0, 0.0)
            w_k = jnp.concatenate(
                [jnp.concatenate([kv_w_uk.astype(F32), jnp.zeros((KV, MH, LANES), F32)], axis=-1),
                 jnp.concatenate([jnp.zeros((LANES, MH, nope), F32),
                                  jnp.broadcast_to(eye[:, None, :], (LANES, MH, LANES))], axis=-1)],
                axis=0).reshape(KV + LANES, MH * (nope + LANES)).astype(BF16)
            w_v = jnp.concatenate(
                [jnp.concatenate([kv_w_uv.astype(F32), jnp.zeros((KV, MH, LANES), F32)], axis=-1),
                 jnp.concatenate([jnp.zeros((LANES, MH, vd), F32),
                                  jnp.broadcast_to(ones_col[:, None, :], (LANES, MH, LANES))], axis=-1)],
                axis=0).reshape(KV + LANES, MH * (vd + LANES)).astype(BF16)
            one_lane = jnp.where(jnp.arange(KV + LANES) == KV + half, 1.0, 0.0)
            lkp = _round_up(L, 512)
            lat_p = lat[:n_prompt_rows].reshape(B, LP, KV + LANES)
            if lkp >= LP:
                lat_p = jnp.pad(lat_p, ((0, 0), (0, lkp - LP), (0, 0)))
            else:
                lat_p = lat_p[:, :lkp]
            lat_p = (lat_p.reshape(B * lkp, KV + LANES) + one_lane).astype(BF16)
            lks = _round_up(past + ts, LANES)
            cache = jnp.concatenate([cache_ckv.astype(F32), _split_half_layout(cache_krope.astype(F32), half)],
                                    axis=-1)
            lat_s = jnp.concatenate([cache, lat[n_prompt_rows:].reshape(DB, ts, KV + LANES),
                                     jnp.zeros((DB, lks - past - ts, KV + LANES), F32)], axis=1)
            lat_s = (lat_s.reshape(DB * lks, KV + LANES) + one_lane).astype(BF16)
            k_slabs = (_matmul(w_k.T, lat_p.T, BF16, bm=1024), _matmul(lat_p, w_v, BF16, bm=1024),
                       _matmul(w_k.T, lat_s.T, BF16, bm=1024), _matmul(lat_s, w_v, BF16, bm=1024), lkp, lks)

    def unsplit(r):
        return jnp.concatenate([r[..., :half], r[..., LANES // 2:LANES // 2 + half]], axis=-1)

    lat_p = lat[:n_prompt_rows].reshape(B, LP, KV + LANES)[:, :L]
    lat_s = lat[n_prompt_rows:].reshape(DB, ts, KV + LANES)
    y_prompt = jnp.stack([x[b * LP + n_meta:b * LP + L] for b in range(B)])
    y_sample = x[n_prompt_rows:].reshape(DB, ts, D)
    s_all = jnp.stack(new_s)
    return (y_prompt, y_sample,
            jnp.stack(new_conv_p), s_all[:, :B],
            lat_p[..., :KV], unsplit(lat_p[..., KV:]),
            jnp.stack(new_conv_s), s_all[:, B:],
            lat_s[..., :KV], unsplit(lat_s[..., KV:]))
```

```python
import functools
import math

import jax
import jax.numpy as jnp
from jax import lax
from jax.experimental import pallas as pl
from jax.experimental.pallas import tpu as pltpu

CHUNK = 64
ROPE_THETA = 10000.0
NORM_EPS = 1e-6
NEG_INF = -1e30
LANES = 128
SUBLANES = 8
VMEM_LIMIT_BYTES = 56 * 1024 * 1024
_CHUNK_SHIFT = CHUNK.bit_length() - 1
assert 1 << _CHUNK_SHIFT == CHUNK

F32 = jnp.float32
BF16 = jnp.bfloat16
_HI = lax.Precision.HIGHEST


def _round_up(x, m):
    return -(-x // m) * m


def _pick_block(dim, target, align):
    best = None
    for d in range(align, min(dim, target) + 1, align):
        if dim % d == 0:
            best = d
    return best if best is not None else dim


def _dot(a, b):
    return jnp.dot(a, b, preferred_element_type=F32)


def _dot_nt(a, b):
    return lax.dot_general(a, b, (((1,), (1,)), ((), ())), preferred_element_type=F32)


def _dot_tn(a, b):
    return lax.dot_general(a, b, (((0,), (0,)), ((), ())), preferred_element_type=F32)


def _sigmoid(x):
    return 1.0 / (1.0 + jnp.exp(-x))


def _params(sem):
    return pltpu.CompilerParams(dimension_semantics=sem, vmem_limit_bytes=VMEM_LIMIT_BYTES)


def _mm_kernel(*refs, nk, epilogue):
    if epilogue == "rms":
        a_ref, b_ref, g_ref, o_ref = refs[:4]
        rest = refs[4:]
    else:
        a_ref, b_ref, o_ref = refs[:3]
        g_ref = None
        rest = refs[3:]

    def finish(acc):
        if epilogue == "relu2":
            r = jnp.maximum(acc, 0.0)
            acc = r * r
        elif epilogue == "rms":
            ms = jnp.mean(acc * acc, axis=-1, keepdims=True)
            acc = acc * lax.rsqrt(ms + NORM_EPS) * g_ref[...]
        o_ref[...] = acc.astype(o_ref.dtype)

    def product():
        return _dot(a_ref[...], b_ref[...].astype(BF16))

    if nk == 1:
        finish(product())
    else:
        acc_ref = rest[0]
        k = pl.program_id(2)

        @pl.when(k == 0)
        def _():
            acc_ref[...] = jnp.zeros_like(acc_ref)

        acc_ref[...] += product()

        @pl.when(k == nk - 1)
        def _():
            finish(acc_ref[...])


def _matmul(a, b, out_dtype, *, layer=None, n_use=None, epilogue="none", gain=None, bm=1312, bn=512, bk=4096):
    m, kdim = a.shape
    n = n_use if n_use is not None else b.shape[-1]
    bm = _pick_block(m, bm, 16)
    bn = n if epilogue == "rms" else _pick_block(n, bn, LANES)
    bk = _pick_block(kdim, bk, LANES)
    nk = kdim // bk
    a_mode = dict(pipeline_mode=pl.Buffered(1)) if nk == 1 else {}
    if b.ndim == 3:
        b_spec = pl.BlockSpec((None, bk, bn), lambda i, j, k: (layer, k, j))
    else:
        b_spec = pl.BlockSpec((bk, bn), lambda i, j, k: (k, j))
    in_specs = [pl.BlockSpec((bm, bk), lambda i, j, k: (i, k), **a_mode), b_spec]
    args = [a, b]
    if epilogue == "rms":
        in_specs.append(pl.BlockSpec((1, bn), lambda i, j, k: (0, j)))
        args.append(gain.reshape(1, n).astype(F32))
    scratch = [pltpu.VMEM((bm, bn), F32)] if nk > 1 else []
    return pl.pallas_call(
        functools.partial(_mm_kernel, nk=nk, epilogue=epilogue),
        out_shape=jax.ShapeDtypeStruct((m, n), out_dtype),
        grid=(m // bm, n // bn, nk),
        in_specs=in_specs,
        out_specs=pl.BlockSpec((bm, bn), lambda i, j, k: (i, j)),
        scratch_shapes=scratch,
        compiler_params=_params(("parallel", "parallel", "arbitrary")),
        name="matmul_" + epilogue,
    )(*args)


def _ln_kernel(x_ref, h_ref, g_ref, b_ref, of_ref, ob_ref, *, alpha):
    y = alpha * x_ref[...] + h_ref[...].astype(F32)
    mu = jnp.mean(y, axis=-1, keepdims=True)
    d = y - mu
    var = jnp.mean(d * d, axis=-1, keepdims=True)
    out = d * lax.rsqrt(var + NORM_EPS) * g_ref[...] + b_ref[...]
    of_ref[...] = out
    ob_ref[...] = out.astype(BF16)


def _deepnorm(x, h, gain, bias, alpha):
    m, d = x.shape
    br = _pick_block(m, 256, 16)
    row = pl.BlockSpec((br, d), lambda i: (i, 0))
    vec = pl.BlockSpec((1, d), lambda i: (0, 0))
    return pl.pallas_call(
        functools.partial(_ln_kernel, alpha=alpha),
        out_shape=(jax.ShapeDtypeStruct((m, d), F32), jax.ShapeDtypeStruct((m, d), BF16)),
        grid=(m // br,),
        in_specs=[row, row, vec, vec],
        out_specs=(row, row),
        compiler_params=_params(("parallel",)),
        name="deepnorm",
    )(x, h, gain.reshape(1, d), bias.reshape(1, d))


def _kv_post_kernel(lat_ref, g_ref, cos_ref, sin_ref, o_ref, *, kv):
    c = lat_ref[:, :kv]
    ms = jnp.mean(c * c, axis=-1, keepdims=True)
    o_ref[:, :kv] = c * lax.rsqrt(ms + NORM_EPS) * g_ref[...]
    r = lat_ref[:, kv:]
    o_ref[:, kv:] = r * cos_ref[...] + pltpu.roll(r, LANES // 2, axis=1) * sin_ref[...]


def _kv_post(lat, gain, cos_t, sin_t, kv):
    m, w = lat.shape
    br = _pick_block(m, 256, 8)
    return pl.pallas_call(
        functools.partial(_kv_post_kernel, kv=kv),
        out_shape=jax.ShapeDtypeStruct((m, w), F32),
        grid=(m // br,),
        in_specs=[pl.BlockSpec((br, w), lambda i: (i, 0)),
                  pl.BlockSpec((1, kv), lambda i: (0, 0)),
                  pl.BlockSpec((br, LANES), lambda i: (i, 0)),
                  pl.BlockSpec((br, LANES), lambda i: (i, 0))],
        out_specs=pl.BlockSpec((br, w), lambda i: (i, 0)),
        compiler_params=_params(("parallel",)),
        name="kv_post",
    )(lat, gain.reshape(1, kv), cos_t, sin_t)


def _rope_query(q, cos, sin, nope, scale):
    q = q.astype(F32)
    qr = q[:, nope:]
    qr = qr * cos + pltpu.roll(qr, LANES // 2, axis=1) * sin
    return (jnp.concatenate([q[:, :nope], qr], axis=1) * (scale * math.log2(math.e))).astype(BF16)


def _attn_causal_kernel(q_ref, cos_ref, sin_ref, kt_ref, v_ref, o_ref,
                        q_scr, s_scr, p_scr, m_scr, a_scr, acc_scr, *, bq, bk, rt, n_groups, nope, vd, scale,
                        n_meta, lk_pad):
    qi = pl.program_id(2)
    tiles = [slice(t * rt, (t + 1) * rt) for t in range(bq // rt)]
    lane_chunks = [slice(c * LANES, (c + 1) * LANES) for c in range(bk // LANES)]

    def visible_end(r):
        chunk_id = lax.shift_right_arithmetic(r - n_meta, _CHUNK_SHIFT) + 1
        return jnp.where(r < n_meta, n_meta, chunk_id * CHUNK + n_meta)

    n_full = visible_end(qi * bq) // bk
    nkb = (jnp.minimum(visible_end(qi * bq + bq - 1), lk_pad) + bk - 1) // bk

    for rows in tiles:
        q_scr[rows, :] = _rope_query(q_ref[rows, :], cos_ref[rows, :], sin_ref[rows, :], nope, scale)
    m_scr[...] = jnp.full(m_scr.shape, NEG_INF, F32)
    acc_scr[...] = jnp.zeros(acc_scr.shape, F32)

    gsz = bq // n_groups
    groups = [slice(g * gsz, (g + 1) * gsz) for g in range(n_groups)]
    tiles_per_group = gsz // rt

    def step(j, carry, masked):
        start = pl.multiple_of(j * bk, bk)
        kt_blk = kt_ref[:, pl.ds(start, bk)]
        v_blk = v_ref[pl.ds(start, bk), :]

        def scores(g):
            s_scr[groups[g], :] = _dot(q_scr[groups[g], :], kt_blk)

        def softmax(g):
            for t in range(g * tiles_per_group, (g + 1) * tiles_per_group):
                rows = tiles[t]
                s = s_scr[rows, :]
                if masked:
                    kpos = start + lax.broadcasted_iota(jnp.int32, (1, bk), 1)
                    vis = visible_end(qi * bq + t * rt + lax.broadcasted_iota(jnp.int32, (rt, 1), 0))
                    s = jnp.where(kpos < vis, s, NEG_INF)
                m_old = m_scr[rows, :]
                m_new = jnp.maximum(m_old, jnp.max(s, axis=-1, keepdims=True))
                for lc in lane_chunks:
                    p_scr[rows, lc] = jnp.exp2(s[:, lc] - m_new).astype(BF16)
                a_scr[rows, :] = jnp.exp2(m_old - m_new)
                m_scr[rows, :] = m_new

        def values(g):
            rows = groups[g]
            a = a_scr[rows, :]
            acc_scr[rows, :] = (jnp.concatenate([a] * (acc_scr.shape[1] // LANES), axis=1) * acc_scr[rows, :]
                                + _dot(p_scr[rows, :], v_blk))

        scores(0)
        for g in range(n_groups):
            if g + 1 < n_groups:
                scores(g + 1)
            softmax(g)
            values(g)
        return carry

    lax.fori_loop(0, n_full, functools.partial(step, masked=False), 0)
    lax.fori_loop(n_full, nkb, functools.partial(step, masked=True), 0)
    for rows in tiles:
        o_ref[rows, :] = (acc_scr[rows, :vd] / acc_scr[rows, vd:vd + 1]).astype(o_ref.dtype)


def _attn_full_kernel(q_ref, cos_ref, sin_ref, knt_ref, krt_ref, v_ref, o_ref, *, hb, nope, vd, scale, n_keys):
    dqk = nope + LANES
    lq = q_ref.shape[0]
    heads = range(hb)
    cos, sin = cos_ref[...], sin_ref[...]
    kpos = lax.broadcasted_iota(jnp.int32, (1, knt_ref.shape[1]), 1)
    qs = [_rope_query(q_ref[:, h * dqk:(h + 1) * dqk], cos, sin, nope, scale) for h in heads]
    s_rope = _dot(jnp.concatenate([q[:, nope:] for q in qs], axis=0), krt_ref[...])
    ss = [jnp.where(kpos < n_keys,
                    _dot(qs[h][:, :nope], knt_ref[h * nope:(h + 1) * nope, :]) + s_rope[h * lq:(h + 1) * lq],
                    NEG_INF) for h in heads]
    es = [jnp.exp2(s - jnp.max(s, axis=-1, keepdims=True)) for s in ss]
    ls = [jnp.sum(e, axis=-1, keepdims=True) for e in es]
    pv = [_dot(es[h].astype(BF16), v_ref[:, h * vd:(h + 1) * vd]) for h in heads]
    for h in heads:
        o_ref[:, h * vd:(h + 1) * vd] = (pv[h] / ls[h]).astype(o_ref.dtype)


def _attention_causal(q, cos_t, sin_t, kt, vf, *, nb, lq, lk_pad, heads, nope, vd, bq, bk, scale, n_meta):
    dqk = nope + LANES
    nq = lq // bq
    rt = 32 if bq % 32 == 0 else bq
    n_groups = 4 if bq % (4 * rt) == 0 else 1
    return pl.pallas_call(
        functools.partial(_attn_causal_kernel, bq=bq, bk=bk, rt=rt, n_groups=n_groups, nope=nope, vd=vd,
                          scale=scale,
                          n_meta=n_meta, lk_pad=lk_pad),
        out_shape=jax.ShapeDtypeStruct((nb * lq, heads * vd), BF16),
        grid=(nb, heads, nq),
        in_specs=[pl.BlockSpec((bq, dqk), lambda b, h, i: (b * nq + i, h)),
                  pl.BlockSpec((bq, LANES), lambda b, h, i: (b * nq + i, 0)),
                  pl.BlockSpec((bq, LANES), lambda b, h, i: (b * nq + i, 0)),
                  pl.BlockSpec((dqk, lk_pad), lambda b, h, i: (h, b)),
                  pl.BlockSpec((lk_pad, vd + LANES), lambda b, h, i: (b, h))],
        out_specs=pl.BlockSpec((bq, vd), lambda b, h, i: (b * nq + i, h)),
        scratch_shapes=[pltpu.VMEM((bq, dqk), BF16), pltpu.VMEM((bq, bk), F32), pltpu.VMEM((bq, bk), BF16),
                        pltpu.VMEM((bq, LANES), F32), pltpu.VMEM((bq, LANES), F32),
                        pltpu.VMEM((bq, vd + LANES), F32)],
        compiler_params=_params(("parallel", "parallel", "arbitrary")),
        name="mla_attention_causal",
    )(q, cos_t, sin_t, kt, vf)


def _attention_full(q, cos_t, sin_t, knt, lat_t, vf, *, nb, lq, lk_pad, row0, heads, hb, nope, vd, kv, scale,
                    n_keys):
    dqk = nope + LANES
    blk0 = row0 // lq
    return pl.pallas_call(
        functools.partial(_attn_full_kernel, hb=hb, nope=nope, vd=vd, scale=scale, n_keys=n_keys),
        out_shape=jax.ShapeDtypeStruct((nb * lq, heads * vd), BF16),
        grid=(nb, heads // hb),
        in_specs=[pl.BlockSpec((lq, hb * dqk), lambda b, g: (blk0 + b, g)),
                  pl.BlockSpec((lq, LANES), lambda b, g: (blk0 + b, 0)),
                  pl.BlockSpec((lq, LANES), lambda b, g: (blk0 + b, 0)),
                  pl.BlockSpec((hb * nope, lk_pad), lambda b, g: (g, b)),
                  pl.BlockSpec((LANES, lk_pad), lambda b, g: (kv // LANES, b)),
                  pl.BlockSpec((lk_pad, hb * vd), lambda b, g: (b, g))],
        out_specs=pl.BlockSpec((lq, hb * vd), lambda b, g: (b, g)),
        compiler_params=_params(("parallel", "parallel")),
        name="mla_attention_full",
    )(q, cos_t, sin_t, knt, lat_t, vf)


def _item_info(i, n_p, cps, cs, l_valid):
    is_p = i < n_p
    j = i - n_p
    seq = jnp.where(is_p, i // cps, n_p // cps + j // cs)
    c = jnp.where(is_p, i % cps, j % cs)
    nc = jnp.where(is_p, cps, cs)
    nvalid = jnp.where(is_p, jnp.clip(l_valid - c * CHUNK, 0, CHUNK), CHUNK)
    return seq, c, nc, nvalid


def _tri_inverse_many(mats):
    n = mats[0].shape[0]
    r = lax.broadcasted_iota(jnp.int32, (n, n), 0)
    c = lax.broadcasted_iota(jnp.int32, (n, n), 1)
    eye = jnp.where(r == c, 1.0, 0.0)
    zero = jnp.zeros((n, n), F32)

    def split(x):
        hi = x.astype(BF16).astype(F32)
        return hi, x - hi

    def lhs_of(x):
        hi, lo = split(x)
        return jnp.concatenate([hi, lo, hi, zero], axis=1).astype(BF16)

    def rhs_of(x):
        hi, lo = split(x)
        return jnp.concatenate([hi, hi, lo, zero], axis=0).astype(BF16)

    ps = [eye - a for a in mats]
    aks = list(mats)
    rhs = [rhs_of(a) for a in mats]
    power = 2
    while power < n:
        aks = [_dot(lhs_of(ak), rk) for ak, rk in zip(aks, rhs)]
        rhs = [rhs_of(ak) for ak in aks]
        ps = [p + _dot(lhs_of(p), rk) for p, rk in zip(ps, rhs)]
        power *= 2
    return ps


def _gdn_pre_kernel(qc_ref, kc_ref, vc_ref, qp_ref, kp_ref, vp_ref, q0_ref, k0_ref, v0_ref,
                    wq_ref, wk_ref, wv_ref, ab_ref, alog_ref, dtb_ref,
                    val_ref, kcd_ref, qdec_ref, kdec_ref, qk_ref, eg_ref,
                    bufq, bufk, bufv, *, hg, dk, n_p, cps, cs, l_valid, width):
    C = CHUNK
    i = pl.program_id(1)
    _, c, _, nvalid = _item_info(i, n_p, cps, cs, l_valid)
    first = c == 0

    def conv_silu(cur_ref, prev_ref, init_ref, w_ref, buf):
        buf[0:SUBLANES, :] = jnp.where(first, init_ref[0], prev_ref[...])
        buf[SUBLANES:SUBLANES + C, :] = cur_ref[...]
        base = SUBLANES - (width - 1)
        y = buf[base:base + C, :] * w_ref[0:1, :]
        for t in range(1, width):
            y = y + buf[base + t:base + t + C, :] * w_ref[t:t + 1, :]
        return y * _sigmoid(y)

    qs = conv_silu(qc_ref, qp_ref, q0_ref, wq_ref, bufq)
    ks = conv_silu(kc_ref, kp_ref, k0_ref, wk_ref, bufk)
    vs = conv_silu(vc_ref, vp_ref, v0_ref, wv_ref, bufv)

    ab = ab_ref[...]
    rows = lax.broadcasted_iota(jnp.int32, (C, LANES), 0)
    valid = rows < nvalid
    xs = ab + dtb_ref[0]
    softplus = jnp.maximum(xs, 0.0) + jnp.log(1.0 + jnp.exp(-jnp.abs(xs)))
    g = jnp.where(valid, -jnp.exp(alog_ref[0]) * softplus, 0.0)
    beta = jnp.where(valid, _sigmoid(ab), 0.0)
    r = lax.broadcasted_iota(jnp.int32, (C, C), 0)
    cc = lax.broadcasted_iota(jnp.int32, (C, C), 1)
    incl = r >= cc
    strict = r > cc
    gcum = jnp.dot(jnp.where(incl, 1.0, 0.0), g, precision=_HI, preferred_element_type=F32)
    gcum_t = gcum.T

    heads = range(hg)
    sls = [slice(h * dk, (h + 1) * dk) for h in heads]
    qn = [qs[:, sl] * lax.rsqrt(jnp.sum(qs[:, sl] * qs[:, sl], axis=-1, keepdims=True) + NORM_EPS)
          * (dk ** -0.5) for sl in sls]
    kn = [ks[:, sl] * lax.rsqrt(jnp.sum(ks[:, sl] * ks[:, sl], axis=-1, keepdims=True) + NORM_EPS)
          for sl in sls]
    bcol = [beta[:, hg + h:hg + h + 1] for h in heads]
    gcol = [gcum[:, h:h + 1] for h in heads]
    glast = [gcum[C - 1:C, h:h + 1] for h in heads]
    eg = [jnp.exp(gcol[h]) for h in heads]
    decay = [jnp.where(incl, jnp.exp(jnp.where(incl, gcol[h] - gcum_t[h:h + 1, :], 0.0)), 0.0) for h in heads]
    kb = [kn[h] * bcol[h] for h in heads]
    kn_t = [kn[h].T.astype(BF16) for h in heads]
    kq = [_dot(jnp.concatenate([kb[h], qn[h]], axis=0).astype(BF16), kn_t[h]) for h in heads]
    a_mat = [jnp.where(strict, kq[h][:C] * decay[h], 0.0) for h in heads]
    t_mat = _tri_inverse_many(a_mat)
    rhs = [jnp.concatenate([vs[:, sls[h]] * bcol[h], kb[h] * eg[h]], axis=1).astype(BF16) for h in heads]
    tv = [_dot(t_mat[h].astype(BF16), rhs[h]) for h in heads]
    qk = [kq[h][C:] * decay[h] for h in heads]
    for h in heads:
        val_ref[:, sls[h]] = tv[h][:, :dk]
        kcd_ref[:, sls[h]] = tv[h][:, dk:].astype(BF16)
        qdec_ref[:, sls[h]] = (qn[h] * eg[h]).astype(BF16)
        kdec_ref[:, sls[h]] = (kn[h] * jnp.exp(glast[h] - gcol[h])).astype(BF16)
        eg_ref[0, 0, h:h + 1, :] = jnp.broadcast_to(jnp.exp(glast[h]), (1, LANES))
    qk_ref[...] = jnp.concatenate(qk, axis=1)


def _gdn_scan_kernel(val_ref, kcd_ref, qdec_ref, kdec_ref, qk_ref, eg_ref, z_ref, gain_ref, s0_ref,
                     o_ref, sout_ref, s_ref, *, hg, dk, n_p, cps, cs, l_valid):
    C = CHUNK
    i = pl.program_id(1)
    _, c, nc, _ = _item_info(i, n_p, cps, cs, l_valid)

    @pl.when(c == 0)
    def _():
        s_ref[...] = s0_ref[0]

    qk_all = qk_ref[...]
    gain = gain_ref[...]
    heads = range(hg)
    sls = [slice(h * dk, (h + 1) * dk) for h in heads]
    s_old = [s_ref[h] for h in heads]
    sb = [s.astype(BF16) for s in s_old]
    ks = [_dot(kcd_ref[:, sls[h]], sb[h]) for h in heads]
    qs = [_dot(qdec_ref[:, sls[h]], sb[h]) for h in heads]
    ub = [(val_ref[:, sls[h]] - ks[h]).astype(BF16) for h in heads]
    ds = [_dot_tn(kdec_ref[:, sls[h]], ub[h]) for h in heads]
    os_ = [qs[h] + _dot(qk_all[:, h * C:(h + 1) * C].astype(BF16), ub[h]) for h in heads]
    for h in heads:
        s_ref[h] = s_old[h] * eg_ref[0, 0, h:h + 1, :] + ds[h]
    for h in heads:
        o = os_[h]
        z = z_ref[:, sls[h]]
        ms = jnp.mean(o * o, axis=-1, keepdims=True)
        o_ref[:, sls[h]] = (o * lax.rsqrt(ms + NORM_EPS) * gain * (z * _sigmoid(z))).astype(o_ref.dtype)

    @pl.when(c == nc - 1)
    def _():
        sout_ref[0] = s_ref[...]


def _gdn_mixer(proj, gates, conv0, s0, conv_w, a_log, dt_bias, o_gain, *, heads, dk, hg, n_p, cps, cs, l_valid):
    m = proj.shape[0]
    C = CHUNK
    w = heads * dk
    n_hg = heads // hg
    gw = hg * dk
    n_items = m // C
    n_seq = s0.shape[0]
    width = conv_w.shape[0]
    info = dict(n_p=n_p, cps=cps, cs=cs, l_valid=l_valid)

    def seq_of(i):
        return _item_info(i, n_p, cps, cs, l_valid)[0]

    cur = lambda off: pl.BlockSpec((C, gw), lambda g, i: (i, off * n_hg + g))
    prev = lambda off: pl.BlockSpec(
        (SUBLANES, gw), lambda g, i: (jnp.maximum(i * (C // SUBLANES) - 1, 0), off * n_hg + g))
    init = lambda off: pl.BlockSpec((1, SUBLANES, gw), lambda g, i: (seq_of(i), 0, off * n_hg + g))
    wspec = lambda off: pl.BlockSpec((width, gw), lambda g, i: (0, off * n_hg + g))
    gate_vec = pl.BlockSpec((1, 1, LANES), lambda g, i: (g, 0, 0))
    tok = pl.BlockSpec((C, gw), lambda g, i: (i, g))

    alog_g = jnp.pad(a_log.reshape(n_hg, 1, hg).astype(F32), ((0, 0), (0, 0), (0, LANES - hg)))
    dtb_g = jnp.pad(dt_bias.reshape(n_hg, 1, hg).astype(F32), ((0, 0), (0, 0), (0, LANES - hg)))

    val, kcd, qdec, kdec, qk, eg = pl.pallas_call(
        functools.partial(_gdn_pre_kernel, hg=hg, dk=dk, width=width, **info),
        out_shape=(jax.ShapeDtypeStruct((m, w), F32),
                   jax.ShapeDtypeStruct((m, w), BF16),
                   jax.ShapeDtypeStruct((m, w), BF16),
                   jax.ShapeDtypeStruct((m, w), BF16),
                   jax.ShapeDtypeStruct((m, heads * C), F32),
                   jax.ShapeDtypeStruct((n_items, n_hg, hg, LANES), F32)),
        grid=(n_hg, n_items),
        in_specs=[cur(0), cur(1), cur(2), prev(0), prev(1), prev(2), init(0), init(1), init(2),
                  wspec(0), wspec(1), wspec(2),
                  pl.BlockSpec((C, LANES), lambda g, i: (i, g)),
                  gate_vec, gate_vec],
        out_specs=(tok, tok, tok, tok,
                   pl.BlockSpec((C, hg * C), lambda g, i: (i, g)),
                   pl.BlockSpec((1, 1, hg, LANES), lambda g, i: (i, g, 0, 0))),
        scratch_shapes=[pltpu.VMEM((SUBLANES + C, gw), F32)] * 3,
        compiler_params=_params(("parallel", "parallel")),
        name="gdn_prepare",
    )(proj, proj, proj, proj, proj, proj, conv0, conv0, conv0, conv_w, conv_w, conv_w,
      gates, alog_g, dtb_g)

    o, s_new = pl.pallas_call(
        functools.partial(_gdn_scan_kernel, hg=hg, dk=dk, **info),
        out_shape=(jax.ShapeDtypeStruct((m, w), BF16),
                   jax.ShapeDtypeStruct((n_seq, heads, dk, dk), F32)),
        grid=(n_hg, n_items),
        in_specs=[tok, tok, tok, tok,
                  pl.BlockSpec((C, hg * C), lambda g, i: (i, g)),
                  pl.BlockSpec((1, 1, hg, LANES), lambda g, i: (i, g, 0, 0)),
                  pl.BlockSpec((C, gw), lambda g, i: (i, 3 * n_hg + g)),
                  pl.BlockSpec((1, dk), lambda g, i: (0, 0)),
                  pl.BlockSpec((1, hg, dk, dk), lambda g, i: (seq_of(i), g, 0, 0))],
        out_specs=(tok, pl.BlockSpec((1, hg, dk, dk), lambda g, i: (seq_of(i), g, 0, 0))),
        scratch_shapes=[pltpu.VMEM((hg, dk, dk), F32)],
        compiler_params=_params(("parallel", "arbitrary")),
        name="gdn_scan",
    )(val, kcd, qdec, kdec, qk, eg, proj, o_gain.reshape(1, dk).astype(F32), s0)
    return o, s_new


def _split_half_layout(x, half):
    pad = [(0, 0)] * (x.ndim - 1) + [(0, LANES // 2 - half)]
    return jnp.concatenate([jnp.pad(x[..., :half], pad), jnp.pad(x[..., half:], pad)], axis=-1)


def _rope_tables(pos, half):
    inv_freq = 1.0 / (ROPE_THETA ** (jnp.arange(half, dtype=F32) / half))
    ang = pos[:, None] * inv_freq[None, :]
    cos, sin = jnp.cos(ang), jnp.sin(ang)
    cos_t = _split_half_layout(jnp.concatenate([cos, cos], -1), half)
    sin_t = _split_half_layout(jnp.concatenate([-sin, sin], -1), half)
    return cos_t, sin_t


def kernel(x_prompt, x_sample, state_conv, state_delta, cache_ckv, cache_krope, meta_tokens,
           a_w_in, a_conv_w, a_a_log, a_dt_bias, a_o_gain, a_w_o,
           b_w_dq, b_q_gain, b_w_uq, b_w_o,
           kv_w_dkv, kv_gain, kv_w_uk, kv_w_uv,
           mlp_w_up, mlp_w_down, ln_gain, ln_bias):
    B, seq, D = x_prompt.shape
    DB, ts, _ = x_sample.shape
    n_meta = meta_tokens.shape[0]
    depth = ln_gain.shape[0]
    n_a = a_w_in.shape[0]
    H = a_a_log.shape[1]
    dk = a_o_gain.shape[1]
    W = H * dk
    width = a_conv_w.shape[1]
    past = cache_ckv.shape[1]
    KV, MH, nope = kv_w_uk.shape
    vd = kv_w_uv.shape[2]
    rope = cache_krope.shape[2]
    half = rope // 2
    alpha = (2 * depth) ** 0.25
    scale = (nope + rope) ** -0.5
    assert ts % CHUNK == 0 and ts >= width - 1 and half < LANES // 2
    assert nope % LANES == 0 and vd % LANES == 0 and KV % LANES == 0 and dk % LANES == 0

    L = n_meta + seq
    LP = _round_up(L, LANES)
    n_prompt_rows = B * LP
    M = n_prompt_rows + DB * ts
    cps, cs = LP // CHUNK, ts // CHUNK
    n_p = B * cps
    hg = min(8, H)
    n_hg = H // hg

    meta = jnp.broadcast_to(meta_tokens.astype(F32)[None], (B, n_meta, D))
    xp = jnp.concatenate([meta, x_prompt, jnp.zeros((B, LP - L, D), F32)], axis=1)
    x = jnp.concatenate([xp.reshape(n_prompt_rows, D), x_sample.reshape(DB * ts, D)], axis=0)
    xb = x.astype(BF16)

    pos = jnp.concatenate([jnp.tile(jnp.arange(LP, dtype=F32), B),
                           jnp.tile(past + jnp.arange(ts, dtype=F32), DB)])
    cos_t, sin_t = _rope_tables(pos, half)

    conv0 = jnp.concatenate([jnp.zeros((n_a, B, width - 1, 3 * W), F32), state_conv.astype(F32)], axis=1)
    conv0 = jnp.pad(conv0, ((0, 0), (0, 0), (SUBLANES - (width - 1), 0), (0, 0)))
    s0 = jnp.concatenate([jnp.zeros((n_a, B, H, dk, dk), F32), state_delta.astype(F32)], axis=1)

    new_conv_p, new_conv_s, new_s = [], [], []
    k_slabs = None
    lat = None
    for layer in range(depth):
        if layer < n_a:
            w_ab = a_w_in[layer, :, 4 * W:]
            gate_w = jnp.concatenate(
                [w_ab[:, :H].reshape(D, n_hg, hg), w_ab[:, H:].reshape(D, n_hg, hg),
                 jnp.zeros((D, n_hg, LANES - 2 * hg), w_ab.dtype)], axis=-1).reshape(D, n_hg * LANES)
            proj = _matmul(xb, a_w_in, F32, layer=layer, n_use=4 * W)
            gates = _matmul(xb, gate_w.astype(BF16), F32)
            o, s_fin = _gdn_mixer(proj, gates, conv0[layer], s0[layer], a_conv_w[layer].astype(F32),
                                  a_a_log[layer], a_dt_bias[layer], a_o_gain[layer],
                                  heads=H, dk=dk, hg=hg, n_p=n_p, cps=cps, cs=cs, l_valid=L)
            h = _matmul(o, a_w_o, BF16, layer=layer)
            new_conv_p.append(jnp.stack(
                [proj[b * LP + L - (width - 1):b * LP + L, :3 * W] for b in range(B)]))
            new_conv_s.append(jnp.stack(
                [proj[n_prompt_rows + ts - (width - 1) + t::ts, :3 * W] for t in range(width - 1)], axis=1))
            new_s.append(s_fin)
        else:
            j = layer - n_a
            ql = _matmul(xb, b_w_dq[j].astype(BF16), BF16, epilogue="rms", gain=b_q_gain[j])
            w_uq = b_w_uq[j].reshape(-1, MH, nope + rope)
            w_uq = jnp.concatenate([w_uq[..., :nope], _split_half_layout(w_uq[..., nope:], half)], axis=-1)
            q = _matmul(ql, w_uq.reshape(-1, MH * (nope + LANES)).astype(BF16), BF16)
            kp, vp, ks, ls_t, vs, lkp, lks = k_slabs
            o_p = _attention_causal(q, cos_t, sin_t, kp, vp, nb=B, lq=LP, lk_pad=lkp, heads=MH, nope=nope,
                                    vd=vd, bq=_pick_block(LP, 1408, LANES), bk=512, scale=scale, n_meta=n_meta)
            o_s = _attention_full(q, cos_t, sin_t, ks, ls_t, vs, nb=DB, lq=ts, lk_pad=lks, row0=n_prompt_rows,
                                  heads=MH, hb=min(4, MH), nope=nope, vd=vd, kv=KV, scale=scale,
                                  n_keys=past + ts)
            h = _matmul(jnp.concatenate([o_p, o_s], axis=0), b_w_o, BF16, layer=j)

        x, xb = _deepnorm(x, h, ln_gain[layer, 0], ln_bias[layer, 0], alpha)
        hid = _matmul(xb, mlp_w_up, BF16, layer=layer, epilogue="relu2")
        h = _matmul(hid, mlp_w_down, BF16, layer=layer, bn=1024, bk=2048)
        x, xb = _deepnorm(x, h, ln_gain[layer, 1], ln_bias[layer, 1], alpha)

        if layer == n_a - 1:
            w_dkv = jnp.concatenate([kv_w_dkv[:, :KV], _split_half_layout(kv_w_dkv[:, KV:], half)], axis=1)
            lat = _kv_post(_matmul(xb, w_dkv.astype(BF16), F32), kv_gain.astype(F32), cos_t, sin_t, KV)
            lane_ids = jnp.arange(LANES)
            eye = jnp.where((lane_ids[:, None] == lane_ids[None, :]) & (lane_ids[:, None] != half), 1.0, 0.0)
            ones_col = jnp.where((lane_ids[:, None] == half) & (lane_ids[None, :] == 0), 1.0, 0.0)
            w_k = jnp.concatenate(
                [jnp.concatenate([kv_w_uk.astype(F32), jnp.zeros((KV, MH, LANES), F32)], axis=-1),
                 jnp.concatenate([jnp.zeros((LANES, MH, nope), F32),
                                  jnp.broadcast_to(eye[:, None, :], (LANES, MH, LANES))], axis=-1)],
                axis=0).reshape(KV + LANES, MH * (nope + LANES)).astype(BF16)
            w_v = jnp.concatenate(
                [jnp.concatenate([kv_w_uv.astype(F32), jnp.zeros((KV, MH, LANES), F32)], axis=-1),
                 jnp.concatenate([jnp.zeros((LANES, MH, vd), F32),
                                  jnp.broadcast_to(ones_col[:, None, :], (LANES, MH, LANES))], axis=-1)],
                axis=0).reshape(KV + LANES, MH * (vd + LANES)).astype(BF16)
            one_lane = jnp.where(jnp.arange(KV + LANES) == KV + half, 1.0, 0.0)
            lkp = _round_up(L, 512)
            lat_p = lat[:n_prompt_rows].reshape(B, LP, KV + LANES)
            if lkp >= LP:
                lat_p = jnp.pad(lat_p, ((0, 0), (0, lkp - LP), (0, 0)))
            else:
                lat_p = lat_p[:, :lkp]
            lat_p = (lat_p.reshape(B * lkp, KV + LANES) + one_lane).astype(BF16)
            lks = _round_up(past + ts, LANES)
            cache = jnp.concatenate([cache_ckv.astype(F32), _split_half_layout(cache_krope.astype(F32), half)],
                                    axis=-1)
            lat_s = jnp.concatenate([cache, lat[n_prompt_rows:].reshape(DB, ts, KV + LANES),
                                     jnp.zeros((DB, lks - past - ts, KV + LANES), F32)], axis=1)
            lat_s = (lat_s.reshape(DB * lks, KV + LANES) + one_lane).astype(BF16)
            w_kn_t = jnp.transpose(kv_w_uk, (1, 2, 0)).reshape(MH * nope, KV).astype(BF16)
            lat_s_t = lat_s.T
            k_slabs = (_matmul(w_k.T, lat_p.T, BF16, bm=1024), _matmul(lat_p, w_v, BF16, bm=1024),
                       _matmul(w_kn_t, lat_s_t, BF16, bm=1024), lat_s_t,
                       _matmul(lat_s[:, :KV], kv_w_uv.reshape(KV, MH * vd).astype(BF16), BF16, bm=1024),
                       lkp, lks)

    def unsplit(r):
        return jnp.concatenate([r[..., :half], r[..., LANES // 2:LANES // 2 + half]], axis=-1)

    lat_p = lat[:n_prompt_rows].reshape(B, LP, KV + LANES)[:, :L]
    lat_s = lat[n_prompt_rows:].reshape(DB, ts, KV + LANES)
    y_prompt = jnp.stack([x[b * LP + n_meta:b * LP + L] for b in range(B)])
    y_sample = x[n_prompt_rows:].reshape(DB, ts, D)
    s_all = jnp.stack(new_s)
    return (y_prompt, y_sample,
            jnp.stack(new_conv_p), s_all[:, :B],
            lat_p[..., :KV], unsplit(lat_p[..., KV:]),
            jnp.stack(new_conv_s), s_all[:, B:],
            lat_s[..., :KV], unsplit(lat_s[..., KV:]))
```

```python
import functools
import math

import jax
import jax.numpy as jnp
from jax import lax
from jax.experimental import pallas as pl
from jax.experimental.pallas import tpu as pltpu

CHUNK = 64
ROPE_THETA = 10000.0
NORM_EPS = 1e-6
NEG_INF = -1e30
LANES = 128
SUBLANES = 8
VMEM_LIMIT_BYTES = 56 * 1024 * 1024
_CHUNK_SHIFT = CHUNK.bit_length() - 1
assert 1 << _CHUNK_SHIFT == CHUNK

F32 = jnp.float32
BF16 = jnp.bfloat16
_HI = lax.Precision.HIGHEST


def _round_up(x, m):
    return -(-x // m) * m


def _pick_block(dim, target, align):
    best = None
    for d in range(align, min(dim, target) + 1, align):
        if dim % d == 0:
            best = d
    return best if best is not None else dim


def _dot(a, b):
    return jnp.dot(a, b, preferred_element_type=F32)


def _dot_nt(a, b):
    return lax.dot_general(a, b, (((1,), (1,)), ((), ())), preferred_element_type=F32)


def _dot_tn(a, b):
    return lax.dot_general(a, b, (((0,), (0,)), ((), ())), preferred_element_type=F32)


def _sigmoid(x):
    return 1.0 / (1.0 + jnp.exp(-x))


def _params(sem):
    return pltpu.CompilerParams(dimension_semantics=sem, vmem_limit_bytes=VMEM_LIMIT_BYTES)


def _mm_kernel(*refs, nk, epilogue):
    if epilogue == "rms":
        a_ref, b_ref, g_ref, o_ref = refs[:4]
        rest = refs[4:]
    else:
        a_ref, b_ref, o_ref = refs[:3]
        g_ref = None
        rest = refs[3:]

    def finish(acc):
        if epilogue == "relu2":
            r = jnp.maximum(acc, 0.0)
            acc = r * r
        elif epilogue == "rms":
            ms = jnp.mean(acc * acc, axis=-1, keepdims=True)
            acc = acc * lax.rsqrt(ms + NORM_EPS) * g_ref[...]
        o_ref[...] = acc.astype(o_ref.dtype)

    def product():
        return _dot(a_ref[...], b_ref[...].astype(BF16))

    if nk == 1:
        finish(product())
    else:
        acc_ref = rest[0]
        k = pl.program_id(2)

        @pl.when(k == 0)
        def _():
            acc_ref[...] = jnp.zeros_like(acc_ref)

        acc_ref[...] += product()

        @pl.when(k == nk - 1)
        def _():
            finish(acc_ref[...])


def _matmul(a, b, out_dtype, *, layer=None, n_use=None, epilogue="none", gain=None, bm=1312, bn=512, bk=4096):
    m, kdim = a.shape
    n = n_use if n_use is not None else b.shape[-1]
    bm = _pick_block(m, bm, 16)
    bn = n if epilogue == "rms" else _pick_block(n, bn, LANES)
    bk = _pick_block(kdim, bk, LANES)
    nk = kdim // bk
    a_mode = dict(pipeline_mode=pl.Buffered(1)) if nk == 1 else {}
    if b.ndim == 3:
        b_spec = pl.BlockSpec((None, bk, bn), lambda i, j, k: (layer, k, j))
    else:
        b_spec = pl.BlockSpec((bk, bn), lambda i, j, k: (k, j))
    in_specs = [pl.BlockSpec((bm, bk), lambda i, j, k: (i, k), **a_mode), b_spec]
    args = [a, b]
    if epilogue == "rms":
        in_specs.append(pl.BlockSpec((1, bn), lambda i, j, k: (0, j)))
        args.append(gain.reshape(1, n).astype(F32))
    scratch = [pltpu.VMEM((bm, bn), F32)] if nk > 1 else []
    return pl.pallas_call(
        functools.partial(_mm_kernel, nk=nk, epilogue=epilogue),
        out_shape=jax.ShapeDtypeStruct((m, n), out_dtype),
        grid=(m // bm, n // bn, nk),
        in_specs=in_specs,
        out_specs=pl.BlockSpec((bm, bn), lambda i, j, k: (i, j)),
        scratch_shapes=scratch,
        compiler_params=_params(("parallel", "parallel", "arbitrary")),
        name="matmul_" + epilogue,
    )(*args)


def _ln_kernel(x_ref, h_ref, g_ref, b_ref, of_ref, ob_ref, *, alpha):
    y = alpha * x_ref[...] + h_ref[...].astype(F32)
    mu = jnp.mean(y, axis=-1, keepdims=True)
    d = y - mu
    var = jnp.mean(d * d, axis=-1, keepdims=True)
    out = d * lax.rsqrt(var + NORM_EPS) * g_ref[...] + b_ref[...]
    of_ref[...] = out
    ob_ref[...] = out.astype(BF16)


def _deepnorm(x, h, gain, bias, alpha):
    m, d = x.shape
    br = _pick_block(m, 256, 16)
    row = pl.BlockSpec((br, d), lambda i: (i, 0))
    vec = pl.BlockSpec((1, d), lambda i: (0, 0))
    return pl.pallas_call(
        functools.partial(_ln_kernel, alpha=alpha),
        out_shape=(jax.ShapeDtypeStruct((m, d), F32), jax.ShapeDtypeStruct((m, d), BF16)),
        grid=(m // br,),
        in_specs=[row, row, vec, vec],
        out_specs=(row, row),
        compiler_params=_params(("parallel",)),
        name="deepnorm",
    )(x, h, gain.reshape(1, d), bias.reshape(1, d))


def _kv_post_kernel(lat_ref, g_ref, cos_ref, sin_ref, o_ref, *, kv):
    c = lat_ref[:, :kv]
    ms = jnp.mean(c * c, axis=-1, keepdims=True)
    o_ref[:, :kv] = c * lax.rsqrt(ms + NORM_EPS) * g_ref[...]
    r = lat_ref[:, kv:]
    o_ref[:, kv:] = r * cos_ref[...] + pltpu.roll(r, LANES // 2, axis=1) * sin_ref[...]


def _kv_post(lat, gain, cos_t, sin_t, kv):
    m, w = lat.shape
    br = _pick_block(m, 256, 8)
    return pl.pallas_call(
        functools.partial(_kv_post_kernel, kv=kv),
        out_shape=jax.ShapeDtypeStruct((m, w), F32),
        grid=(m // br,),
        in_specs=[pl.BlockSpec((br, w), lambda i: (i, 0)),
                  pl.BlockSpec((1, kv), lambda i: (0, 0)),
                  pl.BlockSpec((br, LANES), lambda i: (i, 0)),
                  pl.BlockSpec((br, LANES), lambda i: (i, 0))],
        out_specs=pl.BlockSpec((br, w), lambda i: (i, 0)),
        compiler_params=_params(("parallel",)),
        name="kv_post",
    )(lat, gain.reshape(1, kv), cos_t, sin_t)


def _rope_query(q, cos, sin, nope, scale):
    q = q.astype(F32)
    qr = q[:, nope:]
    qr = qr * cos + pltpu.roll(qr, LANES // 2, axis=1) * sin
    return (jnp.concatenate([q[:, :nope], qr], axis=1) * (scale * math.log2(math.e))).astype(BF16)


def _attn_causal_kernel(q_ref, cos_ref, sin_ref, kt_ref, v_ref, o_ref,
                        q_scr, s_scr, p_scr, m_scr, a_scr, acc_scr, *, bq, bk, rt, n_groups, nope, vd, scale,
                        n_meta, lk_pad):
    qi = pl.program_id(2)
    tiles = [slice(t * rt, (t + 1) * rt) for t in range(bq // rt)]
    lane_chunks = [slice(c * LANES, (c + 1) * LANES) for c in range(bk // LANES)]

    def visible_end(r):
        chunk_id = lax.shift_right_arithmetic(r - n_meta, _CHUNK_SHIFT) + 1
        return jnp.where(r < n_meta, n_meta, chunk_id * CHUNK + n_meta)

    n_full = visible_end(qi * bq) // bk
    nkb = (jnp.minimum(visible_end(qi * bq + bq - 1), lk_pad) + bk - 1) // bk

    for rows in tiles:
        q_scr[rows, :] = _rope_query(q_ref[rows, :], cos_ref[rows, :], sin_ref[rows, :], nope, scale)
    m_scr[...] = jnp.full(m_scr.shape, NEG_INF, F32)
    acc_scr[...] = jnp.zeros(acc_scr.shape, F32)

    gsz = bq // n_groups
    groups = [slice(g * gsz, (g + 1) * gsz) for g in range(n_groups)]
    tiles_per_group = gsz // rt

    def step(j, carry, masked):
        start = pl.multiple_of(j * bk, bk)
        kt_blk = kt_ref[:, pl.ds(start, bk)]
        v_blk = v_ref[pl.ds(start, bk), :]

        def scores(g):
            s_scr[groups[g], :] = _dot(q_scr[groups[g], :], kt_blk)

        def softmax(g):
            for t in range(g * tiles_per_group, (g + 1) * tiles_per_group):
                rows = tiles[t]
                s = s_scr[rows, :]
                if masked:
                    kpos = start + lax.broadcasted_iota(jnp.int32, (1, bk), 1)
                    vis = visible_end(qi * bq + t * rt + lax.broadcasted_iota(jnp.int32, (rt, 1), 0))
                    s = jnp.where(kpos < vis, s, NEG_INF)
                m_old = m_scr[rows, :]
                m_new = jnp.maximum(m_old, jnp.max(s, axis=-1, keepdims=True))
                for lc in lane_chunks:
                    p_scr[rows, lc] = jnp.exp2(s[:, lc] - m_new).astype(BF16)
                a_scr[rows, :] = jnp.exp2(m_old - m_new)
                m_scr[rows, :] = m_new

        def values(g):
            rows = groups[g]
            a = a_scr[rows, :]
            acc_scr[rows, :] = (jnp.concatenate([a] * (acc_scr.shape[1] // LANES), axis=1) * acc_scr[rows, :]
                                + _dot(p_scr[rows, :], v_blk))

        scores(0)
        for g in range(n_groups):
            if g + 1 < n_groups:
                scores(g + 1)
            softmax(g)
            values(g)
        return carry

    lax.fori_loop(0, n_full, functools.partial(step, masked=False), 0)
    lax.fori_loop(n_full, nkb, functools.partial(step, masked=True), 0)
    for rows in tiles:
        o_ref[rows, :] = (acc_scr[rows, :vd] / acc_scr[rows, vd:vd + 1]).astype(o_ref.dtype)


def _attn_full_kernel(q_ref, cos_ref, sin_ref, knt_ref, krt_ref, v_ref, o_ref, *, hb, nope, vd, scale, n_keys):
    dqk = nope + LANES
    lq = q_ref.shape[0]
    heads = range(hb)
    cos, sin = cos_ref[...], sin_ref[...]
    kpos = lax.broadcasted_iota(jnp.int32, (1, knt_ref.shape[1]), 1)
    qs = [_rope_query(q_ref[:, h * dqk:(h + 1) * dqk], cos, sin, nope, scale) for h in heads]
    s_rope = _dot(jnp.concatenate([q[:, nope:] for q in qs], axis=0), krt_ref[...])
    ss = [jnp.where(kpos < n_keys,
                    _dot(qs[h][:, :nope], knt_ref[h * nope:(h + 1) * nope, :]) + s_rope[h * lq:(h + 1) * lq],
                    NEG_INF) for h in heads]
    es = [jnp.exp2(s - jnp.max(s, axis=-1, keepdims=True)) for s in ss]
    ls = [jnp.sum(e, axis=-1, keepdims=True) for e in es]
    pv = [_dot(es[h].astype(BF16), v_ref[:, h * vd:(h + 1) * vd]) for h in heads]
    for h in heads:
        o_ref[:, h * vd:(h + 1) * vd] = (pv[h] / ls[h]).astype(o_ref.dtype)


def _attention_causal(q, cos_t, sin_t, kt, vf, *, nb, lq, lk_pad, heads, nope, vd, bq, bk, scale, n_meta):
    dqk = nope + LANES
    nq = lq // bq
    rt = 32 if bq % 32 == 0 else bq
    n_groups = 4 if bq % (4 * rt) == 0 else 1
    return pl.pallas_call(
        functools.partial(_attn_causal_kernel, bq=bq, bk=bk, rt=rt, n_groups=n_groups, nope=nope, vd=vd,
                          scale=scale,
                          n_meta=n_meta, lk_pad=lk_pad),
        out_shape=jax.ShapeDtypeStruct((q.shape[0], heads * vd), BF16),
        grid=(nb, heads, nq),
        in_specs=[pl.BlockSpec((bq, dqk), lambda b, h, i: (b * nq + i, h)),
                  pl.BlockSpec((bq, LANES), lambda b, h, i: (b * nq + i, 0)),
                  pl.BlockSpec((bq, LANES), lambda b, h, i: (b * nq + i, 0)),
                  pl.BlockSpec((dqk, lk_pad), lambda b, h, i: (h, b)),
                  pl.BlockSpec((lk_pad, vd + LANES), lambda b, h, i: (b, h))],
        out_specs=pl.BlockSpec((bq, vd), lambda b, h, i: (b * nq + i, h)),
        scratch_shapes=[pltpu.VMEM((bq, dqk), BF16), pltpu.VMEM((bq, bk), F32), pltpu.VMEM((bq, bk), BF16),
                        pltpu.VMEM((bq, LANES), F32), pltpu.VMEM((bq, LANES), F32),
                        pltpu.VMEM((bq, vd + LANES), F32)],
        compiler_params=_params(("parallel", "parallel", "arbitrary")),
        name="mla_attention_causal",
    )(q, cos_t, sin_t, kt, vf)


def _attention_full(q, cos_t, sin_t, knt, lat_t, vf, o_all, *, nb, lq, lk_pad, row0, heads, hb, nope, vd, kv,
                    scale, n_keys):
    dqk = nope + LANES
    blk0 = row0 // lq

    def body(q_ref, cos_ref, sin_ref, knt_ref, krt_ref, v_ref, o_all_ref, o_ref):
        _attn_full_kernel(q_ref, cos_ref, sin_ref, knt_ref, krt_ref, v_ref, o_ref,
                          hb=hb, nope=nope, vd=vd, scale=scale, n_keys=n_keys)

    return pl.pallas_call(
        body,
        out_shape=jax.ShapeDtypeStruct(o_all.shape, o_all.dtype),
        grid=(nb, heads // hb),
        in_specs=[pl.BlockSpec((lq, hb * dqk), lambda b, g: (blk0 + b, g)),
                  pl.BlockSpec((lq, LANES), lambda b, g: (blk0 + b, 0)),
                  pl.BlockSpec((lq, LANES), lambda b, g: (blk0 + b, 0)),
                  pl.BlockSpec((hb * nope, lk_pad), lambda b, g: (g, b)),
                  pl.BlockSpec((LANES, lk_pad), lambda b, g: (kv // LANES, b)),
                  pl.BlockSpec((lk_pad, hb * vd), lambda b, g: (b, g)),
                  pl.BlockSpec(memory_space=pl.ANY)],
        out_specs=pl.BlockSpec((lq, hb * vd), lambda b, g: (blk0 + b, g)),
        input_output_aliases={6: 0},
        compiler_params=_params(("parallel", "parallel")),
        name="mla_attention_full",
    )(q, cos_t, sin_t, knt, lat_t, vf, o_all)


def _item_info(i, n_p, cps, cs, l_valid):
    is_p = i < n_p
    j = i - n_p
    seq = jnp.where(is_p, i // cps, n_p // cps + j // cs)
    c = jnp.where(is_p, i % cps, j % cs)
    nc = jnp.where(is_p, cps, cs)
    nvalid = jnp.where(is_p, jnp.clip(l_valid - c * CHUNK, 0, CHUNK), CHUNK)
    return seq, c, nc, nvalid


def _tri_inverse_many(mats):
    n = mats[0].shape[0]
    r = lax.broadcasted_iota(jnp.int32, (n, n), 0)
    c = lax.broadcasted_iota(jnp.int32, (n, n), 1)
    eye = jnp.where(r == c, 1.0, 0.0)
    zero = jnp.zeros((n, n), F32)

    def split(x):
        hi = x.astype(BF16).astype(F32)
        return hi, x - hi

    def lhs_of(x):
        hi, lo = split(x)
        return jnp.concatenate([hi, lo, hi, zero], axis=1).astype(BF16)

    def rhs_of(x):
        hi, lo = split(x)
        return jnp.concatenate([hi, hi, lo, zero], axis=0).astype(BF16)

    ps = [eye - a for a in mats]
    aks = list(mats)
    rhs = [rhs_of(a) for a in mats]
    power = 2
    while power < n:
        aks = [_dot(lhs_of(ak), rk) for ak, rk in zip(aks, rhs)]
        rhs = [rhs_of(ak) for ak in aks]
        ps = [p + _dot(lhs_of(p), rk) for p, rk in zip(ps, rhs)]
        power *= 2
    return ps


def _gdn_kernel(qc_ref, kc_ref, vc_ref, qp_ref, kp_ref, vp_ref, q0_ref, k0_ref, v0_ref,
                wq_ref, wk_ref, wv_ref, ab_ref, alog_ref, dtb_ref, z_ref, gain_ref, s0_ref,
                o_ref, snew_ref, srun_ref,
                bufq, bufk, bufv, val_s, kcd_s, qdec_s, kdec_s, qk_s, eg_s, s_ref,
                *, hg, dk, n_items, n_p, cps, cs, l_valid, width):
    C = CHUNK
    t = pl.program_id(1)
    i = jnp.minimum(t, n_items - 1)
    j = jnp.maximum(t - 1, 0)
    _, c, _, nvalid = _item_info(i, n_p, cps, cs, l_valid)
    _, cj, ncj, _ = _item_info(j, n_p, cps, cs, l_valid)
    first = c == 0
    heads = range(hg)
    sls = [slice(h * dk, (h + 1) * dk) for h in heads]

    @pl.when(t == 0)
    def _():
        for ref in (val_s, kcd_s, qdec_s, kdec_s, qk_s, eg_s):
            ref[...] = jnp.zeros(ref.shape, ref.dtype)

    n_new = n_p // cps
    seq_j = _item_info(j, n_p, cps, cs, l_valid)[0]

    @pl.when(cj == 0)
    def _():
        s_ref[...] = jnp.where(seq_j < n_new, 0.0, s0_ref[0])

    sb = [s_ref[h].astype(BF16) for h in heads]
    ks_s = [_dot(kcd_s[:, sls[h]], sb[h]) for h in heads]
    qs_s = [_dot(qdec_s[:, sls[h]], sb[h]) for h in heads]

    def conv_silu(cur_ref, prev_ref, init_ref, w_ref, buf):
        buf[0:SUBLANES, :] = jnp.where(first, init_ref[0], prev_ref[...])
        buf[SUBLANES:SUBLANES + C, :] = cur_ref[...]
        base = SUBLANES - (width - 1)
        y = buf[base:base + C, :] * w_ref[0:1, :]
        for t in range(1, width):
            y = y + buf[base + t:base + t + C, :] * w_ref[t:t + 1, :]
        return y * _sigmoid(y)

    qs = conv_silu(qc_ref, qp_ref, q0_ref, wq_ref, bufq)
    ks = conv_silu(kc_ref, kp_ref, k0_ref, wk_ref, bufk)
    vs = conv_silu(vc_ref, vp_ref, v0_ref, wv_ref, bufv)

    ub = [(val_s[:, sls[h]] - ks_s[h]).astype(BF16) for h in heads]
    ds_s = [_dot_tn(kdec_s[:, sls[h]], ub[h]) for h in heads]
    os_s = [qs_s[h] + _dot(qk_s[:, h * C:(h + 1) * C].astype(BF16), ub[h]) for h in heads]

    ab = ab_ref[...]
    rows = lax.broadcasted_iota(jnp.int32, (C, LANES), 0)
    valid = rows < nvalid
    xs = ab + dtb_ref[0]
    softplus = jnp.maximum(xs, 0.0) + jnp.log(1.0 + jnp.exp(-jnp.abs(xs)))
    g = jnp.where(valid, -jnp.exp(alog_ref[0]) * softplus, 0.0)
    beta = jnp.where(valid, _sigmoid(ab), 0.0)
    r = lax.broadcasted_iota(jnp.int32, (C, C), 0)
    cc = lax.broadcasted_iota(jnp.int32, (C, C), 1)
    incl = r >= cc
    strict = r > cc
    gcum = jnp.dot(jnp.where(incl, 1.0, 0.0), g, precision=_HI, preferred_element_type=F32)
    gcum_t = gcum.T

    qn = [qs[:, sl] * lax.rsqrt(jnp.sum(qs[:, sl] * qs[:, sl], axis=-1, keepdims=True) + NORM_EPS)
          * (dk ** -0.5) for sl in sls]
    kn = [ks[:, sl] * lax.rsqrt(jnp.sum(ks[:, sl] * ks[:, sl], axis=-1, keepdims=True) + NORM_EPS)
          for sl in sls]
    bcol = [beta[:, hg + h:hg + h + 1] for h in heads]
    gcol = [gcum[:, h:h + 1] for h in heads]
    glast = [gcum[C - 1:C, h:h + 1] for h in heads]
    eg = [jnp.exp(gcol[h]) for h in heads]
    decay = [jnp.where(incl, jnp.exp(jnp.where(incl, gcol[h] - gcum_t[h:h + 1, :], 0.0)), 0.0) for h in heads]
    kb = [kn[h] * bcol[h] for h in heads]
    knb = [kn[h].astype(BF16) for h in heads]
    a_mat = [jnp.where(strict, _dot_nt(kb[h].astype(BF16), knb[h]) * decay[h], 0.0) for h in heads]

    gain = gain_ref[...]
    for h in heads:
        s_ref[h] = s_ref[h] * eg_s[h:h + 1, :] + ds_s[h]
    for h in heads:
        o = os_s[h]
        z = z_ref[:, sls[h]]
        ms = jnp.mean(o * o, axis=-1, keepdims=True)
        o_ref[:, sls[h]] = (o * lax.rsqrt(ms + NORM_EPS) * gain * (z * _sigmoid(z))).astype(o_ref.dtype)

    t_mat = _tri_inverse_many(a_mat)
    rhs = [jnp.concatenate([vs[:, sls[h]] * bcol[h], kb[h] * eg[h]], axis=1).astype(BF16) for h in heads]
    tv = [_dot(t_mat[h].astype(BF16), rhs[h]) for h in heads]
    qk = [_dot_nt(qn[h].astype(BF16), knb[h]) * decay[h] for h in heads]
    for h in heads:
        val_s[:, sls[h]] = tv[h][:, :dk]
        kcd_s[:, sls[h]] = tv[h][:, dk:].astype(BF16)
        qdec_s[:, sls[h]] = (qn[h] * eg[h]).astype(BF16)
        kdec_s[:, sls[h]] = (kn[h] * jnp.exp(glast[h] - gcol[h])).astype(BF16)
        eg_s[h:h + 1, :] = jnp.broadcast_to(jnp.exp(glast[h]), (1, LANES))
    qk_s[...] = jnp.concatenate(qk, axis=1)

    done = (cj == ncj - 1) & (t > 0)

    @pl.when(done & (seq_j < n_new))
    def _():
        snew_ref[0] = s_ref[...]

    @pl.when(done & (seq_j >= n_new))
    def _():
        srun_ref[0] = s_ref[...]


def _gdn_mixer(proj, gates, conv0, s0, conv_w, a_log, dt_bias, o_gain, *, heads, dk, hg, n_p, cps, cs, l_valid):
    m = proj.shape[0]
    C = CHUNK
    w = heads * dk
    n_hg = heads // hg
    gw = hg * dk
    n_items = m // C
    n_new, n_run = n_p // cps, s0.shape[0]
    width = conv_w.shape[0]
    info = dict(n_p=n_p, cps=cps, cs=cs, l_valid=l_valid)

    def seq_of(i):
        return _item_info(i, n_p, cps, cs, l_valid)[0]

    prep = lambda t: jnp.minimum(t, n_items - 1)
    scan = lambda t: jnp.maximum(t - 1, 0)
    cur = lambda off: pl.BlockSpec((C, gw), lambda g, t: (prep(t), off * n_hg + g))
    prev = lambda off: pl.BlockSpec(
        (SUBLANES, gw), lambda g, t: (jnp.maximum(prep(t) * (C // SUBLANES) - 1, 0), off * n_hg + g))
    init = lambda off: pl.BlockSpec((1, SUBLANES, gw), lambda g, t: (seq_of(prep(t)), 0, off * n_hg + g))
    wspec = lambda off: pl.BlockSpec((width, gw), lambda g, t: (0, off * n_hg + g))
    gate_vec = pl.BlockSpec((1, 1, LANES), lambda g, t: (g, 0, 0))
    state_new = pl.BlockSpec((1, hg, dk, dk), lambda g, t: (jnp.minimum(seq_of(scan(t)), n_new - 1), g, 0, 0))
    state_run = pl.BlockSpec((1, hg, dk, dk), lambda g, t: (jnp.maximum(seq_of(scan(t)) - n_new, 0), g, 0, 0))

    alog_g = jnp.pad(a_log.reshape(n_hg, 1, hg).astype(F32), ((0, 0), (0, 0), (0, LANES - hg)))
    dtb_g = jnp.pad(dt_bias.reshape(n_hg, 1, hg).astype(F32), ((0, 0), (0, 0), (0, LANES - hg)))

    return pl.pallas_call(
        functools.partial(_gdn_kernel, hg=hg, dk=dk, width=width, n_items=n_items, **info),
        out_shape=(jax.ShapeDtypeStruct((m, w), BF16),
                   jax.ShapeDtypeStruct((n_new, heads, dk, dk), F32),
                   jax.ShapeDtypeStruct((n_run, heads, dk, dk), F32)),
        grid=(n_hg, n_items + 1),
        in_specs=[cur(0), cur(1), cur(2), prev(0), prev(1), prev(2), init(0), init(1), init(2),
                  wspec(0), wspec(1), wspec(2),
                  pl.BlockSpec((C, LANES), lambda g, t: (prep(t), g)),
                  gate_vec, gate_vec,
                  pl.BlockSpec((C, gw), lambda g, t: (scan(t), 3 * n_hg + g)),
                  pl.BlockSpec((1, dk), lambda g, t: (0, 0)),
                  state_run],
        out_specs=(pl.BlockSpec((C, gw), lambda g, t: (scan(t), g)), state_new, state_run),
        scratch_shapes=[pltpu.VMEM((SUBLANES + C, gw), F32)] * 3
        + [pltpu.VMEM((C, gw), F32), pltpu.VMEM((C, gw), BF16), pltpu.VMEM((C, gw), BF16),
           pltpu.VMEM((C, gw), BF16), pltpu.VMEM((C, hg * C), F32), pltpu.VMEM((hg, LANES), F32),
           pltpu.VMEM((hg, dk, dk), F32)],
        compiler_params=_params(("parallel", "arbitrary")),
        name="gdn_mixer",
    )(proj, proj, proj, proj, proj, proj, conv0, conv0, conv0, conv_w, conv_w, conv_w,
      gates, alog_g, dtb_g, proj, o_gain.reshape(1, dk).astype(F32), s0)


def _split_half_layout(x, half):
    pad = [(0, 0)] * (x.ndim - 1) + [(0, LANES // 2 - half)]
    return jnp.concatenate([jnp.pad(x[..., :half], pad), jnp.pad(x[..., half:], pad)], axis=-1)


def _rope_tables(pos, half):
    inv_freq = 1.0 / (ROPE_THETA ** (jnp.arange(half, dtype=F32) / half))
    ang = pos[:, None] * inv_freq[None, :]
    cos, sin = jnp.cos(ang), jnp.sin(ang)
    cos_t = _split_half_layout(jnp.concatenate([cos, cos], -1), half)
    sin_t = _split_half_layout(jnp.concatenate([-sin, sin], -1), half)
    return cos_t, sin_t


def kernel(x_prompt, x_sample, state_conv, state_delta, cache_ckv, cache_krope, meta_tokens,
           a_w_in, a_conv_w, a_a_log, a_dt_bias, a_o_gain, a_w_o,
           b_w_dq, b_q_gain, b_w_uq, b_w_o,
           kv_w_dkv, kv_gain, kv_w_uk, kv_w_uv,
           mlp_w_up, mlp_w_down, ln_gain, ln_bias):
    B, seq, D = x_prompt.shape
    DB, ts, _ = x_sample.shape
    n_meta = meta_tokens.shape[0]
    depth = ln_gain.shape[0]
    n_a = a_w_in.shape[0]
    H = a_a_log.shape[1]
    dk = a_o_gain.shape[1]
    W = H * dk
    width = a_conv_w.shape[1]
    past = cache_ckv.shape[1]
    KV, MH, nope = kv_w_uk.shape
    vd = kv_w_uv.shape[2]
    rope = cache_krope.shape[2]
    half = rope // 2
    alpha = (2 * depth) ** 0.25
    scale = (nope + rope) ** -0.5
    assert ts % CHUNK == 0 and ts >= width - 1 and half < LANES // 2
    assert nope % LANES == 0 and vd % LANES == 0 and KV % LANES == 0 and dk % LANES == 0

    L = n_meta + seq
    LP = _round_up(L, LANES)
    n_prompt_rows = B * LP
    M = n_prompt_rows + DB * ts
    cps, cs = LP // CHUNK, ts // CHUNK
    n_p = B * cps
    hg = min(8, H)
    n_hg = H // hg

    meta = jnp.broadcast_to(meta_tokens.astype(F32)[None], (B, n_meta, D))
    xp = jnp.concatenate([meta, x_prompt, jnp.zeros((B, LP - L, D), F32)], axis=1)
    x = jnp.concatenate([xp.reshape(n_prompt_rows, D), x_sample.reshape(DB * ts, D)], axis=0)
    xb = x.astype(BF16)

    pos = jnp.concatenate([jnp.tile(jnp.arange(LP, dtype=F32), B),
                           jnp.tile(past + jnp.arange(ts, dtype=F32), DB)])
    cos_t, sin_t = _rope_tables(pos, half)

    conv0 = jnp.concatenate([jnp.zeros((n_a, B, width - 1, 3 * W), F32), state_conv.astype(F32)], axis=1)
    conv0 = jnp.pad(conv0, ((0, 0), (0, 0), (SUBLANES - (width - 1), 0), (0, 0)))
    s0 = state_delta.astype(F32)

    new_conv_p, new_conv_s, new_s = [], [], []
    k_slabs = None
    lat = None
    for layer in range(depth):
        if layer < n_a:
            w_ab = a_w_in[layer, :, 4 * W:]
            gate_w = jnp.concatenate(
                [w_ab[:, :H].reshape(D, n_hg, hg), w_ab[:, H:].reshape(D, n_hg, hg),
                 jnp.zeros((D, n_hg, LANES - 2 * hg), w_ab.dtype)], axis=-1).reshape(D, n_hg * LANES)
            proj = _matmul(xb, a_w_in, F32, layer=layer, n_use=4 * W)
            gates = _matmul(xb, gate_w.astype(BF16), F32)
            o, s_new, s_run = _gdn_mixer(proj, gates, conv0[layer], s0[layer], a_conv_w[layer].astype(F32),
                                         a_a_log[layer], a_dt_bias[layer], a_o_gain[layer],
                                         heads=H, dk=dk, hg=hg, n_p=n_p, cps=cps, cs=cs, l_valid=L)
            h = _matmul(o, a_w_o, BF16, layer=layer)
            new_conv_p.append(jnp.stack(
                [proj[b * LP + L - (width - 1):b * LP + L, :3 * W] for b in range(B)]))
            new_conv_s.append(jnp.stack(
                [proj[n_prompt_rows + ts - (width - 1) + t::ts, :3 * W] for t in range(width - 1)], axis=1))
            new_s.append((s_new, s_run))
        else:
            j = layer - n_a
            ql = _matmul(xb, b_w_dq[j].astype(BF16), BF16, epilogue="rms", gain=b_q_gain[j])
            w_uq = b_w_uq[j].reshape(-1, MH, nope + rope)
            w_uq = jnp.concatenate([w_uq[..., :nope], _split_half_layout(w_uq[..., nope:], half)], axis=-1)
            q = _matmul(ql, w_uq.reshape(-1, MH * (nope + LANES)).astype(BF16), BF16)
            kp, vp, ks, ls_t, vs, lkp, lks = k_slabs
            o = _attention_causal(q, cos_t, sin_t, kp, vp, nb=B, lq=LP, lk_pad=lkp, heads=MH, nope=nope,
                                  vd=vd, bq=_pick_block(LP, 1408, LANES), bk=512, scale=scale, n_meta=n_meta)
            o = _attention_full(q, cos_t, sin_t, ks, ls_t, vs, o, nb=DB, lq=ts, lk_pad=lks, row0=n_prompt_rows,
                                heads=MH, hb=min(4, MH), nope=nope, vd=vd, kv=KV, scale=scale,
                                n_keys=past + ts)
            h = _matmul(o, b_w_o, BF16, layer=j)

        x, xb = _deepnorm(x, h, ln_gain[layer, 0], ln_bias[layer, 0], alpha)
        hid = _matmul(xb, mlp_w_up, BF16, layer=layer, epilogue="relu2")
        h = _matmul(hid, mlp_w_down, BF16, layer=layer, bn=1024, bk=2048)
        x, xb = _deepnorm(x, h, ln_gain[layer, 1], ln_bias[layer, 1], alpha)

        if layer == n_a - 1:
            w_dkv = jnp.concatenate([kv_w_dkv[:, :KV], _split_half_layout(kv_w_dkv[:, KV:], half)], axis=1)
            lat = _kv_post(_matmul(xb, w_dkv.astype(BF16), F32), kv_gain.astype(F32), cos_t, sin_t, KV)
            lane_ids = jnp.arange(LANES)
            eye = jnp.where((lane_ids[:, None] == lane_ids[None, :]) & (lane_ids[:, None] != half), 1.0, 0.0)
            ones_col = jnp.where((lane_ids[:, None] == half) & (lane_ids[None, :] == 0), 1.0, 0.0)
            w_k = jnp.concatenate(
                [jnp.concatenate([kv_w_uk.astype(F32), jnp.zeros((KV, MH, LANES), F32)], axis=-1),
                 jnp.concatenate([jnp.zeros((LANES, MH, nope), F32),
                                  jnp.broadcast_to(eye[:, None, :], (LANES, MH, LANES))], axis=-1)],
                axis=0).reshape(KV + LANES, MH * (nope + LANES)).astype(BF16)
            w_v = jnp.concatenate(
                [jnp.concatenate([kv_w_uv.astype(F32), jnp.zeros((KV, MH, LANES), F32)], axis=-1),
                 jnp.concatenate([jnp.zeros((LANES, MH, vd), F32),
                                  jnp.broadcast_to(ones_col[:, None, :], (LANES, MH, LANES))], axis=-1)],
                axis=0).reshape(KV + LANES, MH * (vd + LANES)).astype(BF16)
            one_lane = jnp.where(jnp.arange(KV + LANES) == KV + half, 1.0, 0.0)
            lkp = _round_up(L, 512)
            lat_p = lat[:n_prompt_rows].reshape(B, LP, KV + LANES)
            if lkp >= LP:
                lat_p = jnp.pad(lat_p, ((0, 0), (0, lkp - LP), (0, 0)))
            else:
                lat_p = lat_p[:, :lkp]
            lat_p = (lat_p.reshape(B * lkp, KV + LANES) + one_lane).astype(BF16)
            lks = _round_up(past + ts, LANES)
            cache = jnp.concatenate([cache_ckv.astype(F32), _split_half_layout(cache_krope.astype(F32), half)],
                                    axis=-1)
            lat_s = jnp.concatenate([cache, lat[n_prompt_rows:].reshape(DB, ts, KV + LANES),
                                     jnp.zeros((DB, lks - past - ts, KV + LANES), F32)], axis=1)
            lat_s = (lat_s.reshape(DB * lks, KV + LANES) + one_lane).astype(BF16)
            w_kn_t = jnp.transpose(kv_w_uk, (1, 2, 0)).reshape(MH * nope, KV).astype(BF16)
            lat_s_t = lat_s.T
            k_slabs = (_matmul(w_k.T, lat_p.T, BF16, bm=1024), _matmul(lat_p, w_v, BF16, bm=1024),
                       _matmul(w_kn_t, lat_s_t, BF16, bm=1024), lat_s_t,
                       _matmul(lat_s[:, :KV], kv_w_uv.reshape(KV, MH * vd).astype(BF16), BF16, bm=1024),
                       lkp, lks)

    def unsplit(r):
        return jnp.concatenate([r[..., :half], r[..., LANES // 2:LANES // 2 + half]], axis=-1)

    lat_p = lat[:n_prompt_rows].reshape(B, LP, KV + LANES)[:, :L]
    lat_s = lat[n_prompt_rows:].reshape(DB, ts, KV + LANES)
    y_prompt = jnp.stack([x[b * LP + n_meta:b * LP + L] for b in range(B)])
    y_sample = x[n_prompt_rows:].reshape(DB, ts, D)
    return (y_prompt, y_sample,
            jnp.stack(new_conv_p), jnp.stack([s[0] for s in new_s]),
            lat_p[..., :KV], unsplit(lat_p[..., KV:]),
            jnp.stack(new_conv_s), jnp.stack([s[1] for s in new_s]),
            lat_s[..., :KV], unsplit(lat_s[..., KV:]))
```

```python
import functools
import math

import jax
import jax.numpy as jnp
from jax import lax
from jax.experimental import pallas as pl
from jax.experimental.pallas import tpu as pltpu

CHUNK = 64
ROPE_THETA = 10000.0
NORM_EPS = 1e-6
NEG_INF = -1e30
LANES = 128
SUBLANES = 8
VMEM_LIMIT_BYTES = 56 * 1024 * 1024
_CHUNK_SHIFT = CHUNK.bit_length() - 1
assert 1 << _CHUNK_SHIFT == CHUNK

F32 = jnp.float32
BF16 = jnp.bfloat16
_HI = lax.Precision.HIGHEST


def _round_up(x, m):
    return -(-x // m) * m


def _pick_block(dim, target, align):
    best = None
    for d in range(align, min(dim, target) + 1, align):
        if dim % d == 0:
            best = d
    return best if best is not None else dim


def _dot(a, b):
    return jnp.dot(a, b, preferred_element_type=F32)


def _dot_nt(a, b):
    return lax.dot_general(a, b, (((1,), (1,)), ((), ())), preferred_element_type=F32)


def _dot_tn(a, b):
    return lax.dot_general(a, b, (((0,), (0,)), ((), ())), preferred_element_type=F32)


def _sigmoid(x):
    return 1.0 / (1.0 + jnp.exp(-x))


def _params(sem):
    return pltpu.CompilerParams(dimension_semantics=sem, vmem_limit_bytes=VMEM_LIMIT_BYTES)


def _mm_kernel(*refs, nk, epilogue):
    if epilogue == "rms":
        a_ref, b_ref, g_ref, o_ref = refs[:4]
        rest = refs[4:]
    else:
        a_ref, b_ref, o_ref = refs[:3]
        g_ref = None
        rest = refs[3:]

    def finish(acc):
        if epilogue == "relu2":
            r = jnp.maximum(acc, 0.0)
            acc = r * r
        elif epilogue == "rms":
            ms = jnp.mean(acc * acc, axis=-1, keepdims=True)
            acc = acc * lax.rsqrt(ms + NORM_EPS) * g_ref[...]
        o_ref[...] = acc.astype(o_ref.dtype)

    def product():
        return _dot(a_ref[...], b_ref[...].astype(BF16))

    if nk == 1:
        finish(product())
    else:
        acc_ref = rest[0]
        k = pl.program_id(2)

        @pl.when(k == 0)
        def _():
            acc_ref[...] = jnp.zeros_like(acc_ref)

        acc_ref[...] += product()

        @pl.when(k == nk - 1)
        def _():
            finish(acc_ref[...])


def _matmul(a, b, out_dtype, *, layer=None, n_use=None, epilogue="none", gain=None, bm=1312, bn=512, bk=4096):
    m, kdim = a.shape
    n = n_use if n_use is not None else b.shape[-1]
    bm = _pick_block(m, bm, 16)
    bn = n if epilogue == "rms" else _pick_block(n, bn, LANES)
    bk = _pick_block(kdim, bk, LANES)
    nk = kdim // bk
    a_mode = dict(pipeline_mode=pl.Buffered(1)) if nk == 1 and n // bn >= 4 else {}
    if b.ndim == 3:
        b_spec = pl.BlockSpec((None, bk, bn), lambda i, j, k: (layer, k, j))
    else:
        b_spec = pl.BlockSpec((bk, bn), lambda i, j, k: (k, j))
    in_specs = [pl.BlockSpec((bm, bk), lambda i, j, k: (i, k), **a_mode), b_spec]
    args = [a, b]
    if epilogue == "rms":
        in_specs.append(pl.BlockSpec((1, bn), lambda i, j, k: (0, j)))
        args.append(gain.reshape(1, n).astype(F32))
    scratch = [pltpu.VMEM((bm, bn), F32)] if nk > 1 else []
    return pl.pallas_call(
        functools.partial(_mm_kernel, nk=nk, epilogue=epilogue),
        out_shape=jax.ShapeDtypeStruct((m, n), out_dtype),
        grid=(m // bm, n // bn, nk),
        in_specs=in_specs,
        out_specs=pl.BlockSpec((bm, bn), lambda i, j, k: (i, j)),
        scratch_shapes=scratch,
        compiler_params=_params(("parallel", "parallel", "arbitrary")),
        name="matmul_" + epilogue,
    )(*args)


def _ln_kernel(x_ref, h_ref, g_ref, b_ref, of_ref, ob_ref, *, alpha):
    y = alpha * x_ref[...] + h_ref[...].astype(F32)
    mu = jnp.mean(y, axis=-1, keepdims=True)
    d = y - mu
    var = jnp.mean(d * d, axis=-1, keepdims=True)
    out = d * lax.rsqrt(var + NORM_EPS) * g_ref[...] + b_ref[...]
    of_ref[...] = out
    ob_ref[...] = out.astype(BF16)


def _deepnorm(x, h, gain, bias, alpha):
    m, d = x.shape
    br = _pick_block(m, 256, 16)
    row = pl.BlockSpec((br, d), lambda i: (i, 0))
    vec = pl.BlockSpec((1, d), lambda i: (0, 0))
    return pl.pallas_call(
        functools.partial(_ln_kernel, alpha=alpha),
        out_shape=(jax.ShapeDtypeStruct((m, d), F32), jax.ShapeDtypeStruct((m, d), BF16)),
        grid=(m // br,),
        in_specs=[row, row, vec, vec],
        out_specs=(row, row),
        compiler_params=_params(("parallel",)),
        name="deepnorm",
    )(x, h, gain.reshape(1, d), bias.reshape(1, d))


def _kv_post_kernel(lat_ref, g_ref, cos_ref, sin_ref, o_ref, *, kv):
    c = lat_ref[:, :kv]
    ms = jnp.mean(c * c, axis=-1, keepdims=True)
    o_ref[:, :kv] = c * lax.rsqrt(ms + NORM_EPS) * g_ref[...]
    r = lat_ref[:, kv:]
    o_ref[:, kv:] = r * cos_ref[...] + pltpu.roll(r, LANES // 2, axis=1) * sin_ref[...]


def _kv_post(lat, gain, cos_t, sin_t, kv):
    m, w = lat.shape
    br = _pick_block(m, 256, 8)
    return pl.pallas_call(
        functools.partial(_kv_post_kernel, kv=kv),
        out_shape=jax.ShapeDtypeStruct((m, w), F32),
        grid=(m // br,),
        in_specs=[pl.BlockSpec((br, w), lambda i: (i, 0)),
                  pl.BlockSpec((1, kv), lambda i: (0, 0)),
                  pl.BlockSpec((br, LANES), lambda i: (i, 0)),
                  pl.BlockSpec((br, LANES), lambda i: (i, 0))],
        out_specs=pl.BlockSpec((br, w), lambda i: (i, 0)),
        compiler_params=_params(("parallel",)),
        name="kv_post",
    )(lat, gain.reshape(1, kv), cos_t, sin_t)


def _rope_query(q, cos, sin, nope, scale):
    q = q.astype(F32)
    qr = q[:, nope:]
    qr = qr * cos + pltpu.roll(qr, LANES // 2, axis=1) * sin
    return (jnp.concatenate([q[:, :nope], qr], axis=1) * (scale * math.log2(math.e))).astype(BF16)


def _attn_causal_kernel(q_ref, cos_ref, sin_ref, kt_ref, v_ref, o_ref,
                        q_scr, s_scr, p_scr, m_scr, a_scr, acc_scr, *, bq, bk, rt, n_groups, nope, vd, scale,
                        n_meta, lk_pad):
    qi = pl.program_id(2)
    tiles = [slice(t * rt, (t + 1) * rt) for t in range(bq // rt)]
    lane_chunks = [slice(c * LANES, (c + 1) * LANES) for c in range(bk // LANES)]

    def visible_end(r):
        chunk_id = lax.shift_right_arithmetic(r - n_meta, _CHUNK_SHIFT) + 1
        return jnp.where(r < n_meta, n_meta, chunk_id * CHUNK + n_meta)

    n_full = visible_end(qi * bq) // bk
    nkb = (jnp.minimum(visible_end(qi * bq + bq - 1), lk_pad) + bk - 1) // bk

    for rows in tiles:
        q_scr[rows, :] = _rope_query(q_ref[rows, :], cos_ref[rows, :], sin_ref[rows, :], nope, scale)
    m_scr[...] = jnp.full(m_scr.shape, NEG_INF, F32)
    acc_scr[...] = jnp.zeros(acc_scr.shape, F32)

    gsz = bq // n_groups
    groups = [slice(g * gsz, (g + 1) * gsz) for g in range(n_groups)]
    tiles_per_group = gsz // rt
    last_block = lk_pad // bk - 1
    last_g = n_groups - 1

    def block_start(j):
        return pl.multiple_of(j * bk, bk)

    def scores(g, j):
        s_scr[groups[g], :] = _dot(q_scr[groups[g], :], kt_ref[:, pl.ds(block_start(j), bk)])

    def values(g, j):
        rows = groups[g]
        a = a_scr[rows, :]
        acc_scr[rows, :] = (jnp.concatenate([a] * (acc_scr.shape[1] // LANES), axis=1) * acc_scr[rows, :]
                            + _dot(p_scr[rows, :], v_ref[pl.ds(block_start(j), bk), :]))

    p_scr[groups[last_g], :] = jnp.zeros((gsz, bk), BF16)
    a_scr[groups[last_g], :] = jnp.ones((gsz, LANES), F32)
    scores(0, 0)

    def step(j, carry, masked):
        start = block_start(j)

        def softmax(g):
            for t in range(g * tiles_per_group, (g + 1) * tiles_per_group):
                rows = tiles[t]
                s = s_scr[rows, :]
                if masked:
                    kpos = start + lax.broadcasted_iota(jnp.int32, (1, bk), 1)
                    vis = visible_end(qi * bq + t * rt + lax.broadcasted_iota(jnp.int32, (rt, 1), 0))
                    s = jnp.where(kpos < vis, s, NEG_INF)
                m_old = m_scr[rows, :]
                m_new = jnp.maximum(m_old, jnp.max(s, axis=-1, keepdims=True))
                for lc in lane_chunks:
                    p_scr[rows, lc] = jnp.exp2(s[:, lc] - m_new).astype(BF16)
                a_scr[rows, :] = jnp.exp2(m_old - m_new)
                m_scr[rows, :] = m_new

        values(last_g, jnp.maximum(j - 1, 0))
        for g in range(n_groups):
            if g < last_g:
                scores(g + 1, j)
            else:
                scores(0, jnp.minimum(j + 1, last_block))
            softmax(g)
            if g < last_g:
                values(g, j)
        return carry

    lax.fori_loop(0, n_full, functools.partial(step, masked=False), 0)
    lax.fori_loop(n_full, nkb, functools.partial(step, masked=True), 0)
    values(last_g, nkb - 1)
    for rows in tiles:
        o_ref[rows, :] = (acc_scr[rows, :vd] / acc_scr[rows, vd:vd + 1]).astype(o_ref.dtype)


def _attn_full_kernel(q_ref, cos_ref, sin_ref, knt_ref, krt_ref, v_ref, o_ref, *, hb, nope, vd, scale, n_keys):
    dqk = nope + LANES
    lq = q_ref.shape[0]
    heads = range(hb)
    cos, sin = cos_ref[...], sin_ref[...]
    kpos = lax.broadcasted_iota(jnp.int32, (1, knt_ref.shape[1]), 1)
    qs = [_rope_query(q_ref[:, h * dqk:(h + 1) * dqk], cos, sin, nope, scale) for h in heads]
    s_rope = _dot(jnp.concatenate([q[:, nope:] for q in qs], axis=0), krt_ref[...])
    ss = [jnp.where(kpos < n_keys,
                    _dot(qs[h][:, :nope], knt_ref[h * nope:(h + 1) * nope, :]) + s_rope[h * lq:(h + 1) * lq],
                    NEG_INF) for h in heads]
    es = [jnp.exp2(s - jnp.max(s, axis=-1, keepdims=True)) for s in ss]
    ls = [jnp.sum(e, axis=-1, keepdims=True) for e in es]
    pv = [_dot(es[h].astype(BF16), v_ref[:, h * vd:(h + 1) * vd]) for h in heads]
    for h in heads:
        o_ref[:, h * vd:(h + 1) * vd] = (pv[h] / ls[h]).astype(o_ref.dtype)


def _attention_causal(q, cos_t, sin_t, kt, vf, *, nb, lq, lk_pad, heads, nope, vd, bq, bk, scale, n_meta):
    dqk = nope + LANES
    nq = lq // bq
    rt = 32 if bq % 32 == 0 else bq
    n_groups = 4 if bq % (4 * rt) == 0 else 2
    assert bq % (n_groups * rt) == 0
    return pl.pallas_call(
        functools.partial(_attn_causal_kernel, bq=bq, bk=bk, rt=rt, n_groups=n_groups, nope=nope, vd=vd,
                          scale=scale,
                          n_meta=n_meta, lk_pad=lk_pad),
        out_shape=jax.ShapeDtypeStruct((q.shape[0], heads * vd), BF16),
        grid=(nb, heads, nq),
        in_specs=[pl.BlockSpec((bq, dqk), lambda b, h, i: (b * nq + i, h)),
                  pl.BlockSpec((bq, LANES), lambda b, h, i: (b * nq + i, 0)),
                  pl.BlockSpec((bq, LANES), lambda b, h, i: (b * nq + i, 0)),
                  pl.BlockSpec((dqk, lk_pad), lambda b, h, i: (h, b)),
                  pl.BlockSpec((lk_pad, vd + LANES), lambda b, h, i: (b, h))],
        out_specs=pl.BlockSpec((bq, vd), lambda b, h, i: (b * nq + i, h)),
        scratch_shapes=[pltpu.VMEM((bq, dqk), BF16), pltpu.VMEM((bq, bk), F32), pltpu.VMEM((bq, bk), BF16),
                        pltpu.VMEM((bq, LANES), F32), pltpu.VMEM((bq, LANES), F32),
                        pltpu.VMEM((bq, vd + LANES), F32)],
        compiler_params=_params(("parallel", "parallel", "arbitrary")),
        name="mla_attention_causal",
    )(q, cos_t, sin_t, kt, vf)


def _attention_full(q, cos_t, sin_t, knt, lat_t, vf, o_all, *, nb, lq, lk_pad, row0, heads, hb, nope, vd, kv,
                    scale, n_keys):
    dqk = nope + LANES
    blk0 = row0 // lq

    def body(q_ref, cos_ref, sin_ref, knt_ref, krt_ref, v_ref, o_all_ref, o_ref):
        _attn_full_kernel(q_ref, cos_ref, sin_ref, knt_ref, krt_ref, v_ref, o_ref,
                          hb=hb, nope=nope, vd=vd, scale=scale, n_keys=n_keys)

    return pl.pallas_call(
        body,
        out_shape=jax.ShapeDtypeStruct(o_all.shape, o_all.dtype),
        grid=(nb, heads // hb),
        in_specs=[pl.BlockSpec((lq, hb * dqk), lambda b, g: (blk0 + b, g)),
                  pl.BlockSpec((lq, LANES), lambda b, g: (blk0 + b, 0)),
                  pl.BlockSpec((lq, LANES), lambda b, g: (blk0 + b, 0)),
                  pl.BlockSpec((hb * nope, lk_pad), lambda b, g: (g, b)),
                  pl.BlockSpec((LANES, lk_pad), lambda b, g: (kv // LANES, b)),
                  pl.BlockSpec((lk_pad, hb * vd), lambda b, g: (b, g)),
                  pl.BlockSpec(memory_space=pl.ANY)],
        out_specs=pl.BlockSpec((lq, hb * vd), lambda b, g: (blk0 + b, g)),
        input_output_aliases={6: 0},
        compiler_params=_params(("parallel", "parallel")),
        name="mla_attention_full",
    )(q, cos_t, sin_t, knt, lat_t, vf, o_all)


def _item_info(i, n_p, cps, cs, l_valid):
    is_p = i < n_p
    j = i - n_p
    seq = jnp.where(is_p, i // cps, n_p // cps + j // cs)
    c = jnp.where(is_p, i % cps, j % cs)
    nc = jnp.where(is_p, cps, cs)
    nvalid = jnp.where(is_p, jnp.clip(l_valid - c * CHUNK, 0, CHUNK), CHUNK)
    return seq, c, nc, nvalid


def _tri_inverse_many(mats):
    n = mats[0].shape[0]
    r = lax.broadcasted_iota(jnp.int32, (n, n), 0)
    c = lax.broadcasted_iota(jnp.int32, (n, n), 1)
    eye = jnp.where(r == c, 1.0, 0.0)
    zero = jnp.zeros((n, n), F32)

    def split(x):
        hi = x.astype(BF16).astype(F32)
        return hi, x - hi

    def lhs_of(x):
        hi, lo = split(x)
        return jnp.concatenate([hi, lo, hi, zero], axis=1).astype(BF16)

    def rhs_of(x):
        hi, lo = split(x)
        return jnp.concatenate([hi, hi, lo, zero], axis=0).astype(BF16)

    ps = [eye - a for a in mats]
    aks = list(mats)
    rhs = [rhs_of(a) for a in mats]
    power = 2
    while power < n:
        aks = [_dot(lhs_of(ak), rk) for ak, rk in zip(aks, rhs)]
        rhs = [rhs_of(ak) for ak in aks]
        ps = [p + _dot(lhs_of(p), rk) for p, rk in zip(ps, rhs)]
        power *= 2
    return ps


def _gdn_kernel(qc_ref, kc_ref, vc_ref, qp_ref, kp_ref, vp_ref, q0_ref, k0_ref, v0_ref,
                wq_ref, wk_ref, wv_ref, ab_ref, alog_ref, dtb_ref, z_ref, gain_ref, s0_ref,
                o_ref, snew_ref, srun_ref,
                bufq, bufk, bufv, val_s, kcd_s, qdec_s, kdec_s, qk_s, eg_s, s_ref,
                *, hg, dk, n_items, n_p, cps, cs, l_valid, width):
    C = CHUNK
    t = pl.program_id(1)
    i = jnp.minimum(t, n_items - 1)
    j = jnp.maximum(t - 1, 0)
    _, c, _, nvalid = _item_info(i, n_p, cps, cs, l_valid)
    _, cj, ncj, _ = _item_info(j, n_p, cps, cs, l_valid)
    first = c == 0
    heads = range(hg)
    sls = [slice(h * dk, (h + 1) * dk) for h in heads]

    @pl.when(t == 0)
    def _():
        for ref in (val_s, kcd_s, qdec_s, kdec_s, qk_s, eg_s):
            ref[...] = jnp.zeros(ref.shape, ref.dtype)

    n_new = n_p // cps
    seq_j = _item_info(j, n_p, cps, cs, l_valid)[0]

    @pl.when(cj == 0)
    def _():
        s_ref[...] = jnp.where(seq_j < n_new, 0.0, s0_ref[0])

    sb = [s_ref[h].astype(BF16) for h in heads]
    ks_s = [_dot(kcd_s[:, sls[h]], sb[h]) for h in heads]
    qs_s = [_dot(qdec_s[:, sls[h]], sb[h]) for h in heads]

    def conv_silu(cur_ref, prev_ref, init_ref, w_ref, buf):
        buf[0:SUBLANES, :] = jnp.where(first, init_ref[0], prev_ref[...])
        buf[SUBLANES:SUBLANES + C, :] = cur_ref[...]
        base = SUBLANES - (width - 1)
        y = buf[base:base + C, :] * w_ref[0:1, :]
        for t in range(1, width):
            y = y + buf[base + t:base + t + C, :] * w_ref[t:t + 1, :]
        return y * _sigmoid(y)

    qs = conv_silu(qc_ref, qp_ref, q0_ref, wq_ref, bufq)
    ks = conv_silu(kc_ref, kp_ref, k0_ref, wk_ref, bufk)
    vs = conv_silu(vc_ref, vp_ref, v0_ref, wv_ref, bufv)

    ub = [(val_s[:, sls[h]] - ks_s[h]).astype(BF16) for h in heads]
    ds_s = [_dot_tn(kdec_s[:, sls[h]], ub[h]) for h in heads]
    os_s = [qs_s[h] + _dot(qk_s[:, h * C:(h + 1) * C].astype(BF16), ub[h]) for h in heads]

    ab = ab_ref[...]
    rows = lax.broadcasted_iota(jnp.int32, (C, LANES), 0)
    valid = rows < nvalid
    xs = ab + dtb_ref[0]
    softplus = jnp.maximum(xs, 0.0) + jnp.log(1.0 + jnp.exp(-jnp.abs(xs)))
    g = jnp.where(valid, -jnp.exp(alog_ref[0]) * softplus, 0.0)
    beta = jnp.where(valid, _sigmoid(ab), 0.0)
    r = lax.broadcasted_iota(jnp.int32, (C, C), 0)
    cc = lax.broadcasted_iota(jnp.int32, (C, C), 1)
    incl = r >= cc
    strict = r > cc
    gcum = jnp.dot(jnp.where(incl, 1.0, 0.0), g, precision=_HI, preferred_element_type=F32)
    gcum_t = gcum.T

    qn = [qs[:, sl] * lax.rsqrt(jnp.sum(qs[:, sl] * qs[:, sl], axis=-1, keepdims=True) + NORM_EPS)
          * (dk ** -0.5) for sl in sls]
    kn = [ks[:, sl] * lax.rsqrt(jnp.sum(ks[:, sl] * ks[:, sl], axis=-1, keepdims=True) + NORM_EPS)
          for sl in sls]
    bcol = [beta[:, hg + h:hg + h + 1] for h in heads]
    gcol = [gcum[:, h:h + 1] for h in heads]
    glast = [gcum[C - 1:C, h:h + 1] for h in heads]
    eg = [jnp.exp(gcol[h]) for h in heads]
    decay = [jnp.where(incl, jnp.exp(jnp.where(incl, gcol[h] - gcum_t[h:h + 1, :], 0.0)), 0.0) for h in heads]
    kb = [kn[h] * bcol[h] for h in heads]
    knb = [kn[h].astype(BF16) for h in heads]
    a_mat = [jnp.where(strict, _dot_nt(kb[h].astype(BF16), knb[h]) * decay[h], 0.0) for h in heads]

    gain = gain_ref[...]
    for h in heads:
        s_ref[h] = s_ref[h] * eg_s[h:h + 1, :] + ds_s[h]
    for h in heads:
        o = os_s[h]
        z = z_ref[:, sls[h]]
        ms = jnp.mean(o * o, axis=-1, keepdims=True)
        o_ref[:, sls[h]] = (o * lax.rsqrt(ms + NORM_EPS) * gain * (z * _sigmoid(z))).astype(o_ref.dtype)

    t_mat = _tri_inverse_many(a_mat)
    rhs = [jnp.concatenate([vs[:, sls[h]] * bcol[h], kb[h] * eg[h]], axis=1).astype(BF16) for h in heads]
    tv = [_dot(t_mat[h].astype(BF16), rhs[h]) for h in heads]
    qk = [_dot_nt(qn[h].astype(BF16), knb[h]) * decay[h] for h in heads]
    for h in heads:
        val_s[:, sls[h]] = tv[h][:, :dk]
        kcd_s[:, sls[h]] = tv[h][:, dk:].astype(BF16)
        qdec_s[:, sls[h]] = (qn[h] * eg[h]).astype(BF16)
        kdec_s[:, sls[h]] = (kn[h] * jnp.exp(glast[h] - gcol[h])).astype(BF16)
        eg_s[h:h + 1, :] = jnp.broadcast_to(jnp.exp(glast[h]), (1, LANES))
    qk_s[...] = jnp.concatenate(qk, axis=1)

    done = (cj == ncj - 1) & (t > 0)

    @pl.when(done & (seq_j < n_new))
    def _():
        snew_ref[0] = s_ref[...]

    @pl.when(done & (seq_j >= n_new))
    def _():
        srun_ref[0] = s_ref[...]


def _gdn_mixer(proj, gates, conv0, s0, conv_w, a_log, dt_bias, o_gain, *, heads, dk, hg, n_p, cps, cs, l_valid):
    m = proj.shape[0]
    C = CHUNK
    w = heads * dk
    n_hg = heads // hg
    gw = hg * dk
    n_items = m // C
    n_new, n_run = n_p // cps, s0.shape[0]
    width = conv_w.shape[0]
    info = dict(n_p=n_p, cps=cps, cs=cs, l_valid=l_valid)

    def seq_of(i):
        return _item_info(i, n_p, cps, cs, l_valid)[0]

    prep = lambda t: jnp.minimum(t, n_items - 1)
    scan = lambda t: jnp.maximum(t - 1, 0)
    cur = lambda off: pl.BlockSpec((C, gw), lambda g, t: (prep(t), off * n_hg + g))
    prev = lambda off: pl.BlockSpec(
        (SUBLANES, gw), lambda g, t: (jnp.maximum(prep(t) * (C // SUBLANES) - 1, 0), off * n_hg + g))
    init = lambda off: pl.BlockSpec((1, SUBLANES, gw), lambda g, t: (seq_of(prep(t)), 0, off * n_hg + g))
    wspec = lambda off: pl.BlockSpec((width, gw), lambda g, t: (0, off * n_hg + g))
    gate_vec = pl.BlockSpec((1, 1, LANES), lambda g, t: (g, 0, 0))
    state_new = pl.BlockSpec((1, hg, dk, dk), lambda g, t: (jnp.minimum(seq_of(scan(t)), n_new - 1), g, 0, 0))
    state_run = pl.BlockSpec((1, hg, dk, dk), lambda g, t: (jnp.maximum(seq_of(scan(t)) - n_new, 0), g, 0, 0))

    alog_g = jnp.pad(a_log.reshape(n_hg, 1, hg).astype(F32), ((0, 0), (0, 0), (0, LANES - hg)))
    dtb_g = jnp.pad(dt_bias.reshape(n_hg, 1, hg).astype(F32), ((0, 0), (0, 0), (0, LANES - hg)))

    return pl.pallas_call(
        functools.partial(_gdn_kernel, hg=hg, dk=dk, width=width, n_items=n_items, **info),
        out_shape=(jax.ShapeDtypeStruct((m, w), BF16),
                   jax.ShapeDtypeStruct((n_new, heads, dk, dk), F32),
                   jax.ShapeDtypeStruct((n_run, heads, dk, dk), F32)),
        grid=(n_hg, n_items + 1),
        in_specs=[cur(0), cur(1), cur(2), prev(0), prev(1), prev(2), init(0), init(1), init(2),
                  wspec(0), wspec(1), wspec(2),
                  pl.BlockSpec((C, LANES), lambda g, t: (prep(t), g)),
                  gate_vec, gate_vec,
                  pl.BlockSpec((C, gw), lambda g, t: (scan(t), 3 * n_hg + g)),
                  pl.BlockSpec((1, dk), lambda g, t: (0, 0)),
                  state_run],
        out_specs=(pl.BlockSpec((C, gw), lambda g, t: (scan(t), g)), state_new, state_run),
        scratch_shapes=[pltpu.VMEM((SUBLANES + C, gw), F32)] * 3
        + [pltpu.VMEM((C, gw), F32), pltpu.VMEM((C, gw), BF16), pltpu.VMEM((C, gw), BF16),
           pltpu.VMEM((C, gw), BF16), pltpu.VMEM((C, hg * C), F32), pltpu.VMEM((hg, LANES), F32),
           pltpu.VMEM((hg, dk, dk), F32)],
        compiler_params=_params(("parallel", "arbitrary")),
        name="gdn_mixer",
    )(proj, proj, proj, proj, proj, proj, conv0, conv0, conv0, conv_w, conv_w, conv_w,
      gates, alog_g, dtb_g, proj, o_gain.reshape(1, dk).astype(F32), s0)


def _split_half_layout(x, half):
    pad = [(0, 0)] * (x.ndim - 1) + [(0, LANES // 2 - half)]
    return jnp.concatenate([jnp.pad(x[..., :half], pad), jnp.pad(x[..., half:], pad)], axis=-1)


def _rope_tables(pos, half):
    inv_freq = 1.0 / (ROPE_THETA ** (jnp.arange(half, dtype=F32) / half))
    ang = pos[:, None] * inv_freq[None, :]
    cos, sin = jnp.cos(ang), jnp.sin(ang)
    cos_t = _split_half_layout(jnp.concatenate([cos, cos], -1), half)
    sin_t = _split_half_layout(jnp.concatenate([-sin, sin], -1), half)
    return cos_t, sin_t


def kernel(x_prompt, x_sample, state_conv, state_delta, cache_ckv, cache_krope, meta_tokens,
           a_w_in, a_conv_w, a_a_log, a_dt_bias, a_o_gain, a_w_o,
           b_w_dq, b_q_gain, b_w_uq, b_w_o,
           kv_w_dkv, kv_gain, kv_w_uk, kv_w_uv,
           mlp_w_up, mlp_w_down, ln_gain, ln_bias):
    B, seq, D = x_prompt.shape
    DB, ts, _ = x_sample.shape
    n_meta = meta_tokens.shape[0]
    depth = ln_gain.shape[0]
    n_a = a_w_in.shape[0]
    H = a_a_log.shape[1]
    dk = a_o_gain.shape[1]
    W = H * dk
    width = a_conv_w.shape[1]
    past = cache_ckv.shape[1]
    KV, MH, nope = kv_w_uk.shape
    vd = kv_w_uv.shape[2]
    rope = cache_krope.shape[2]
    half = rope // 2
    alpha = (2 * depth) ** 0.25
    scale = (nope + rope) ** -0.5
    assert ts % CHUNK == 0 and ts >= width - 1 and half < LANES // 2
    assert nope % LANES == 0 and vd % LANES == 0 and KV % LANES == 0 and dk % LANES == 0

    L = n_meta + seq
    LP = _round_up(L, LANES)
    n_prompt_rows = B * LP
    M = n_prompt_rows + DB * ts
    cps, cs = LP // CHUNK, ts // CHUNK
    n_p = B * cps
    hg = min(8, H)
    n_hg = H // hg

    meta = jnp.broadcast_to(meta_tokens.astype(F32)[None], (B, n_meta, D))
    xp = jnp.concatenate([meta, x_prompt, jnp.zeros((B, LP - L, D), F32)], axis=1)
    x = jnp.concatenate([xp.reshape(n_prompt_rows, D), x_sample.reshape(DB * ts, D)], axis=0)
    xb = x.astype(BF16)

    pos = jnp.concatenate([jnp.tile(jnp.arange(LP, dtype=F32), B),
                           jnp.tile(past + jnp.arange(ts, dtype=F32), DB)])
    cos_t, sin_t = _rope_tables(pos, half)

    conv0 = jnp.concatenate([jnp.zeros((n_a, B, width - 1, 3 * W), F32), state_conv.astype(F32)], axis=1)
    conv0 = jnp.pad(conv0, ((0, 0), (0, 0), (SUBLANES - (width - 1), 0), (0, 0)))
    s0 = state_delta.astype(F32)

    new_conv_p, new_conv_s, new_s = [], [], []
    k_slabs = None
    lat = None
    for layer in range(depth):
        if layer < n_a:
            w_ab = a_w_in[layer, :, 4 * W:]
            gate_w = jnp.concatenate(
                [w_ab[:, :H].reshape(D, n_hg, hg), w_ab[:, H:].reshape(D, n_hg, hg),
                 jnp.zeros((D, n_hg, LANES - 2 * hg), w_ab.dtype)], axis=-1).reshape(D, n_hg * LANES)
            proj = _matmul(xb, a_w_in, F32, layer=layer, n_use=4 * W)
            gates = _matmul(xb, gate_w.astype(BF16), F32, bm=328)
            o, s_new, s_run = _gdn_mixer(proj, gates, conv0[layer], s0[layer], a_conv_w[layer].astype(F32),
                                         a_a_log[layer], a_dt_bias[layer], a_o_gain[layer],
                                         heads=H, dk=dk, hg=hg, n_p=n_p, cps=cps, cs=cs, l_valid=L)
            h = _matmul(o, a_w_o, BF16, layer=layer)
            new_conv_p.append(jnp.stack(
                [proj[b * LP + L - (width - 1):b * LP + L, :3 * W] for b in range(B)]))
            new_conv_s.append(jnp.stack(
                [proj[n_prompt_rows + ts - (width - 1) + t::ts, :3 * W] for t in range(width - 1)], axis=1))
            new_s.append((s_new, s_run))
        else:
            j = layer - n_a
            ql = _matmul(xb, b_w_dq[j].astype(BF16), BF16, epilogue="rms", gain=b_q_gain[j])
            w_uq = b_w_uq[j].reshape(-1, MH, nope + rope)
            w_uq = jnp.concatenate([w_uq[..., :nope], _split_half_layout(w_uq[..., nope:], half)], axis=-1)
            q = _matmul(ql, w_uq.reshape(-1, MH * (nope + LANES)).astype(BF16), BF16)
            kp, vp, ks, ls_t, vs, lkp, lks = k_slabs
            o = _attention_causal(q, cos_t, sin_t, kp, vp, nb=B, lq=LP, lk_pad=lkp, heads=MH, nope=nope,
                                  vd=vd, bq=_pick_block(LP, 1408, LANES), bk=512, scale=scale, n_meta=n_meta)
            o = _attention_full(q, cos_t, sin_t, ks, ls_t, vs, o, nb=DB, lq=ts, lk_pad=lks, row0=n_prompt_rows,
                                heads=MH, hb=min(4, MH), nope=nope, vd=vd, kv=KV, scale=scale,
                                n_keys=past + ts)
            h = _matmul(o, b_w_o, BF16, layer=j)

        x, xb = _deepnorm(x, h, ln_gain[layer, 0], ln_bias[layer, 0], alpha)
        hid = _matmul(xb, mlp_w_up, BF16, layer=layer, epilogue="relu2")
        h = _matmul(hid, mlp_w_down, BF16, layer=layer, bn=1024, bk=2048)
        x, xb = _deepnorm(x, h, ln_gain[layer, 1], ln_bias[layer, 1], alpha)

        if layer == n_a - 1:
            w_dkv = jnp.concatenate([kv_w_dkv[:, :KV], _split_half_layout(kv_w_dkv[:, KV:], half)], axis=1)
            lat = _kv_post(_matmul(xb, w_dkv.astype(BF16), F32, bm=328, bn=640), kv_gain.astype(F32),
                           cos_t, sin_t, KV)
            lane_ids = jnp.arange(LANES)
            eye = jnp.where((lane_ids[:, None] == lane_ids[None, :]) & (lane_ids[:, None] != half), 1.0, 0.0)
            ones_col = jnp.where((lane_ids[:, None] == half) & (lane_ids[None, :] == 0), 1.0, 0.0)
            w_k = jnp.concatenate(
                [jnp.concatenate([kv_w_uk.astype(F32), jnp.zeros((KV, MH, LANES), F32)], axis=-1),
                 jnp.concatenate([jnp.zeros((LANES, MH, nope), F32),
                                  jnp.broadcast_to(eye[:, None, :], (LANES, MH, LANES))], axis=-1)],
                axis=0).reshape(KV + LANES, MH * (nope + LANES)).astype(BF16)
            w_v = jnp.concatenate(
                [jnp.concatenate([kv_w_uv.astype(F32), jnp.zeros((KV, MH, LANES), F32)], axis=-1),
                 jnp.concatenate([jnp.zeros((LANES, MH, vd), F32),
                                  jnp.broadcast_to(ones_col[:, None, :], (LANES, MH, LANES))], axis=-1)],
                axis=0).reshape(KV + LANES, MH * (vd + LANES)).astype(BF16)
            one_lane = jnp.where(jnp.arange(KV + LANES) == KV + half, 1.0, 0.0)
            lkp = _round_up(L, 512)
            lat_p = lat[:n_prompt_rows].reshape(B, LP, KV + LANES)
            if lkp >= LP:
                lat_p = jnp.pad(lat_p, ((0, 0), (0, lkp - LP), (0, 0)))
            else:
                lat_p = lat_p[:, :lkp]
            lat_p = (lat_p.reshape(B * lkp, KV + LANES) + one_lane).astype(BF16)
            lks = _round_up(past + ts, LANES)
            cache = jnp.concatenate([cache_ckv.astype(F32), _split_half_layout(cache_krope.astype(F32), half)],
                                    axis=-1)
            lat_s = jnp.concatenate([cache, lat[n_prompt_rows:].reshape(DB, ts, KV + LANES),
                                     jnp.zeros((DB, lks - past - ts, KV + LANES), F32)], axis=1)
            lat_s = (lat_s.reshape(DB * lks, KV + LANES) + one_lane).astype(BF16)
            w_kn_t = jnp.transpose(kv_w_uk, (1, 2, 0)).reshape(MH * nope, KV).astype(BF16)
            lat_s_t = lat_s.T
            k_slabs = (_matmul(w_k.T, lat_p.T, BF16, bm=1024), _matmul(lat_p, w_v, BF16, bm=1024),
                       _matmul(w_kn_t, lat_s_t, BF16, bm=1024), lat_s_t,
                       _matmul(lat_s[:, :KV], kv_w_uv.reshape(KV, MH * vd).astype(BF16), BF16, bm=1024),
                       lkp, lks)

    def unsplit(r):
        return jnp.concatenate([r[..., :half], r[..., LANES // 2:LANES // 2 + half]], axis=-1)

    lat_p = lat[:n_prompt_rows].reshape(B, LP, KV + LANES)[:, :L]
    lat_s = lat[n_prompt_rows:].reshape(DB, ts, KV + LANES)
    y_prompt = jnp.stack([x[b * LP + n_meta:b * LP + L] for b in range(B)])
    y_sample = x[n_prompt_rows:].reshape(DB, ts, D)
    return (y_prompt, y_sample,
            jnp.stack(new_conv_p), jnp.stack([s[0] for s in new_s]),
            lat_p[..., :KV], unsplit(lat_p[..., KV:]),
            jnp.stack(new_conv_s), jnp.stack([s[1] for s in new_s]),
            lat_s[..., :KV], unsplit(lat_s[..., KV:]))
```

```python
import functools
import math

import jax
import jax.numpy as jnp
from jax import lax
from jax.experimental import pallas as pl
from jax.experimental.pallas import tpu as pltpu

CHUNK = 64
ROPE_THETA = 10000.0
NORM_EPS = 1e-6
NEG_INF = -1e30
LANES = 128
SUBLANES = 8
VMEM_LIMIT_BYTES = 56 * 1024 * 1024
_SPLIT_LEVELS = 2
_CHUNK_SHIFT = CHUNK.bit_length() - 1
assert 1 << _CHUNK_SHIFT == CHUNK

F32 = jnp.float32
BF16 = jnp.bfloat16
_HI = lax.Precision.HIGHEST


def _round_up(x, m):
    return -(-x // m) * m


def _pick_block(dim, target, align):
    best = None
    for d in range(align, min(dim, target) + 1, align):
        if dim % d == 0:
            best = d
    return best if best is not None else dim


def _dot(a, b):
    return jnp.dot(a, b, preferred_element_type=F32)


def _dot_nt(a, b):
    return lax.dot_general(a, b, (((1,), (1,)), ((), ())), preferred_element_type=F32)


def _dot_tn(a, b):
    return lax.dot_general(a, b, (((0,), (0,)), ((), ())), preferred_element_type=F32)


def _sigmoid(x):
    return 1.0 / (1.0 + jnp.exp(-x))


def _params(sem):
    return pltpu.CompilerParams(dimension_semantics=sem, vmem_limit_bytes=VMEM_LIMIT_BYTES)


def _mm_kernel(*refs, nk, epilogue, b_transposed):
    if epilogue == "rms":
        a_ref, b_ref, g_ref, o_ref = refs[:4]
        rest = refs[4:]
    else:
        a_ref, b_ref, o_ref = refs[:3]
        g_ref = None
        rest = refs[3:]

    def finish(acc):
        if epilogue == "relu2":
            r = jnp.maximum(acc, 0.0)
            acc = r * r
        elif epilogue == "rms":
            ms = jnp.mean(acc * acc, axis=-1, keepdims=True)
            acc = acc * lax.rsqrt(ms + NORM_EPS) * g_ref[...]
        o_ref[...] = acc.astype(o_ref.dtype)

    def product():
        b = b_ref[...].astype(BF16)
        return _dot_nt(a_ref[...], b) if b_transposed else _dot(a_ref[...], b)

    if nk == 1:
        finish(product())
    else:
        acc_ref = rest[0]
        k = pl.program_id(2)

        @pl.when(k == 0)
        def _():
            acc_ref[...] = jnp.zeros_like(acc_ref)

        acc_ref[...] += product()

        @pl.when(k == nk - 1)
        def _():
            finish(acc_ref[...])


def _matmul(a, b, out_dtype, *, layer=None, n_use=None, b_transposed=False, epilogue="none", gain=None,
            bm=1312, bn=512, bk=4096):
    m, kdim = a.shape
    n = n_use if n_use is not None else b.shape[-2 if b_transposed else -1]
    bm = _pick_block(m, bm, 16)
    bn = n if epilogue == "rms" else _pick_block(n, bn, LANES)
    bk = _pick_block(kdim, bk, LANES)
    nk = kdim // bk
    a_mode = dict(pipeline_mode=pl.Buffered(1)) if nk == 1 and n // bn >= 4 else {}
    b_blk = (bn, bk) if b_transposed else (bk, bn)
    b_idx = (lambda k, j: (j, k)) if b_transposed else (lambda k, j: (k, j))
    if b.ndim == 3:
        b_spec = pl.BlockSpec((None,) + b_blk, lambda i, j, k: (layer,) + b_idx(k, j))
    else:
        b_spec = pl.BlockSpec(b_blk, lambda i, j, k: b_idx(k, j))
    in_specs = [pl.BlockSpec((bm, bk), lambda i, j, k: (i, k), **a_mode), b_spec]
    args = [a, b]
    if epilogue == "rms":
        in_specs.append(pl.BlockSpec((1, bn), lambda i, j, k: (0, j)))
        args.append(gain.reshape(1, n).astype(F32))
    scratch = [pltpu.VMEM((bm, bn), F32)] if nk > 1 else []
    return pl.pallas_call(
        functools.partial(_mm_kernel, nk=nk, epilogue=epilogue, b_transposed=b_transposed),
        out_shape=jax.ShapeDtypeStruct((m, n), out_dtype),
        grid=(m // bm, n // bn, nk),
        in_specs=in_specs,
        out_specs=pl.BlockSpec((bm, bn), lambda i, j, k: (i, j)),
        scratch_shapes=scratch,
        compiler_params=_params(("parallel", "parallel", "arbitrary")),
        name="matmul_" + epilogue,
    )(*args)


def _ln_kernel(x_ref, h_ref, g_ref, b_ref, of_ref, ob_ref, *, alpha):
    y = alpha * x_ref[...] + h_ref[...].astype(F32)
    mu = jnp.mean(y, axis=-1, keepdims=True)
    d = y - mu
    var = jnp.mean(d * d, axis=-1, keepdims=True)
    out = d * lax.rsqrt(var + NORM_EPS) * g_ref[...] + b_ref[...]
    of_ref[...] = out
    ob_ref[...] = out.astype(BF16)


def _deepnorm(x, h, gain, bias, alpha):
    m, d = x.shape
    br = _pick_block(m, 256, 16)
    row = pl.BlockSpec((br, d), lambda i: (i, 0))
    vec = pl.BlockSpec((1, d), lambda i: (0, 0))
    return pl.pallas_call(
        functools.partial(_ln_kernel, alpha=alpha),
        out_shape=(jax.ShapeDtypeStruct((m, d), F32), jax.ShapeDtypeStruct((m, d), BF16)),
        grid=(m // br,),
        in_specs=[row, row, vec, vec],
        out_specs=(row, row),
        compiler_params=_params(("parallel",)),
        name="deepnorm",
    )(x, h, gain.reshape(1, d), bias.reshape(1, d))


def _kv_post_kernel(lat_ref, g_ref, cos_ref, sin_ref, o_ref, *, kv):
    c = lat_ref[:, :kv]
    ms = jnp.mean(c * c, axis=-1, keepdims=True)
    o_ref[:, :kv] = c * lax.rsqrt(ms + NORM_EPS) * g_ref[...]
    r = lat_ref[:, kv:]
    o_ref[:, kv:] = r * cos_ref[...] + pltpu.roll(r, LANES // 2, axis=1) * sin_ref[...]


def _kv_post(lat, gain, cos_t, sin_t, kv):
    m, w = lat.shape
    br = _pick_block(m, 256, 8)
    return pl.pallas_call(
        functools.partial(_kv_post_kernel, kv=kv),
        out_shape=jax.ShapeDtypeStruct((m, w), F32),
        grid=(m // br,),
        in_specs=[pl.BlockSpec((br, w), lambda i: (i, 0)),
                  pl.BlockSpec((1, kv), lambda i: (0, 0)),
                  pl.BlockSpec((br, LANES), lambda i: (i, 0)),
                  pl.BlockSpec((br, LANES), lambda i: (i, 0))],
        out_specs=pl.BlockSpec((br, w), lambda i: (i, 0)),
        compiler_params=_params(("parallel",)),
        name="kv_post",
    )(lat, gain.reshape(1, kv), cos_t, sin_t)


def _rope_query(q, cos, sin, nope, scale):
    q = q.astype(F32)
    qr = q[:, nope:]
    qr = qr * cos + pltpu.roll(qr, LANES // 2, axis=1) * sin
    return (jnp.concatenate([q[:, :nope], qr], axis=1) * (scale * math.log2(math.e))).astype(BF16)


def _attn_causal_kernel(q_ref, cos_ref, sin_ref, kt_ref, v_ref, o_ref,
                        q_scr, s_scr, p_scr, m_scr, a_scr, acc_scr, *, bq, bk, rt, n_groups, nope, vd, scale,
                        n_meta, lk_pad):
    qi = pl.program_id(2)
    tiles = [slice(t * rt, (t + 1) * rt) for t in range(bq // rt)]
    lane_chunks = [slice(c * LANES, (c + 1) * LANES) for c in range(bk // LANES)]

    def visible_end(r):
        chunk_id = lax.shift_right_arithmetic(r - n_meta, _CHUNK_SHIFT) + 1
        return jnp.where(r < n_meta, n_meta, chunk_id * CHUNK + n_meta)

    n_full = visible_end(qi * bq) // bk
    nkb = (jnp.minimum(visible_end(qi * bq + bq - 1), lk_pad) + bk - 1) // bk

    for rows in tiles:
        q_scr[rows, :] = _rope_query(q_ref[rows, :], cos_ref[rows, :], sin_ref[rows, :], nope, scale)
    m_scr[...] = jnp.full(m_scr.shape, NEG_INF, F32)
    acc_scr[...] = jnp.zeros(acc_scr.shape, F32)

    gsz = bq // n_groups
    groups = [slice(g * gsz, (g + 1) * gsz) for g in range(n_groups)]
    tiles_per_group = gsz // rt
    last_block = lk_pad // bk - 1
    last_g = n_groups - 1

    def block_start(j):
        return pl.multiple_of(j * bk, bk)

    def scores(g, j):
        s_scr[groups[g], :] = _dot(q_scr[groups[g], :], kt_ref[:, pl.ds(block_start(j), bk)])

    def values(g, j):
        rows = groups[g]
        a = a_scr[rows, :]
        acc_scr[rows, :] = (jnp.concatenate([a] * (acc_scr.shape[1] // LANES), axis=1) * acc_scr[rows, :]
                            + _dot(p_scr[rows, :], v_ref[pl.ds(block_start(j), bk), :]))

    p_scr[groups[last_g], :] = jnp.zeros((gsz, bk), BF16)
    a_scr[groups[last_g], :] = jnp.ones((gsz, LANES), F32)
    scores(0, 0)

    def step(j, carry, masked):
        start = block_start(j)

        def softmax(g):
            for t in range(g * tiles_per_group, (g + 1) * tiles_per_group):
                rows = tiles[t]
                s = s_scr[rows, :]
                if masked:
                    kpos = start + lax.broadcasted_iota(jnp.int32, (1, bk), 1)
                    vis = visible_end(qi * bq + t * rt + lax.broadcasted_iota(jnp.int32, (rt, 1), 0))
                    s = jnp.where(kpos < vis, s, NEG_INF)
                m_old = m_scr[rows, :]
                m_new = jnp.maximum(m_old, jnp.max(s, axis=-1, keepdims=True))
                for lc in lane_chunks:
                    p_scr[rows, lc] = jnp.exp2(s[:, lc] - m_new).astype(BF16)
                a_scr[rows, :] = jnp.exp2(m_old - m_new)
                m_scr[rows, :] = m_new

        values(last_g, jnp.maximum(j - 1, 0))
        for g in range(n_groups):
            if g < last_g:
                scores(g + 1, j)
            else:
                scores(0, jnp.minimum(j + 1, last_block))
            softmax(g)
            if g < last_g:
                values(g, j)
        return carry

    lax.fori_loop(0, n_full, functools.partial(step, masked=False), 0)
    lax.fori_loop(n_full, nkb, functools.partial(step, masked=True), 0)
    values(last_g, nkb - 1)
    for rows in tiles:
        o_ref[rows, :] = (acc_scr[rows, :vd] / acc_scr[rows, vd:vd + 1]).astype(o_ref.dtype)


def _attn_full_kernel(q_ref, cos_ref, sin_ref, knt_ref, krt_ref, v_ref, o_ref, *, hb, nope, vd, scale, n_keys):
    dqk = nope + LANES
    lq = q_ref.shape[0]
    heads = range(hb)
    cos, sin = cos_ref[...], sin_ref[...]
    kpos = lax.broadcasted_iota(jnp.int32, (1, knt_ref.shape[1]), 1)
    qs = [_rope_query(q_ref[:, h * dqk:(h + 1) * dqk], cos, sin, nope, scale) for h in heads]
    s_rope = _dot(jnp.concatenate([q[:, nope:] for q in qs], axis=0), krt_ref[...])
    ss = [jnp.where(kpos < n_keys,
                    _dot(qs[h][:, :nope], knt_ref[h * nope:(h + 1) * nope, :]) + s_rope[h * lq:(h + 1) * lq],
                    NEG_INF) for h in heads]
    es = [jnp.exp2(s - jnp.max(s, axis=-1, keepdims=True)) for s in ss]
    ls = [jnp.sum(e, axis=-1, keepdims=True) for e in es]
    pv = [_dot(es[h].astype(BF16), v_ref[:, h * vd:(h + 1) * vd]) for h in heads]
    for h in heads:
        o_ref[:, h * vd:(h + 1) * vd] = (pv[h] / ls[h]).astype(o_ref.dtype)


def _attention_causal(q, cos_t, sin_t, kt, vf, *, nb, lq, lk_pad, heads, nope, vd, bq, bk, scale, n_meta):
    dqk = nope + LANES
    nq = lq // bq
    rt = 32 if bq % 32 == 0 else bq
    n_groups = 4 if bq % (4 * rt) == 0 else 2
    assert bq % (n_groups * rt) == 0
    return pl.pallas_call(
        functools.partial(_attn_causal_kernel, bq=bq, bk=bk, rt=rt, n_groups=n_groups, nope=nope, vd=vd,
                          scale=scale,
                          n_meta=n_meta, lk_pad=lk_pad),
        out_shape=jax.ShapeDtypeStruct((q.shape[0], heads * vd), BF16),
        grid=(nb, heads, nq),
        in_specs=[pl.BlockSpec((bq, dqk), lambda b, h, i: (b * nq + i, h)),
                  pl.BlockSpec((bq, LANES), lambda b, h, i: (b * nq + i, 0)),
                  pl.BlockSpec((bq, LANES), lambda b, h, i: (b * nq + i, 0)),
                  pl.BlockSpec((dqk, lk_pad), lambda b, h, i: (h, b)),
                  pl.BlockSpec((lk_pad, vd + LANES), lambda b, h, i: (b, h))],
        out_specs=pl.BlockSpec((bq, vd), lambda b, h, i: (b * nq + i, h)),
        scratch_shapes=[pltpu.VMEM((bq, dqk), BF16), pltpu.VMEM((bq, bk), F32), pltpu.VMEM((bq, bk), BF16),
                        pltpu.VMEM((bq, LANES), F32), pltpu.VMEM((bq, LANES), F32),
                        pltpu.VMEM((bq, vd + LANES), F32)],
        compiler_params=_params(("parallel", "parallel", "arbitrary")),
        name="mla_attention_causal",
    )(q, cos_t, sin_t, kt, vf)


def _attention_full(q, cos_t, sin_t, knt, lat_t, vf, o_all, *, nb, lq, lk_pad, row0, heads, hb, nope, vd, kv,
                    scale, n_keys):
    dqk = nope + LANES
    blk0 = row0 // lq

    def body(q_ref, cos_ref, sin_ref, knt_ref, krt_ref, v_ref, o_all_ref, o_ref):
        _attn_full_kernel(q_ref, cos_ref, sin_ref, knt_ref, krt_ref, v_ref, o_ref,
                          hb=hb, nope=nope, vd=vd, scale=scale, n_keys=n_keys)

    return pl.pallas_call(
        body,
        out_shape=jax.ShapeDtypeStruct(o_all.shape, o_all.dtype),
        grid=(nb, heads // hb),
        in_specs=[pl.BlockSpec((lq, hb * dqk), lambda b, g: (blk0 + b, g)),
                  pl.BlockSpec((lq, LANES), lambda b, g: (blk0 + b, 0)),
                  pl.BlockSpec((lq, LANES), lambda b, g: (blk0 + b, 0)),
                  pl.BlockSpec((hb * nope, lk_pad), lambda b, g: (g, b)),
                  pl.BlockSpec((LANES, lk_pad), lambda b, g: (kv // LANES, b)),
                  pl.BlockSpec((lk_pad, hb * vd), lambda b, g: (b, g)),
                  pl.BlockSpec(memory_space=pl.ANY)],
        out_specs=pl.BlockSpec((lq, hb * vd), lambda b, g: (blk0 + b, g)),
        input_output_aliases={6: 0},
        compiler_params=_params(("parallel", "parallel")),
        name="mla_attention_full",
    )(q, cos_t, sin_t, knt, lat_t, vf, o_all)


def _item_info(i, n_p, cps, cs, l_valid):
    is_p = i < n_p
    j = i - n_p
    seq = jnp.where(is_p, i // cps, n_p // cps + j // cs)
    c = jnp.where(is_p, i % cps, j % cs)
    nc = jnp.where(is_p, cps, cs)
    nvalid = jnp.where(is_p, jnp.clip(l_valid - c * CHUNK, 0, CHUNK), CHUNK)
    return seq, c, nc, nvalid


def _tri_inverse_many(mats):
    n = mats[0].shape[0]
    r = lax.broadcasted_iota(jnp.int32, (n, n), 0)
    c = lax.broadcasted_iota(jnp.int32, (n, n), 1)
    eye = jnp.where(r == c, 1.0, 0.0)
    zero = jnp.zeros((n, n), F32)

    def split(x):
        hi = x.astype(BF16).astype(F32)
        return hi, x - hi

    def lhs_of(x):
        hi, lo = split(x)
        return jnp.concatenate([hi, lo, hi, zero], axis=1).astype(BF16)

    def rhs_of(x):
        hi, lo = split(x)
        return jnp.concatenate([hi, hi, lo, zero], axis=0).astype(BF16)

    ps = [eye - a for a in mats]
    aks = list(mats)
    rhs = [rhs_of(a) for a in mats]
    power, level = 2, 0
    while power < n:
        if level < _SPLIT_LEVELS:
            aks = [_dot(lhs_of(ak), rk) for ak, rk in zip(aks, rhs)]
            rhs = [rhs_of(ak) for ak in aks]
            ps = [p + _dot(lhs_of(p), rk) for p, rk in zip(ps, rhs)]
        else:
            akb = [ak.astype(BF16) for ak in aks]
            aks = [_dot(b, b) for b in akb]
            ps = [p + _dot(p.astype(BF16), ak.astype(BF16)) for p, ak in zip(ps, aks)]
        power *= 2
        level += 1
    return ps


def _gdn_kernel(qc_ref, kc_ref, vc_ref, qp_ref, kp_ref, vp_ref, q0_ref, k0_ref, v0_ref,
                wq_ref, wk_ref, wv_ref, ab_ref, alog_ref, dtb_ref, z_ref, gain_ref, s0_ref,
                o_ref, snew_ref, srun_ref,
                bufq, bufk, bufv, val_s, kcd_s, qdec_s, kdec_s, qk_s, eg_s, s_ref,
                *, hg, dk, n_items, n_p, cps, cs, l_valid, width):
    C = CHUNK
    t = pl.program_id(1)
    i = jnp.minimum(t, n_items - 1)
    j = jnp.maximum(t - 1, 0)
    _, c, _, nvalid = _item_info(i, n_p, cps, cs, l_valid)
    _, cj, ncj, _ = _item_info(j, n_p, cps, cs, l_valid)
    first = c == 0
    heads = range(hg)
    sls = [slice(h * dk, (h + 1) * dk) for h in heads]

    @pl.when(t == 0)
    def _():
        for ref in (val_s, kcd_s, qdec_s, kdec_s, qk_s, eg_s):
            ref[...] = jnp.zeros(ref.shape, ref.dtype)

    n_new = n_p // cps
    seq_j = _item_info(j, n_p, cps, cs, l_valid)[0]

    @pl.when(cj == 0)
    def _():
        s_ref[...] = jnp.where(seq_j < n_new, 0.0, s0_ref[0])

    sb = [s_ref[h].astype(BF16) for h in heads]
    ks_s = [_dot(kcd_s[:, sls[h]], sb[h]) for h in heads]
    qs_s = [_dot(qdec_s[:, sls[h]], sb[h]) for h in heads]

    def conv_silu(cur_ref, prev_ref, init_ref, w_ref, buf):
        buf[0:SUBLANES, :] = jnp.where(first, init_ref[0], prev_ref[...])
        buf[SUBLANES:SUBLANES + C, :] = cur_ref[...]
        base = SUBLANES - (width - 1)
        y = buf[base:base + C, :] * w_ref[0:1, :]
        for t in range(1, width):
            y = y + buf[base + t:base + t + C, :] * w_ref[t:t + 1, :]
        return y * _sigmoid(y)

    qs = conv_silu(qc_ref, qp_ref, q0_ref, wq_ref, bufq)
    ks = conv_silu(kc_ref, kp_ref, k0_ref, wk_ref, bufk)
    vs = conv_silu(vc_ref, vp_ref, v0_ref, wv_ref, bufv)

    ub = [(val_s[:, sls[h]] - ks_s[h]).astype(BF16) for h in heads]
    ds_s = [_dot_tn(kdec_s[:, sls[h]], ub[h]) for h in heads]
    os_s = [qs_s[h] + _dot(qk_s[:, h * C:(h + 1) * C].astype(BF16), ub[h]) for h in heads]

    ab = ab_ref[...]
    rows = lax.broadcasted_iota(jnp.int32, (C, LANES), 0)
    valid = rows < nvalid
    xs = ab + dtb_ref[0]
    softplus = jnp.maximum(xs, 0.0) + jnp.log(1.0 + jnp.exp(-jnp.abs(xs)))
    g = jnp.where(valid, -jnp.exp(alog_ref[0]) * softplus, 0.0)
    beta = jnp.where(valid, _sigmoid(ab), 0.0)
    r = lax.broadcasted_iota(jnp.int32, (C, C), 0)
    cc = lax.broadcasted_iota(jnp.int32, (C, C), 1)
    incl = r >= cc
    strict = r > cc
    gcum = jnp.dot(jnp.where(incl, 1.0, 0.0), g, precision=_HI, preferred_element_type=F32)
    gcum_t = gcum.T

    qn = [qs[:, sl] * lax.rsqrt(jnp.sum(qs[:, sl] * qs[:, sl], axis=-1, keepdims=True) + NORM_EPS)
          * (dk ** -0.5) for sl in sls]
    kn = [ks[:, sl] * lax.rsqrt(jnp.sum(ks[:, sl] * ks[:, sl], axis=-1, keepdims=True) + NORM_EPS)
          for sl in sls]
    bcol = [beta[:, hg + h:hg + h + 1] for h in heads]
    gcol = [gcum[:, h:h + 1] for h in heads]
    glast = [gcum[C - 1:C, h:h + 1] for h in heads]
    eg = [jnp.exp(gcol[h]) for h in heads]
    decay = [jnp.where(incl, jnp.exp(jnp.where(incl, gcol[h] - gcum_t[h:h + 1, :], 0.0)), 0.0) for h in heads]
    kb = [kn[h] * bcol[h] for h in heads]
    knb = [kn[h].astype(BF16) for h in heads]
    a_mat = [jnp.where(strict, _dot_nt(kb[h].astype(BF16), knb[h]) * decay[h], 0.0) for h in heads]

    gain = gain_ref[...]
    for h in heads:
        s_ref[h] = s_ref[h] * eg_s[h:h + 1, :] + ds_s[h]
    for h in heads:
        o = os_s[h]
        z = z_ref[:, sls[h]]
        ms = jnp.mean(o * o, axis=-1, keepdims=True)
        o_ref[:, sls[h]] = (o * lax.rsqrt(ms + NORM_EPS) * gain * (z * _sigmoid(z))).astype(o_ref.dtype)

    t_mat = _tri_inverse_many(a_mat)
    rhs = [jnp.concatenate([vs[:, sls[h]] * bcol[h], kb[h] * eg[h]], axis=1).astype(BF16) for h in heads]
    tv = [_dot(t_mat[h].astype(BF16), rhs[h]) for h in heads]
    qk = [_dot_nt(qn[h].astype(BF16), knb[h]) * decay[h] for h in heads]
    for h in heads:
        val_s[:, sls[h]] = tv[h][:, :dk]
        kcd_s[:, sls[h]] = tv[h][:, dk:].astype(BF16)
        qdec_s[:, sls[h]] = (qn[h] * eg[h]).astype(BF16)
        kdec_s[:, sls[h]] = (kn[h] * jnp.exp(glast[h] - gcol[h])).astype(BF16)
        eg_s[h:h + 1, :] = jnp.broadcast_to(jnp.exp(glast[h]), (1, LANES))
    qk_s[...] = jnp.concatenate(qk, axis=1)

    done = (cj == ncj - 1) & (t > 0)

    @pl.when(done & (seq_j < n_new))
    def _():
        snew_ref[0] = s_ref[...]

    @pl.when(done & (seq_j >= n_new))
    def _():
        srun_ref[0] = s_ref[...]


def _gdn_mixer(proj, gates, conv0, s0, conv_w, a_log, dt_bias, o_gain, *, heads, dk, hg, n_p, cps, cs, l_valid):
    m = proj.shape[0]
    C = CHUNK
    w = heads * dk
    n_hg = heads // hg
    gw = hg * dk
    n_items = m // C
    n_new, n_run = n_p // cps, s0.shape[0]
    width = conv_w.shape[0]
    info = dict(n_p=n_p, cps=cps, cs=cs, l_valid=l_valid)

    def seq_of(i):
        return _item_info(i, n_p, cps, cs, l_valid)[0]

    prep = lambda t: jnp.minimum(t, n_items - 1)
    scan = lambda t: jnp.maximum(t - 1, 0)
    cur = lambda off: pl.BlockSpec((C, gw), lambda g, t: (prep(t), off * n_hg + g))
    prev = lambda off: pl.BlockSpec(
        (SUBLANES, gw), lambda g, t: (jnp.maximum(prep(t) * (C // SUBLANES) - 1, 0), off * n_hg + g))
    init = lambda off: pl.BlockSpec((1, SUBLANES, gw), lambda g, t: (seq_of(prep(t)), 0, off * n_hg + g))
    wspec = lambda off: pl.BlockSpec((width, gw), lambda g, t: (0, off * n_hg + g))
    gate_vec = pl.BlockSpec((1, 1, LANES), lambda g, t: (g, 0, 0))
    state_new = pl.BlockSpec((1, hg, dk, dk), lambda g, t: (jnp.minimum(seq_of(scan(t)), n_new - 1), g, 0, 0))
    state_run = pl.BlockSpec((1, hg, dk, dk), lambda g, t: (jnp.maximum(seq_of(scan(t)) - n_new, 0), g, 0, 0))

    alog_g = jnp.pad(a_log.reshape(n_hg, 1, hg).astype(F32), ((0, 0), (0, 0), (0, LANES - hg)))
    dtb_g = jnp.pad(dt_bias.reshape(n_hg, 1, hg).astype(F32), ((0, 0), (0, 0), (0, LANES - hg)))

    return pl.pallas_call(
        functools.partial(_gdn_kernel, hg=hg, dk=dk, width=width, n_items=n_items, **info),
        out_shape=(jax.ShapeDtypeStruct((m, w), BF16),
                   jax.ShapeDtypeStruct((n_new, heads, dk, dk), F32),
                   jax.ShapeDtypeStruct((n_run, heads, dk, dk), F32)),
        grid=(n_hg, n_items + 1),
        in_specs=[cur(0), cur(1), cur(2), prev(0), prev(1), prev(2), init(0), init(1), init(2),
                  wspec(0), wspec(1), wspec(2),
                  pl.BlockSpec((C, LANES), lambda g, t: (prep(t), g)),
                  gate_vec, gate_vec,
                  pl.BlockSpec((C, gw), lambda g, t: (scan(t), 3 * n_hg + g)),
                  pl.BlockSpec((1, dk), lambda g, t: (0, 0)),
                  state_run],
        out_specs=(pl.BlockSpec((C, gw), lambda g, t: (scan(t), g)), state_new, state_run),
        scratch_shapes=[pltpu.VMEM((SUBLANES + C, gw), F32)] * 3
        + [pltpu.VMEM((C, gw), F32), pltpu.VMEM((C, gw), BF16), pltpu.VMEM((C, gw), BF16),
           pltpu.VMEM((C, gw), BF16), pltpu.VMEM((C, hg * C), F32), pltpu.VMEM((hg, LANES), F32),
           pltpu.VMEM((hg, dk, dk), F32)],
        compiler_params=_params(("parallel", "arbitrary")),
        name="gdn_mixer",
    )(proj, proj, proj, proj, proj, proj, conv0, conv0, conv0, conv_w, conv_w, conv_w,
      gates, alog_g, dtb_g, proj, o_gain.reshape(1, dk).astype(F32), s0)


def _split_half_layout(x, half):
    pad = [(0, 0)] * (x.ndim - 1) + [(0, LANES // 2 - half)]
    return jnp.concatenate([jnp.pad(x[..., :half], pad), jnp.pad(x[..., half:], pad)], axis=-1)


def _rope_tables(pos, half):
    inv_freq = 1.0 / (ROPE_THETA ** (jnp.arange(half, dtype=F32) / half))
    ang = pos[:, None] * inv_freq[None, :]
    cos, sin = jnp.cos(ang), jnp.sin(ang)
    cos_t = _split_half_layout(jnp.concatenate([cos, cos], -1), half)
    sin_t = _split_half_layout(jnp.concatenate([-sin, sin], -1), half)
    return cos_t, sin_t


def kernel(x_prompt, x_sample, state_conv, state_delta, cache_ckv, cache_krope, meta_tokens,
           a_w_in, a_conv_w, a_a_log, a_dt_bias, a_o_gain, a_w_o,
           b_w_dq, b_q_gain, b_w_uq, b_w_o,
           kv_w_dkv, kv_gain, kv_w_uk, kv_w_uv,
           mlp_w_up, mlp_w_down, ln_gain, ln_bias):
    B, seq, D = x_prompt.shape
    DB, ts, _ = x_sample.shape
    n_meta = meta_tokens.shape[0]
    depth = ln_gain.shape[0]
    n_a = a_w_in.shape[0]
    H = a_a_log.shape[1]
    dk = a_o_gain.shape[1]
    W = H * dk
    width = a_conv_w.shape[1]
    past = cache_ckv.shape[1]
    KV, MH, nope = kv_w_uk.shape
    vd = kv_w_uv.shape[2]
    rope = cache_krope.shape[2]
    half = rope // 2
    alpha = (2 * depth) ** 0.25
    scale = (nope + rope) ** -0.5
    assert ts % CHUNK == 0 and ts >= width - 1 and half < LANES // 2
    assert nope % LANES == 0 and vd % LANES == 0 and KV % LANES == 0 and dk % LANES == 0

    L = n_meta + seq
    LP = _round_up(L, LANES)
    n_prompt_rows = B * LP
    M = n_prompt_rows + DB * ts
    cps, cs = LP // CHUNK, ts // CHUNK
    n_p = B * cps
    hg = min(8, H)
    n_hg = H // hg

    meta = jnp.broadcast_to(meta_tokens.astype(F32)[None], (B, n_meta, D))
    xp = jnp.concatenate([meta, x_prompt, jnp.zeros((B, LP - L, D), F32)], axis=1)
    x = jnp.concatenate([xp.reshape(n_prompt_rows, D), x_sample.reshape(DB * ts, D)], axis=0)
    xb = x.astype(BF16)

    pos = jnp.concatenate([jnp.tile(jnp.arange(LP, dtype=F32), B),
                           jnp.tile(past + jnp.arange(ts, dtype=F32), DB)])
    cos_t, sin_t = _rope_tables(pos, half)

    w_in_t = jnp.swapaxes(a_w_in, 1, 2)

    conv0 = jnp.concatenate([jnp.zeros((n_a, B, width - 1, 3 * W), F32), state_conv.astype(F32)], axis=1)
    conv0 = jnp.pad(conv0, ((0, 0), (0, 0), (SUBLANES - (width - 1), 0), (0, 0)))
    s0 = state_delta.astype(F32)

    new_conv_p, new_conv_s, new_s = [], [], []
    k_slabs = None
    lat = None
    for layer in range(depth):
        if layer < n_a:
            w_ab = a_w_in[layer, :, 4 * W:]
            gate_w = jnp.concatenate(
                [w_ab[:, :H].reshape(D, n_hg, hg), w_ab[:, H:].reshape(D, n_hg, hg),
                 jnp.zeros((D, n_hg, LANES - 2 * hg), w_ab.dtype)], axis=-1).reshape(D, n_hg * LANES)
            proj = _matmul(xb, w_in_t, F32, layer=layer, n_use=4 * W, b_transposed=True)
            gates = _matmul(xb, gate_w.astype(BF16), F32, bm=328)
            o, s_new, s_run = _gdn_mixer(proj, gates, conv0[layer], s0[layer], a_conv_w[layer].astype(F32),
                                         a_a_log[layer], a_dt_bias[layer], a_o_gain[layer],
                                         heads=H, dk=dk, hg=hg, n_p=n_p, cps=cps, cs=cs, l_valid=L)
            h = _matmul(o, a_w_o, BF16, layer=layer)
            new_conv_p.append(jnp.stack(
                [proj[b * LP + L - (width - 1):b * LP + L, :3 * W] for b in range(B)]))
            new_conv_s.append(jnp.stack(
                [proj[n_prompt_rows + ts - (width - 1) + t::ts, :3 * W] for t in range(width - 1)], axis=1))
            new_s.append((s_new, s_run))
        else:
            j = layer - n_a
            ql = _matmul(xb, b_w_dq[j].astype(BF16), BF16, epilogue="rms", gain=b_q_gain[j])
            w_uq = b_w_uq[j].reshape(-1, MH, nope + rope)
            w_uq = jnp.concatenate([w_uq[..., :nope], _split_half_layout(w_uq[..., nope:], half)], axis=-1)
            q = _matmul(ql, w_uq.reshape(-1, MH * (nope + LANES)).astype(BF16), BF16)
            kp, vp, ks, ls_t, vs, lkp, lks = k_slabs
            o = _attention_causal(q, cos_t, sin_t, kp, vp, nb=B, lq=LP, lk_pad=lkp, heads=MH, nope=nope,
                                  vd=vd, bq=_pick_block(LP, 1408, LANES), bk=512, scale=scale, n_meta=n_meta)
            o = _attention_full(q, cos_t, sin_t, ks, ls_t, vs, o, nb=DB, lq=ts, lk_pad=lks, row0=n_prompt_rows,
                                heads=MH, hb=min(4, MH), nope=nope, vd=vd, kv=KV, scale=scale,
                                n_keys=past + ts)
            h = _matmul(o, b_w_o, BF16, layer=j)

        x, xb = _deepnorm(x, h, ln_gain[layer, 0], ln_bias[layer, 0], alpha)
        hid = _matmul(xb, mlp_w_up, BF16, layer=layer, epilogue="relu2")
        h = _matmul(hid, mlp_w_down, BF16, layer=layer, bn=1024, bk=2048)
        x, xb = _deepnorm(x, h, ln_gain[layer, 1], ln_bias[layer, 1], alpha)

        if layer == n_a - 1:
            w_dkv = jnp.concatenate([kv_w_dkv[:, :KV], _split_half_layout(kv_w_dkv[:, KV:], half)], axis=1)
            lat = _kv_post(_matmul(xb, w_dkv.astype(BF16), F32, bm=328, bn=640), kv_gain.astype(F32),
                           cos_t, sin_t, KV)
            lane_ids = jnp.arange(LANES)
            eye = jnp.where((lane_ids[:, None] == lane_ids[None, :]) & (lane_ids[:, None] != half), 1.0, 0.0)
            ones_col = jnp.where((lane_ids[:, None] == half) & (lane_ids[None, :] == 0), 1.0, 0.0)
            w_k = jnp.concatenate(
                [jnp.concatenate([kv_w_uk.astype(F32), jnp.zeros((KV, MH, LANES), F32)], axis=-1),
                 jnp.concatenate([jnp.zeros((LANES, MH, nope), F32),
                                  jnp.broadcast_to(eye[:, None, :], (LANES, MH, LANES))], axis=-1)],
                axis=0).reshape(KV + LANES, MH * (nope + LANES)).astype(BF16)
            w_v = jnp.concatenate(
                [jnp.concatenate([kv_w_uv.astype(F32), jnp.zeros((KV, MH, LANES), F32)], axis=-1),
                 jnp.concatenate([jnp.zeros((LANES, MH, vd), F32),
                                  jnp.broadcast_to(ones_col[:, None, :], (LANES, MH, LANES))], axis=-1)],
                axis=0).reshape(KV + LANES, MH * (vd + LANES)).astype(BF16)
            one_lane = jnp.where(jnp.arange(KV + LANES) == KV + half, 1.0, 0.0)
            lkp = _round_up(L, 512)
            lat_p = lat[:n_prompt_rows].reshape(B, LP, KV + LANES)
            if lkp >= LP:
                lat_p = jnp.pad(lat_p, ((0, 0), (0, lkp - LP), (0, 0)))
            else:
                lat_p = lat_p[:, :lkp]
            lat_p = (lat_p.reshape(B * lkp, KV + LANES) + one_lane).astype(BF16)
            lks = _round_up(past + ts, LANES)
            cache = jnp.concatenate([cache_ckv.astype(F32), _split_half_layout(cache_krope.astype(F32), half)],
                                    axis=-1)
            lat_s = jnp.concatenate([cache, lat[n_prompt_rows:].reshape(DB, ts, KV + LANES),
                                     jnp.zeros((DB, lks - past - ts, KV + LANES), F32)], axis=1)
            lat_s = (lat_s.reshape(DB * lks, KV + LANES) + one_lane).astype(BF16)
            w_kn_t = jnp.transpose(kv_w_uk, (1, 2, 0)).reshape(MH * nope, KV).astype(BF16)
            lat_s_t = lat_s.T
            k_slabs = (_matmul(w_k.T, lat_p.T, BF16, bm=1024), _matmul(lat_p, w_v, BF16, bm=1024),
                       _matmul(w_kn_t, lat_s_t, BF16, bm=1024), lat_s_t,
                       _matmul(lat_s[:, :KV], kv_w_uv.reshape(KV, MH * vd).astype(BF16), BF16, bm=1024),
                       lkp, lks)

    def unsplit(r):
        return jnp.concatenate([r[..., :half], r[..., LANES // 2:LANES // 2 + half]], axis=-1)

    lat_p = lat[:n_prompt_rows].reshape(B, LP, KV + LANES)[:, :L]
    lat_s = lat[n_prompt_rows:].reshape(DB, ts, KV + LANES)
    y_prompt = jnp.stack([x[b * LP + n_meta:b * LP + L] for b in range(B)])
    y_sample = x[n_prompt_rows:].reshape(DB, ts, D)
    return (y_prompt, y_sample,
            jnp.stack(new_conv_p), jnp.stack([s[0] for s in new_s]),
            lat_p[..., :KV], unsplit(lat_p[..., KV:]),
            jnp.stack(new_conv_s), jnp.stack([s[1] for s in new_s]),
            lat_s[..., :KV], unsplit(lat_s[..., KV:]))
```

```python
import functools
import math

import jax
import jax.numpy as jnp
from jax import lax
from jax.experimental import pallas as pl
from jax.experimental.pallas import tpu as pltpu

CHUNK = 64
ROPE_THETA = 10000.0
NORM_EPS = 1e-6
NEG_INF = -1e30
LANES = 128
SUBLANES = 8
VMEM_LIMIT_BYTES = 56 * 1024 * 1024
_SPLIT_LEVELS = 2
_CHUNK_SHIFT = CHUNK.bit_length() - 1
assert 1 << _CHUNK_SHIFT == CHUNK

F32 = jnp.float32
BF16 = jnp.bfloat16
_HI = lax.Precision.HIGHEST


def _round_up(x, m):
    return -(-x // m) * m


def _pick_block(dim, target, align):
    best = None
    for d in range(align, min(dim, target) + 1, align):
        if dim % d == 0:
            best = d
    return best if best is not None else dim


def _dot(a, b):
    return jnp.dot(a, b, preferred_element_type=F32)


def _dot_nt(a, b):
    return lax.dot_general(a, b, (((1,), (1,)), ((), ())), preferred_element_type=F32)


def _dot_tn(a, b):
    return lax.dot_general(a, b, (((0,), (0,)), ((), ())), preferred_element_type=F32)


def _sigmoid(x):
    return 1.0 / (1.0 + jnp.exp(-x))


def _params(sem):
    return pltpu.CompilerParams(dimension_semantics=sem, vmem_limit_bytes=VMEM_LIMIT_BYTES)


def _mm_kernel(*refs, nk, epilogue, b_transposed):
    if epilogue == "rms":
        a_ref, b_ref, g_ref, o_ref = refs[:4]
        rest = refs[4:]
    else:
        a_ref, b_ref, o_ref = refs[:3]
        g_ref = None
        rest = refs[3:]

    def finish(acc):
        if epilogue == "relu2":
            r = jnp.maximum(acc, 0.0)
            acc = r * r
        elif epilogue == "rms":
            ms = jnp.mean(acc * acc, axis=-1, keepdims=True)
            acc = acc * lax.rsqrt(ms + NORM_EPS) * g_ref[...]
        o_ref[...] = acc.astype(o_ref.dtype)

    def product():
        b = b_ref[...].astype(BF16)
        return _dot_nt(a_ref[...], b) if b_transposed else _dot(a_ref[...], b)

    if nk == 1:
        finish(product())
    else:
        acc_ref = rest[0]
        k = pl.program_id(2)

        @pl.when(k == 0)
        def _():
            acc_ref[...] = jnp.zeros_like(acc_ref)

        acc_ref[...] += product()

        @pl.when(k == nk - 1)
        def _():
            finish(acc_ref[...])


def _matmul(a, b, out_dtype, *, layer=None, n_use=None, b_transposed=False, epilogue="none", gain=None,
            bm=1312, bn=512, bk=4096):
    m, kdim = a.shape
    n = n_use if n_use is not None else b.shape[-2 if b_transposed else -1]
    bm = _pick_block(m, bm, 16)
    bn = n if epilogue == "rms" else _pick_block(n, bn, LANES)
    bk = _pick_block(kdim, bk, LANES)
    nk = kdim // bk
    a_mode = dict(pipeline_mode=pl.Buffered(1)) if nk == 1 and n // bn >= 4 else {}
    b_blk = (bn, bk) if b_transposed else (bk, bn)
    b_idx = (lambda k, j: (j, k)) if b_transposed else (lambda k, j: (k, j))
    if b.ndim == 3:
        b_spec = pl.BlockSpec((None,) + b_blk, lambda i, j, k: (layer,) + b_idx(k, j))
    else:
        b_spec = pl.BlockSpec(b_blk, lambda i, j, k: b_idx(k, j))
    in_specs = [pl.BlockSpec((bm, bk), lambda i, j, k: (i, k), **a_mode), b_spec]
    args = [a, b]
    if epilogue == "rms":
        in_specs.append(pl.BlockSpec((1, bn), lambda i, j, k: (0, j)))
        args.append(gain.reshape(1, n).astype(F32))
    scratch = [pltpu.VMEM((bm, bn), F32)] if nk > 1 else []
    return pl.pallas_call(
        functools.partial(_mm_kernel, nk=nk, epilogue=epilogue, b_transposed=b_transposed),
        out_shape=jax.ShapeDtypeStruct((m, n), out_dtype),
        grid=(m // bm, n // bn, nk),
        in_specs=in_specs,
        out_specs=pl.BlockSpec((bm, bn), lambda i, j, k: (i, j)),
        scratch_shapes=scratch,
        compiler_params=_params(("parallel", "parallel", "arbitrary")),
        name="matmul_" + epilogue,
    )(*args)


def _ln_kernel(x_ref, h_ref, g_ref, b_ref, of_ref, ob_ref, *, alpha):
    y = alpha * x_ref[...] + h_ref[...].astype(F32)
    mu = jnp.mean(y, axis=-1, keepdims=True)
    d = y - mu
    var = jnp.mean(d * d, axis=-1, keepdims=True)
    out = d * lax.rsqrt(var + NORM_EPS) * g_ref[...] + b_ref[...]
    of_ref[...] = out
    ob_ref[...] = out.astype(BF16)


def _deepnorm(x, h, gain, bias, alpha):
    m, d = x.shape
    br = _pick_block(m, 256, 16)
    row = pl.BlockSpec((br, d), lambda i: (i, 0))
    vec = pl.BlockSpec((1, d), lambda i: (0, 0))
    return pl.pallas_call(
        functools.partial(_ln_kernel, alpha=alpha),
        out_shape=(jax.ShapeDtypeStruct((m, d), F32), jax.ShapeDtypeStruct((m, d), BF16)),
        grid=(m // br,),
        in_specs=[row, row, vec, vec],
        out_specs=(row, row),
        compiler_params=_params(("parallel",)),
        name="deepnorm",
    )(x, h, gain.reshape(1, d), bias.reshape(1, d))


def _kv_post_kernel(lat_ref, g_ref, cos_ref, sin_ref, o_ref, *, kv):
    c = lat_ref[:, :kv]
    ms = jnp.mean(c * c, axis=-1, keepdims=True)
    o_ref[:, :kv] = c * lax.rsqrt(ms + NORM_EPS) * g_ref[...]
    r = lat_ref[:, kv:]
    o_ref[:, kv:] = r * cos_ref[...] + pltpu.roll(r, LANES // 2, axis=1) * sin_ref[...]


def _kv_post(lat, gain, cos_t, sin_t, kv):
    m, w = lat.shape
    br = _pick_block(m, 256, 8)
    return pl.pallas_call(
        functools.partial(_kv_post_kernel, kv=kv),
        out_shape=jax.ShapeDtypeStruct((m, w), F32),
        grid=(m // br,),
        in_specs=[pl.BlockSpec((br, w), lambda i: (i, 0)),
                  pl.BlockSpec((1, kv), lambda i: (0, 0)),
                  pl.BlockSpec((br, LANES), lambda i: (i, 0)),
                  pl.BlockSpec((br, LANES), lambda i: (i, 0))],
        out_specs=pl.BlockSpec((br, w), lambda i: (i, 0)),
        compiler_params=_params(("parallel",)),
        name="kv_post",
    )(lat, gain.reshape(1, kv), cos_t, sin_t)


def _rope_query(q, cos, sin, nope, scale):
    q = q.astype(F32)
    qr = q[:, nope:]
    qr = qr * cos + pltpu.roll(qr, LANES // 2, axis=1) * sin
    return (jnp.concatenate([q[:, :nope], qr], axis=1) * (scale * math.log2(math.e))).astype(BF16)


def _attn_causal_kernel(q_ref, cos_ref, sin_ref, kt_ref, v_ref, o_ref,
                        q_scr, s_scr, p_scr, m_scr, a_scr, acc_scr, *, bq, bk, rt, n_groups, nope, vd, scale,
                        n_meta, lk_pad):
    qi = pl.program_id(2)
    tiles = [slice(t * rt, (t + 1) * rt) for t in range(bq // rt)]
    lane_chunks = [slice(c * LANES, (c + 1) * LANES) for c in range(bk // LANES)]

    def visible_end(r):
        chunk_id = lax.shift_right_arithmetic(r - n_meta, _CHUNK_SHIFT) + 1
        return jnp.where(r < n_meta, n_meta, chunk_id * CHUNK + n_meta)

    n_full = visible_end(qi * bq) // bk
    nkb = (jnp.minimum(visible_end(qi * bq + bq - 1), lk_pad) + bk - 1) // bk

    for rows in tiles:
        q_scr[rows, :] = _rope_query(q_ref[rows, :], cos_ref[rows, :], sin_ref[rows, :], nope, scale)
    m_scr[...] = jnp.full(m_scr.shape, NEG_INF, F32)
    acc_scr[...] = jnp.zeros(acc_scr.shape, F32)

    gsz = bq // n_groups
    groups = [slice(g * gsz, (g + 1) * gsz) for g in range(n_groups)]
    tiles_per_group = gsz // rt
    last_block = lk_pad // bk - 1
    last_g = n_groups - 1

    def block_start(j):
        return pl.multiple_of(j * bk, bk)

    def scores(g, j):
        s_scr[groups[g], :] = _dot(q_scr[groups[g], :], kt_ref[:, pl.ds(block_start(j), bk)])

    def values(g, j):
        rows = groups[g]
        a = a_scr[rows, :]
        acc_scr[rows, :] = (jnp.concatenate([a] * (acc_scr.shape[1] // LANES), axis=1) * acc_scr[rows, :]
                            + _dot(p_scr[rows, :], v_ref[pl.ds(block_start(j), bk), :]))

    p_scr[groups[last_g], :] = jnp.zeros((gsz, bk), BF16)
    a_scr[groups[last_g], :] = jnp.ones((gsz, LANES), F32)
    scores(0, 0)

    def step(j, carry, masked):
        start = block_start(j)

        def softmax(g):
            for t in range(g * tiles_per_group, (g + 1) * tiles_per_group):
                rows = tiles[t]
                s = s_scr[rows, :]
                if masked:
                    kpos = start + lax.broadcasted_iota(jnp.int32, (1, bk), 1)
                    vis = visible_end(qi * bq + t * rt + lax.broadcasted_iota(jnp.int32, (rt, 1), 0))
                    s = jnp.where(kpos < vis, s, NEG_INF)
                m_old = m_scr[rows, :]
                m_new = jnp.maximum(m_old, jnp.max(s, axis=-1, keepdims=True))
                for lc in lane_chunks:
                    p_scr[rows, lc] = jnp.exp2(s[:, lc] - m_new).astype(BF16)
                a_scr[rows, :] = jnp.exp2(m_old - m_new)
                m_scr[rows, :] = m_new

        values(last_g, jnp.maximum(j - 1, 0))
        for g in range(n_groups):
            if g < last_g:
                scores(g + 1, j)
            else:
                scores(0, jnp.minimum(j + 1, last_block))
            softmax(g)
            if g < last_g:
                values(g, j)
        return carry

    lax.fori_loop(0, n_full, functools.partial(step, masked=False), 0)
    lax.fori_loop(n_full, nkb, functools.partial(step, masked=True), 0)
    values(last_g, nkb - 1)
    for rows in tiles:
        o_ref[rows, :] = (acc_scr[rows, :vd] / acc_scr[rows, vd:vd + 1]).astype(o_ref.dtype)


def _attn_full_kernel(q_ref, cos_ref, sin_ref, knt_ref, krt_ref, v_ref, o_ref, *, hb, nope, vd, scale, n_keys):
    dqk = nope + LANES
    lq = q_ref.shape[0]
    heads = range(hb)
    cos, sin = cos_ref[...], sin_ref[...]
    kpos = lax.broadcasted_iota(jnp.int32, (1, knt_ref.shape[1]), 1)
    qs = [_rope_query(q_ref[:, h * dqk:(h + 1) * dqk], cos, sin, nope, scale) for h in heads]
    s_rope = _dot(jnp.concatenate([q[:, nope:] for q in qs], axis=0), krt_ref[...])
    ss = [jnp.where(kpos < n_keys,
                    _dot(qs[h][:, :nope], knt_ref[h * nope:(h + 1) * nope, :]) + s_rope[h * lq:(h + 1) * lq],
                    NEG_INF) for h in heads]
    es = [jnp.exp2(s - jnp.max(s, axis=-1, keepdims=True)) for s in ss]
    ls = [jnp.sum(e, axis=-1, keepdims=True) for e in es]
    pv = [_dot(es[h].astype(BF16), v_ref[:, h * vd:(h + 1) * vd]) for h in heads]
    for h in heads:
        o_ref[:, h * vd:(h + 1) * vd] = (pv[h] / ls[h]).astype(o_ref.dtype)


def _attention_causal(q, cos_t, sin_t, kt, vf, *, nb, lq, lk_pad, heads, nope, vd, bq, bk, scale, n_meta):
    dqk = nope + LANES
    nq = lq // bq
    rt = 32 if bq % 32 == 0 else bq
    n_groups = 4 if bq % (4 * rt) == 0 else 2
    assert bq % (n_groups * rt) == 0
    return pl.pallas_call(
        functools.partial(_attn_causal_kernel, bq=bq, bk=bk, rt=rt, n_groups=n_groups, nope=nope, vd=vd,
                          scale=scale,
                          n_meta=n_meta, lk_pad=lk_pad),
        out_shape=jax.ShapeDtypeStruct((q.shape[0], heads * vd), BF16),
        grid=(nb, heads, nq),
        in_specs=[pl.BlockSpec((bq, dqk), lambda b, h, i: (b * nq + i, h)),
                  pl.BlockSpec((bq, LANES), lambda b, h, i: (b * nq + i, 0)),
                  pl.BlockSpec((bq, LANES), lambda b, h, i: (b * nq + i, 0)),
                  pl.BlockSpec((dqk, lk_pad), lambda b, h, i: (h, b)),
                  pl.BlockSpec((lk_pad, vd + LANES), lambda b, h, i: (b, h))],
        out_specs=pl.BlockSpec((bq, vd), lambda b, h, i: (b * nq + i, h)),
        scratch_shapes=[pltpu.VMEM((bq, dqk), BF16), pltpu.VMEM((bq, bk), F32), pltpu.VMEM((bq, bk), BF16),
                        pltpu.VMEM((bq, LANES), F32), pltpu.VMEM((bq, LANES), F32),
                        pltpu.VMEM((bq, vd + LANES), F32)],
        compiler_params=_params(("parallel", "parallel", "arbitrary")),
        name="mla_attention_causal",
    )(q, cos_t, sin_t, kt, vf)


def _attention_full(q, cos_t, sin_t, knt, lat_t, vf, o_all, *, nb, lq, lk_pad, row0, heads, hb, nope, vd, kv,
                    scale, n_keys):
    dqk = nope + LANES
    blk0 = row0 // lq

    def body(q_ref, cos_ref, sin_ref, knt_ref, krt_ref, v_ref, o_all_ref, o_ref):
        _attn_full_kernel(q_ref, cos_ref, sin_ref, knt_ref, krt_ref, v_ref, o_ref,
                          hb=hb, nope=nope, vd=vd, scale=scale, n_keys=n_keys)

    return pl.pallas_call(
        body,
        out_shape=jax.ShapeDtypeStruct(o_all.shape, o_all.dtype),
        grid=(nb, heads // hb),
        in_specs=[pl.BlockSpec((lq, hb * dqk), lambda b, g: (blk0 + b, g)),
                  pl.BlockSpec((lq, LANES), lambda b, g: (blk0 + b, 0)),
                  pl.BlockSpec((lq, LANES), lambda b, g: (blk0 + b, 0)),
                  pl.BlockSpec((hb * nope, lk_pad), lambda b, g: (g, b)),
                  pl.BlockSpec((LANES, lk_pad), lambda b, g: (kv // LANES, b)),
                  pl.BlockSpec((lk_pad, hb * vd), lambda b, g: (b, g)),
                  pl.BlockSpec(memory_space=pl.ANY)],
        out_specs=pl.BlockSpec((lq, hb * vd), lambda b, g: (blk0 + b, g)),
        input_output_aliases={6: 0},
        compiler_params=_params(("parallel", "parallel")),
        name="mla_attention_full",
    )(q, cos_t, sin_t, knt, lat_t, vf, o_all)


def _item_info(i, n_p, cps, cs, l_valid):
    is_p = i < n_p
    j = i - n_p
    seq = jnp.where(is_p, i // cps, n_p // cps + j // cs)
    c = jnp.where(is_p, i % cps, j % cs)
    nc = jnp.where(is_p, cps, cs)
    nvalid = jnp.where(is_p, jnp.clip(l_valid - c * CHUNK, 0, CHUNK), CHUNK)
    return seq, c, nc, nvalid


def _row_sums_of_squares(xs):
    rows, n = xs[0].shape
    parts = []
    for x in xs:
        sq = x * x
        hi = sq.astype(BF16)
        parts.append(jnp.concatenate([hi, (sq - hi.astype(F32)).astype(BF16)], axis=1))
    tot = _dot(jnp.concatenate(parts, axis=0), jnp.ones((2 * n, n), BF16))
    return [tot[i * rows:(i + 1) * rows] for i in range(len(xs))]


def _tri_inverse_many(mats):
    n = mats[0].shape[0]
    r = lax.broadcasted_iota(jnp.int32, (n, n), 0)
    c = lax.broadcasted_iota(jnp.int32, (n, n), 1)
    eye = jnp.where(r == c, 1.0, 0.0)
    zero = jnp.zeros((n, n), F32)

    def split(x):
        hi = x.astype(BF16).astype(F32)
        return hi, x - hi

    def lhs_of(x):
        hi, lo = split(x)
        return jnp.concatenate([hi, lo, hi, zero], axis=1).astype(BF16)

    def rhs_of(x):
        hi, lo = split(x)
        return jnp.concatenate([hi, hi, lo, zero], axis=0).astype(BF16)

    ps = [eye - a for a in mats]
    aks = list(mats)
    rhs = [rhs_of(a) for a in mats]
    power, level = 2, 0
    while power < n:
        if level < _SPLIT_LEVELS:
            aks = [_dot(lhs_of(ak), rk) for ak, rk in zip(aks, rhs)]
            rhs = [rhs_of(ak) for ak in aks]
            ps = [p + _dot(lhs_of(p), rk) for p, rk in zip(ps, rhs)]
        else:
            akb = [ak.astype(BF16) for ak in aks]
            aks = [_dot(b, b) for b in akb]
            ps = [p + _dot(p.astype(BF16), ak.astype(BF16)) for p, ak in zip(ps, aks)]
        power *= 2
        level += 1
    return ps


def _gdn_kernel(qc_ref, kc_ref, vc_ref, qp_ref, kp_ref, vp_ref, q0_ref, k0_ref, v0_ref,
                wq_ref, wk_ref, wv_ref, ab_ref, alog_ref, dtb_ref, shift_ref, z_ref, gain_ref, s0_ref,
                o_ref, snew_ref, srun_ref,
                val_s, kcd_s, qdec_s, kdec_s, qk_s, eg_s, s_ref,
                *, hg, dk, n_items, n_p, cps, cs, l_valid, width):
    C = CHUNK
    t = pl.program_id(1)
    i = jnp.minimum(t, n_items - 1)
    j = jnp.maximum(t - 1, 0)
    _, c, _, nvalid = _item_info(i, n_p, cps, cs, l_valid)
    _, cj, ncj, _ = _item_info(j, n_p, cps, cs, l_valid)
    first = c == 0
    heads = range(hg)
    sls = [slice(h * dk, (h + 1) * dk) for h in heads]

    @pl.when(t == 0)
    def _():
        for ref in (val_s, kcd_s, qdec_s, kdec_s, qk_s, eg_s):
            ref[...] = jnp.zeros(ref.shape, ref.dtype)

    n_new = n_p // cps
    seq_j = _item_info(j, n_p, cps, cs, l_valid)[0]

    @pl.when(cj == 0)
    def _():
        s_ref[...] = jnp.where(seq_j < n_new, 0.0, s0_ref[0])

    sb = [s_ref[h].astype(BF16) for h in heads]
    ks_s = [_dot(kcd_s[:, sls[h]], sb[h]) for h in heads]
    qs_s = [_dot(qdec_s[:, sls[h]], sb[h]) for h in heads]

    halo = SUBLANES
    kpad = shift_ref.shape[1] // 2
    shift = shift_ref[...]

    def conv_silu(cur_ref, prev_ref, init_ref, w_ref):
        cur = cur_ref[...]
        rows = jnp.concatenate([jnp.where(first, init_ref[0], prev_ref[...]), cur,
                                jnp.zeros((kpad - halo - C, cur.shape[1]), F32)], axis=0)
        hi = rows.astype(BF16)
        lo = (rows - hi.astype(F32)).astype(BF16)
        shifted = _dot(shift, jnp.concatenate([hi, lo], axis=0))
        y = cur * w_ref[width - 1:width, :]
        for t in range(width - 1):
            y = y + shifted[t * C:(t + 1) * C] * w_ref[t:t + 1, :]
        return y * _sigmoid(y)

    qs = conv_silu(qc_ref, qp_ref, q0_ref, wq_ref)
    ks = conv_silu(kc_ref, kp_ref, k0_ref, wk_ref)
    vs = conv_silu(vc_ref, vp_ref, v0_ref, wv_ref)

    ub = [(val_s[:, sls[h]] - ks_s[h]).astype(BF16) for h in heads]
    ds_s = [_dot_tn(kdec_s[:, sls[h]], ub[h]) for h in heads]
    os_s = [qs_s[h] + _dot(qk_s[:, h * C:(h + 1) * C].astype(BF16), ub[h]) for h in heads]

    ab = ab_ref[...]
    rows = lax.broadcasted_iota(jnp.int32, (C, LANES), 0)
    valid = rows < nvalid
    xs = ab + dtb_ref[0]
    softplus = jnp.maximum(xs, 0.0) + jnp.log(1.0 + jnp.exp(-jnp.abs(xs)))
    g = jnp.where(valid, -jnp.exp(alog_ref[0]) * softplus, 0.0)
    beta = jnp.where(valid, _sigmoid(ab), 0.0)
    r = lax.broadcasted_iota(jnp.int32, (C, C), 0)
    cc = lax.broadcasted_iota(jnp.int32, (C, C), 1)
    incl = r >= cc
    strict = r > cc
    gcum = jnp.dot(jnp.where(incl, 1.0, 0.0), g, precision=_HI, preferred_element_type=F32)
    gcum_t = gcum.T

    q_ssq = _row_sums_of_squares([qs[:, sl] for sl in sls])
    k_ssq = _row_sums_of_squares([ks[:, sl] for sl in sls])
    qn = [qs[:, sls[h]] * lax.rsqrt(q_ssq[h] + NORM_EPS) * (dk ** -0.5) for h in heads]
    kn = [ks[:, sls[h]] * lax.rsqrt(k_ssq[h] + NORM_EPS) for h in heads]
    bcol = [beta[:, hg + h:hg + h + 1] for h in heads]
    gcol = [gcum[:, h:h + 1] for h in heads]
    glast = [gcum[C - 1:C, h:h + 1] for h in heads]
    eg = [jnp.exp(gcol[h]) for h in heads]
    decay = [jnp.where(incl, jnp.exp(jnp.where(incl, gcol[h] - gcum_t[h:h + 1, :], 0.0)), 0.0) for h in heads]
    kb = [kn[h] * bcol[h] for h in heads]
    knb = [kn[h].astype(BF16) for h in heads]
    a_mat = [jnp.where(strict, _dot_nt(kb[h].astype(BF16), knb[h]) * decay[h], 0.0) for h in heads]

    gain = gain_ref[...]
    for h in heads:
        s_ref[h] = s_ref[h] * eg_s[h:h + 1, :] + ds_s[h]
    for h in heads:
        o = os_s[h]
        z = z_ref[:, sls[h]]
        ms = jnp.mean(o * o, axis=-1, keepdims=True)
        o_ref[:, sls[h]] = (o * lax.rsqrt(ms + NORM_EPS) * gain * (z * _sigmoid(z))).astype(o_ref.dtype)

    t_mat = _tri_inverse_many(a_mat)
    rhs = [jnp.concatenate([vs[:, sls[h]] * bcol[h], kb[h] * eg[h]], axis=1).astype(BF16) for h in heads]
    tv = [_dot(t_mat[h].astype(BF16), rhs[h]) for h in heads]
    qk = [_dot_nt(qn[h].astype(BF16), knb[h]) * decay[h] for h in heads]
    for h in heads:
        val_s[:, sls[h]] = tv[h][:, :dk]
        kcd_s[:, sls[h]] = tv[h][:, dk:].astype(BF16)
        qdec_s[:, sls[h]] = (qn[h] * eg[h]).astype(BF16)
        kdec_s[:, sls[h]] = (kn[h] * jnp.exp(glast[h] - gcol[h])).astype(BF16)
        eg_s[h:h + 1, :] = jnp.broadcast_to(jnp.exp(glast[h]), (1, LANES))
    qk_s[...] = jnp.concatenate(qk, axis=1)

    done = (cj == ncj - 1) & (t > 0)

    @pl.when(done & (seq_j < n_new))
    def _():
        snew_ref[0] = s_ref[...]

    @pl.when(done & (seq_j >= n_new))
    def _():
        srun_ref[0] = s_ref[...]


def _gdn_mixer(proj, gates, conv0, s0, conv_w, a_log, dt_bias, o_gain, *, heads, dk, hg, n_p, cps, cs, l_valid):
    m = proj.shape[0]
    C = CHUNK
    w = heads * dk
    n_hg = heads // hg
    gw = hg * dk
    n_items = m // C
    n_new, n_run = n_p // cps, s0.shape[0]
    width = conv_w.shape[0]
    info = dict(n_p=n_p, cps=cps, cs=cs, l_valid=l_valid)

    def seq_of(i):
        return _item_info(i, n_p, cps, cs, l_valid)[0]

    prep = lambda t: jnp.minimum(t, n_items - 1)
    scan = lambda t: jnp.maximum(t - 1, 0)
    cur = lambda off: pl.BlockSpec((C, gw), lambda g, t: (prep(t), off * n_hg + g))
    prev = lambda off: pl.BlockSpec(
        (SUBLANES, gw), lambda g, t: (jnp.maximum(prep(t) * (C // SUBLANES) - 1, 0), off * n_hg + g))
    init = lambda off: pl.BlockSpec((1, SUBLANES, gw), lambda g, t: (seq_of(prep(t)), 0, off * n_hg + g))
    wspec = lambda off: pl.BlockSpec((width, gw), lambda g, t: (0, off * n_hg + g))
    gate_vec = pl.BlockSpec((1, 1, LANES), lambda g, t: (g, 0, 0))
    state_new = pl.BlockSpec((1, hg, dk, dk), lambda g, t: (jnp.minimum(seq_of(scan(t)), n_new - 1), g, 0, 0))
    state_run = pl.BlockSpec((1, hg, dk, dk), lambda g, t: (jnp.maximum(seq_of(scan(t)) - n_new, 0), g, 0, 0))

    kpad = _round_up(SUBLANES + C, LANES)
    r = jnp.arange((width - 1) * C)
    src = (r % C) + SUBLANES - (width - 1) + r // C
    col = jnp.arange(2 * kpad)
    shift = ((col[None, :] == src[:, None]) | (col[None, :] == src[:, None] + kpad)).astype(BF16)

    alog_g = jnp.pad(a_log.reshape(n_hg, 1, hg).astype(F32), ((0, 0), (0, 0), (0, LANES - hg)))
    dtb_g = jnp.pad(dt_bias.reshape(n_hg, 1, hg).astype(F32), ((0, 0), (0, 0), (0, LANES - hg)))

    return pl.pallas_call(
        functools.partial(_gdn_kernel, hg=hg, dk=dk, width=width, n_items=n_items, **info),
        out_shape=(jax.ShapeDtypeStruct((m, w), BF16),
                   jax.ShapeDtypeStruct((n_new, heads, dk, dk), F32),
                   jax.ShapeDtypeStruct((n_run, heads, dk, dk), F32)),
        grid=(n_hg, n_items + 1),
        in_specs=[cur(0), cur(1), cur(2), prev(0), prev(1), prev(2), init(0), init(1), init(2),
                  wspec(0), wspec(1), wspec(2),
                  pl.BlockSpec((C, LANES), lambda g, t: (prep(t), g)),
                  gate_vec, gate_vec,
                  pl.BlockSpec(shift.shape, lambda g, t: (0, 0)),
                  pl.BlockSpec((C, gw), lambda g, t: (scan(t), 3 * n_hg + g)),
                  pl.BlockSpec((1, dk), lambda g, t: (0, 0)),
                  state_run],
        out_specs=(pl.BlockSpec((C, gw), lambda g, t: (scan(t), g)), state_new, state_run),
        scratch_shapes=[pltpu.VMEM((C, gw), F32), pltpu.VMEM((C, gw), BF16), pltpu.VMEM((C, gw), BF16),
                        pltpu.VMEM((C, gw), BF16), pltpu.VMEM((C, hg * C), F32), pltpu.VMEM((hg, LANES), F32),
                        pltpu.VMEM((hg, dk, dk), F32)],
        compiler_params=_params(("parallel", "arbitrary")),
        name="gdn_mixer",
    )(proj, proj, proj, proj, proj, proj, conv0, conv0, conv0, conv_w, conv_w, conv_w,
      gates, alog_g, dtb_g, shift, proj, o_gain.reshape(1, dk).astype(F32), s0)


def _split_half_layout(x, half):
    pad = [(0, 0)] * (x.ndim - 1) + [(0, LANES // 2 - half)]
    return jnp.concatenate([jnp.pad(x[..., :half], pad), jnp.pad(x[..., half:], pad)], axis=-1)


def _rope_tables(pos, half):
    inv_freq = 1.0 / (ROPE_THETA ** (jnp.arange(half, dtype=F32) / half))
    ang = pos[:, None] * inv_freq[None, :]
    cos, sin = jnp.cos(ang), jnp.sin(ang)
    cos_t = _split_half_layout(jnp.concatenate([cos, cos], -1), half)
    sin_t = _split_half_layout(jnp.concatenate([-sin, sin], -1), half)
    return cos_t, sin_t


def kernel(x_prompt, x_sample, state_conv, state_delta, cache_ckv, cache_krope, meta_tokens,
           a_w_in, a_conv_w, a_a_log, a_dt_bias, a_o_gain, a_w_o,
           b_w_dq, b_q_gain, b_w_uq, b_w_o,
           kv_w_dkv, kv_gain, kv_w_uk, kv_w_uv,
           mlp_w_up, mlp_w_down, ln_gain, ln_bias):
    B, seq, D = x_prompt.shape
    DB, ts, _ = x_sample.shape
    n_meta = meta_tokens.shape[0]
    depth = ln_gain.shape[0]
    n_a = a_w_in.shape[0]
    H = a_a_log.shape[1]
    dk = a_o_gain.shape[1]
    W = H * dk
    width = a_conv_w.shape[1]
    past = cache_ckv.shape[1]
    KV, MH, nope = kv_w_uk.shape
    vd = kv_w_uv.shape[2]
    rope = cache_krope.shape[2]
    half = rope // 2
    alpha = (2 * depth) ** 0.25
    scale = (nope + rope) ** -0.5
    assert ts % CHUNK == 0 and ts >= width - 1 and half < LANES // 2
    assert nope % LANES == 0 and vd % LANES == 0 and KV % LANES == 0 and dk % LANES == 0

    L = n_meta + seq
    LP = _round_up(L, LANES)
    n_prompt_rows = B * LP
    M = n_prompt_rows + DB * ts
    cps, cs = LP // CHUNK, ts // CHUNK
    n_p = B * cps
    hg = min(32, H)
    n_hg = H // hg

    meta = jnp.broadcast_to(meta_tokens.astype(F32)[None], (B, n_meta, D))
    xp = jnp.concatenate([meta, x_prompt, jnp.zeros((B, LP - L, D), F32)], axis=1)
    x = jnp.concatenate([xp.reshape(n_prompt_rows, D), x_sample.reshape(DB * ts, D)], axis=0)
    xb = x.astype(BF16)

    pos = jnp.concatenate([jnp.tile(jnp.arange(LP, dtype=F32), B),
                           jnp.tile(past + jnp.arange(ts, dtype=F32), DB)])
    cos_t, sin_t = _rope_tables(pos, half)

    w_in_t = jnp.swapaxes(a_w_in, 1, 2)

    conv0 = jnp.concatenate([jnp.zeros((n_a, B, width - 1, 3 * W), F32), state_conv.astype(F32)], axis=1)
    conv0 = jnp.pad(conv0, ((0, 0), (0, 0), (SUBLANES - (width - 1), 0), (0, 0)))
    s0 = state_delta.astype(F32)

    new_conv_p, new_conv_s, new_s = [], [], []
    k_slabs = None
    lat = None
    for layer in range(depth):
        if layer < n_a:
            w_ab = a_w_in[layer, :, 4 * W:]
            gate_w = jnp.concatenate(
                [w_ab[:, :H].reshape(D, n_hg, hg), w_ab[:, H:].reshape(D, n_hg, hg),
                 jnp.zeros((D, n_hg, LANES - 2 * hg), w_ab.dtype)], axis=-1).reshape(D, n_hg * LANES)
            proj = _matmul(xb, w_in_t, F32, layer=layer, n_use=4 * W, b_transposed=True)
            gates = _matmul(xb, gate_w.astype(BF16), F32, bm=328)
            o, s_new, s_run = _gdn_mixer(proj, gates, conv0[layer], s0[layer], a_conv_w[layer].astype(F32),
                                         a_a_log[layer], a_dt_bias[layer], a_o_gain[layer],
                                         heads=H, dk=dk, hg=hg, n_p=n_p, cps=cps, cs=cs, l_valid=L)
            h = _matmul(o, a_w_o, BF16, layer=layer)
            new_conv_p.append(jnp.stack(
                [proj[b * LP + L - (width - 1):b * LP + L, :3 * W] for b in range(B)]))
            new_conv_s.append(jnp.stack(
                [proj[n_prompt_rows + ts - (width - 1) + t::ts, :3 * W] for t in range(width - 1)], axis=1))
            new_s.append((s_new, s_run))
        else:
            j = layer - n_a
            ql = _matmul(xb, b_w_dq[j].astype(BF16), BF16, epilogue="rms", gain=b_q_gain[j])
            w_uq = b_w_uq[j].reshape(-1, MH, nope + rope)
            w_uq = jnp.concatenate([w_uq[..., :nope], _split_half_layout(w_uq[..., nope:], half)], axis=-1)
            q = _matmul(ql, w_uq.reshape(-1, MH * (nope + LANES)).astype(BF16), BF16)
            kp, vp, ks, ls_t, vs, lkp, lks = k_slabs
            o = _attention_causal(q, cos_t, sin_t, kp, vp, nb=B, lq=LP, lk_pad=lkp, heads=MH, nope=nope,
                                  vd=vd, bq=_pick_block(LP, 1408, LANES), bk=512, scale=scale, n_meta=n_meta)
            o = _attention_full(q, cos_t, sin_t, ks, ls_t, vs, o, nb=DB, lq=ts, lk_pad=lks, row0=n_prompt_rows,
                                heads=MH, hb=min(4, MH), nope=nope, vd=vd, kv=KV, scale=scale,
                                n_keys=past + ts)
            h = _matmul(o, b_w_o, BF16, layer=j)

        x, xb = _deepnorm(x, h, ln_gain[layer, 0], ln_bias[layer, 0], alpha)
        hid = _matmul(xb, mlp_w_up, BF16, layer=layer, epilogue="relu2")
        h = _matmul(hid, mlp_w_down, BF16, layer=layer, bn=1024, bk=2048)
        x, xb = _deepnorm(x, h, ln_gain[layer, 1], ln_bias[layer, 1], alpha)

        if layer == n_a - 1:
            w_dkv = jnp.concatenate([kv_w_dkv[:, :KV], _split_half_layout(kv_w_dkv[:, KV:], half)], axis=1)
            lat = _kv_post(_matmul(xb, w_dkv.astype(BF16), F32, bm=328, bn=640), kv_gain.astype(F32),
                           cos_t, sin_t, KV)
            lane_ids = jnp.arange(LANES)
            eye = jnp.where((lane_ids[:, None] == lane_ids[None, :]) & (lane_ids[:, None] != half), 1.0, 0.0)
            ones_col = jnp.where((lane_ids[:, None] == half) & (lane_ids[None, :] == 0), 1.0, 0.0)
            w_k = jnp.concatenate(
                [jnp.concatenate([kv_w_uk.astype(F32), jnp.zeros((KV, MH, LANES), F32)], axis=-1),
                 jnp.concatenate([jnp.zeros((LANES, MH, nope), F32),
                                  jnp.broadcast_to(eye[:, None, :], (LANES, MH, LANES))], axis=-1)],
                axis=0).reshape(KV + LANES, MH * (nope + LANES)).astype(BF16)
            w_v = jnp.concatenate(
                [jnp.concatenate([kv_w_uv.astype(F32), jnp.zeros((KV, MH, LANES), F32)], axis=-1),
                 jnp.concatenate([jnp.zeros((LANES, MH, vd), F32),
                                  jnp.broadcast_to(ones_col[:, None, :], (LANES, MH, LANES))], axis=-1)],
                axis=0).reshape(KV + LANES, MH * (vd + LANES)).astype(BF16)
            one_lane = jnp.where(jnp.arange(KV + LANES) == KV + half, 1.0, 0.0)
            lkp = _round_up(L, 512)
            lat_p = lat[:n_prompt_rows].reshape(B, LP, KV + LANES)
            if lkp >= LP:
                lat_p = jnp.pad(lat_p, ((0, 0), (0, lkp - LP), (0, 0)))
            else:
                lat_p = lat_p[:, :lkp]
            lat_p = (lat_p.reshape(B * lkp, KV + LANES) + one_lane).astype(BF16)
            lks = _round_up(past + ts, LANES)
            cache = jnp.concatenate([cache_ckv.astype(F32), _split_half_layout(cache_krope.astype(F32), half)],
                                    axis=-1)
            lat_s = jnp.concatenate([cache, lat[n_prompt_rows:].reshape(DB, ts, KV + LANES),
                                     jnp.zeros((DB, lks - past - ts, KV + LANES), F32)], axis=1)
            lat_s = (lat_s.reshape(DB * lks, KV + LANES) + one_lane).astype(BF16)
            w_kn_t = jnp.transpose(kv_w_uk, (1, 2, 0)).reshape(MH * nope, KV).astype(BF16)
            lat_s_t = lat_s.T
            k_slabs = (_matmul(w_k.T, lat_p.T, BF16, bm=1024), _matmul(lat_p, w_v, BF16, bm=1024),
                       _matmul(w_kn_t, lat_s_t, BF16, bm=1024), lat_s_t,
                       _matmul(lat_s[:, :KV], kv_w_uv.reshape(KV, MH * vd).astype(BF16), BF16, bm=1024),
                       lkp, lks)

    def unsplit(r):
        return jnp.concatenate([r[..., :half], r[..., LANES // 2:LANES // 2 + half]], axis=-1)

    lat_p = lat[:n_prompt_rows].reshape(B, LP, KV + LANES)[:, :L]
    lat_s = lat[n_prompt_rows:].reshape(DB, ts, KV + LANES)
    y_prompt = jnp.stack([x[b * LP + n_meta:b * LP + L] for b in range(B)])
    y_sample = x[n_prompt_rows:].reshape(DB, ts, D)
    return (y_prompt, y_sample,
            jnp.stack(new_conv_p), jnp.stack([s[0] for s in new_s]),
            lat_p[..., :KV], unsplit(lat_p[..., KV:]),
            jnp.stack(new_conv_s), jnp.stack([s[1] for s in new_s]),
            lat_s[..., :KV], unsplit(lat_s[..., KV:]))
```

```python
import functools
import math

import jax
import jax.numpy as jnp
from jax import lax
from jax.experimental import pallas as pl
from jax.experimental.pallas import tpu as pltpu

CHUNK = 64
ROPE_THETA = 10000.0
NORM_EPS = 1e-6
NEG_INF = -1e30
LANES = 128
SUBLANES = 8
VMEM_LIMIT_BYTES = 56 * 1024 * 1024
_SPLIT_LEVELS = 2
_CHUNK_SHIFT = CHUNK.bit_length() - 1
assert 1 << _CHUNK_SHIFT == CHUNK

F32 = jnp.float32
BF16 = jnp.bfloat16
_HI = lax.Precision.HIGHEST


def _round_up(x, m):
    return -(-x // m) * m


def _pick_block(dim, target, align):
    best = None
    for d in range(align, min(dim, target) + 1, align):
        if dim % d == 0:
            best = d
    return best if best is not None else dim


def _dot(a, b):
    return jnp.dot(a, b, preferred_element_type=F32)


def _dot_nt(a, b):
    return lax.dot_general(a, b, (((1,), (1,)), ((), ())), preferred_element_type=F32)


def _dot_tn(a, b):
    return lax.dot_general(a, b, (((0,), (0,)), ((), ())), preferred_element_type=F32)


def _sigmoid(x):
    return 1.0 / (1.0 + jnp.exp(-x))


def _params(sem):
    return pltpu.CompilerParams(dimension_semantics=sem, vmem_limit_bytes=VMEM_LIMIT_BYTES)


def _mm_kernel(*refs, nk, epilogue, b_transposed):
    if epilogue == "rms":
        a_ref, b_ref, g_ref, o_ref = refs[:4]
        rest = refs[4:]
    else:
        a_ref, b_ref, o_ref = refs[:3]
        g_ref = None
        rest = refs[3:]

    def finish(acc):
        if epilogue == "relu2":
            r = jnp.maximum(acc, 0.0)
            acc = r * r
        elif epilogue == "rms":
            ms = jnp.mean(acc * acc, axis=-1, keepdims=True)
            acc = acc * lax.rsqrt(ms + NORM_EPS) * g_ref[...]
        o_ref[...] = acc.astype(o_ref.dtype)

    def product():
        b = b_ref[...].astype(BF16)
        return _dot_nt(a_ref[...], b) if b_transposed else _dot(a_ref[...], b)

    if nk == 1:
        finish(product())
    else:
        acc_ref = rest[0]
        k = pl.program_id(2)

        @pl.when(k == 0)
        def _():
            acc_ref[...] = jnp.zeros_like(acc_ref)

        acc_ref[...] += product()

        @pl.when(k == nk - 1)
        def _():
            finish(acc_ref[...])


def _matmul(a, b, out_dtype, *, layer=None, n_use=None, b_transposed=False, epilogue="none", gain=None,
            bm=1312, bn=512, bk=4096):
    m, kdim = a.shape
    n = n_use if n_use is not None else b.shape[-2 if b_transposed else -1]
    bm = _pick_block(m, bm, 16)
    bn = n if epilogue == "rms" else _pick_block(n, bn, LANES)
    bk = _pick_block(kdim, bk, LANES)
    nk = kdim // bk
    a_mode = dict(pipeline_mode=pl.Buffered(1)) if nk == 1 and n // bn >= 4 else {}
    b_blk = (bn, bk) if b_transposed else (bk, bn)
    b_idx = (lambda k, j: (j, k)) if b_transposed else (lambda k, j: (k, j))
    if b.ndim == 3:
        b_spec = pl.BlockSpec((None,) + b_blk, lambda i, j, k: (layer,) + b_idx(k, j))
    else:
        b_spec = pl.BlockSpec(b_blk, lambda i, j, k: b_idx(k, j))
    in_specs = [pl.BlockSpec((bm, bk), lambda i, j, k: (i, k), **a_mode), b_spec]
    args = [a, b]
    if epilogue == "rms":
        in_specs.append(pl.BlockSpec((1, bn), lambda i, j, k: (0, j)))
        args.append(gain.reshape(1, n).astype(F32))
    scratch = [pltpu.VMEM((bm, bn), F32)] if nk > 1 else []
    return pl.pallas_call(
        functools.partial(_mm_kernel, nk=nk, epilogue=epilogue, b_transposed=b_transposed),
        out_shape=jax.ShapeDtypeStruct((m, n), out_dtype),
        grid=(m // bm, n // bn, nk),
        in_specs=in_specs,
        out_specs=pl.BlockSpec((bm, bn), lambda i, j, k: (i, j)),
        scratch_shapes=scratch,
        compiler_params=_params(("parallel", "parallel", "arbitrary")),
        name="matmul_" + epilogue,
    )(*args)


def _ln_kernel(x_ref, h_ref, g_ref, b_ref, of_ref, ob_ref, *, alpha):
    y = alpha * x_ref[...] + h_ref[...].astype(F32)
    mu = jnp.mean(y, axis=-1, keepdims=True)
    d = y - mu
    var = jnp.mean(d * d, axis=-1, keepdims=True)
    out = d * lax.rsqrt(var + NORM_EPS) * g_ref[...] + b_ref[...]
    of_ref[...] = out
    ob_ref[...] = out.astype(BF16)


def _deepnorm(x, h, gain, bias, alpha):
    m, d = x.shape
    br = _pick_block(m, 256, 16)
    row = pl.BlockSpec((br, d), lambda i: (i, 0))
    vec = pl.BlockSpec((1, d), lambda i: (0, 0))
    return pl.pallas_call(
        functools.partial(_ln_kernel, alpha=alpha),
        out_shape=(jax.ShapeDtypeStruct((m, d), F32), jax.ShapeDtypeStruct((m, d), BF16)),
        grid=(m // br,),
        in_specs=[row, row, vec, vec],
        out_specs=(row, row),
        compiler_params=_params(("parallel",)),
        name="deepnorm",
    )(x, h, gain.reshape(1, d), bias.reshape(1, d))


def _kv_post_kernel(lat_ref, g_ref, cos_ref, sin_ref, o_ref, *, kv):
    c = lat_ref[:, :kv]
    ms = jnp.mean(c * c, axis=-1, keepdims=True)
    o_ref[:, :kv] = c * lax.rsqrt(ms + NORM_EPS) * g_ref[...]
    r = lat_ref[:, kv:]
    o_ref[:, kv:] = r * cos_ref[...] + pltpu.roll(r, LANES // 2, axis=1) * sin_ref[...]


def _kv_post(lat, gain, cos_t, sin_t, kv):
    m, w = lat.shape
    br = _pick_block(m, 256, 8)
    return pl.pallas_call(
        functools.partial(_kv_post_kernel, kv=kv),
        out_shape=jax.ShapeDtypeStruct((m, w), F32),
        grid=(m // br,),
        in_specs=[pl.BlockSpec((br, w), lambda i: (i, 0)),
                  pl.BlockSpec((1, kv), lambda i: (0, 0)),
                  pl.BlockSpec((br, LANES), lambda i: (i, 0)),
                  pl.BlockSpec((br, LANES), lambda i: (i, 0))],
        out_specs=pl.BlockSpec((br, w), lambda i: (i, 0)),
        compiler_params=_params(("parallel",)),
        name="kv_post",
    )(lat, gain.reshape(1, kv), cos_t, sin_t)


def _rope_query(q, cos, sin, nope, scale):
    q = q.astype(F32)
    qr = q[:, nope:]
    qr = qr * cos + pltpu.roll(qr, LANES // 2, axis=1) * sin
    return (jnp.concatenate([q[:, :nope], qr], axis=1) * (scale * math.log2(math.e))).astype(BF16)


def _attn_causal_kernel(q_ref, cos_ref, sin_ref, kt_ref, v_ref, o_ref,
                        q_scr, s_scr, p_scr, m_scr, a_scr, acc_scr, *, bq, bk, rt, n_groups, nope, vd, scale,
                        n_meta, lk_pad):
    qi = pl.program_id(2)
    tiles = [slice(t * rt, (t + 1) * rt) for t in range(bq // rt)]
    lane_chunks = [slice(c * LANES, (c + 1) * LANES) for c in range(bk // LANES)]

    def visible_end(r):
        chunk_id = lax.shift_right_arithmetic(r - n_meta, _CHUNK_SHIFT) + 1
        return jnp.where(r < n_meta, n_meta, chunk_id * CHUNK + n_meta)

    n_full = visible_end(qi * bq) // bk
    nkb = (jnp.minimum(visible_end(qi * bq + bq - 1), lk_pad) + bk - 1) // bk

    for rows in tiles:
        q_scr[rows, :] = _rope_query(q_ref[rows, :], cos_ref[rows, :], sin_ref[rows, :], nope, scale)
    m_scr[...] = jnp.full(m_scr.shape, NEG_INF, F32)
    acc_scr[...] = jnp.zeros(acc_scr.shape, F32)

    gsz = bq // n_groups
    groups = [slice(g * gsz, (g + 1) * gsz) for g in range(n_groups)]
    tiles_per_group = gsz // rt
    last_block = lk_pad // bk - 1
    last_g = n_groups - 1

    def block_start(j):
        return pl.multiple_of(j * bk, bk)

    def scores(g, j):
        s_scr[groups[g], :] = _dot(q_scr[groups[g], :], kt_ref[:, pl.ds(block_start(j), bk)])

    def values(g, j):
        rows = groups[g]
        a = a_scr[rows, :]
        acc_scr[rows, :] = (jnp.concatenate([a] * (acc_scr.shape[1] // LANES), axis=1) * acc_scr[rows, :]
                            + _dot(p_scr[rows, :], v_ref[pl.ds(block_start(j), bk), :]))

    p_scr[groups[last_g], :] = jnp.zeros((gsz, bk), BF16)
    a_scr[groups[last_g], :] = jnp.ones((gsz, LANES), F32)
    scores(0, 0)

    def step(j, carry, masked):
        start = block_start(j)

        def softmax(g):
            for t in range(g * tiles_per_group, (g + 1) * tiles_per_group):
                rows = tiles[t]
                s = s_scr[rows, :]
                if masked:
                    kpos = start + lax.broadcasted_iota(jnp.int32, (1, bk), 1)
                    vis = visible_end(qi * bq + t * rt + lax.broadcasted_iota(jnp.int32, (rt, 1), 0))
                    s = jnp.where(kpos < vis, s, NEG_INF)
                m_old = m_scr[rows, :]
                m_new = jnp.maximum(m_old, jnp.max(s, axis=-1, keepdims=True))
                for lc in lane_chunks:
                    p_scr[rows, lc] = jnp.exp2(s[:, lc] - m_new).astype(BF16)
                a_scr[rows, :] = jnp.exp2(m_old - m_new)
                m_scr[rows, :] = m_new

        values(last_g, jnp.maximum(j - 1, 0))
        for g in range(n_groups):
            if g < last_g:
                scores(g + 1, j)
            else:
                scores(0, jnp.minimum(j + 1, last_block))
            softmax(g)
            if g < last_g:
                values(g, j)
        return carry

    lax.fori_loop(0, n_full, functools.partial(step, masked=False), 0)
    lax.fori_loop(n_full, nkb, functools.partial(step, masked=True), 0)
    values(last_g, nkb - 1)
    for rows in tiles:
        o_ref[rows, :] = (acc_scr[rows, :vd] / acc_scr[rows, vd:vd + 1]).astype(o_ref.dtype)


def _attn_full_kernel(q_ref, cos_ref, sin_ref, knt_ref, krt_ref, v_ref, o_ref, *, hb, nope, vd, scale, n_keys):
    dqk = nope + LANES
    lq = q_ref.shape[0]
    heads = range(hb)
    cos, sin = cos_ref[...], sin_ref[...]
    kpos = lax.broadcasted_iota(jnp.int32, (1, knt_ref.shape[1]), 1)
    qs = [_rope_query(q_ref[:, h * dqk:(h + 1) * dqk], cos, sin, nope, scale) for h in heads]
    s_rope = _dot(jnp.concatenate([q[:, nope:] for q in qs], axis=0), krt_ref[...])
    ss = [jnp.where(kpos < n_keys,
                    _dot(qs[h][:, :nope], knt_ref[h * nope:(h + 1) * nope, :]) + s_rope[h * lq:(h + 1) * lq],
                    NEG_INF) for h in heads]
    es = [jnp.exp2(s - jnp.max(s, axis=-1, keepdims=True)) for s in ss]
    ls = [jnp.sum(e, axis=-1, keepdims=True) for e in es]
    pv = [_dot(es[h].astype(BF16), v_ref[:, h * vd:(h + 1) * vd]) for h in heads]
    for h in heads:
        o_ref[:, h * vd:(h + 1) * vd] = (pv[h] / ls[h]).astype(o_ref.dtype)


def _attention_causal(q, cos_t, sin_t, kt, vf, *, nb, lq, lk_pad, heads, nope, vd, bq, bk, scale, n_meta):
    dqk = nope + LANES
    nq = lq // bq
    rt = 32 if bq % 32 == 0 else bq
    n_groups = 4 if bq % (4 * rt) == 0 else 2
    assert bq % (n_groups * rt) == 0
    return pl.pallas_call(
        functools.partial(_attn_causal_kernel, bq=bq, bk=bk, rt=rt, n_groups=n_groups, nope=nope, vd=vd,
                          scale=scale,
                          n_meta=n_meta, lk_pad=lk_pad),
        out_shape=jax.ShapeDtypeStruct((q.shape[0], heads * vd), BF16),
        grid=(nb, heads, nq),
        in_specs=[pl.BlockSpec((bq, dqk), lambda b, h, i: (b * nq + i, h)),
                  pl.BlockSpec((bq, LANES), lambda b, h, i: (b * nq + i, 0)),
                  pl.BlockSpec((bq, LANES), lambda b, h, i: (b * nq + i, 0)),
                  pl.BlockSpec((dqk, lk_pad), lambda b, h, i: (h, b)),
                  pl.BlockSpec((lk_pad, vd + LANES), lambda b, h, i: (b, h))],
        out_specs=pl.BlockSpec((bq, vd), lambda b, h, i: (b * nq + i, h)),
        scratch_shapes=[pltpu.VMEM((bq, dqk), BF16), pltpu.VMEM((bq, bk), F32), pltpu.VMEM((bq, bk), BF16),
                        pltpu.VMEM((bq, LANES), F32), pltpu.VMEM((bq, LANES), F32),
                        pltpu.VMEM((bq, vd + LANES), F32)],
        compiler_params=_params(("parallel", "parallel", "arbitrary")),
        name="mla_attention_causal",
    )(q, cos_t, sin_t, kt, vf)


def _attention_full(q, cos_t, sin_t, knt, lat_t, vf, o_all, *, nb, lq, lk_pad, row0, heads, hb, nope, vd, kv,
                    scale, n_keys):
    dqk = nope + LANES
    blk0 = row0 // lq

    def body(q_ref, cos_ref, sin_ref, knt_ref, krt_ref, v_ref, o_all_ref, o_ref):
        _attn_full_kernel(q_ref, cos_ref, sin_ref, knt_ref, krt_ref, v_ref, o_ref,
                          hb=hb, nope=nope, vd=vd, scale=scale, n_keys=n_keys)

    return pl.pallas_call(
        body,
        out_shape=jax.ShapeDtypeStruct(o_all.shape, o_all.dtype),
        grid=(nb, heads // hb),
        in_specs=[pl.BlockSpec((lq, hb * dqk), lambda b, g: (blk0 + b, g)),
                  pl.BlockSpec((lq, LANES), lambda b, g: (blk0 + b, 0)),
                  pl.BlockSpec((lq, LANES), lambda b, g: (blk0 + b, 0)),
                  pl.BlockSpec((hb * nope, lk_pad), lambda b, g: (g, b)),
                  pl.BlockSpec((LANES, lk_pad), lambda b, g: (kv // LANES, b)),
                  pl.BlockSpec((lk_pad, hb * vd), lambda b, g: (b, g)),
                  pl.BlockSpec(memory_space=pl.ANY)],
        out_specs=pl.BlockSpec((lq, hb * vd), lambda b, g: (blk0 + b, g)),
        input_output_aliases={6: 0},
        compiler_params=_params(("parallel", "parallel")),
        name="mla_attention_full",
    )(q, cos_t, sin_t, knt, lat_t, vf, o_all)


def _item_info(i, n_p, cps, cs, l_valid):
    is_p = i < n_p
    j = i - n_p
    seq = jnp.where(is_p, i // cps, n_p // cps + j // cs)
    c = jnp.where(is_p, i % cps, j % cs)
    nc = jnp.where(is_p, cps, cs)
    nvalid = jnp.where(is_p, jnp.clip(l_valid - c * CHUNK, 0, CHUNK), CHUNK)
    return seq, c, nc, nvalid


def _row_sums_of_squares(xs):
    rows, n = xs[0].shape
    parts = []
    for x in xs:
        sq = x * x
        hi = sq.astype(BF16)
        parts.append(jnp.concatenate([hi, (sq - hi.astype(F32)).astype(BF16)], axis=1))
    tot = _dot(jnp.concatenate(parts, axis=0), jnp.ones((2 * n, n), BF16))
    return [tot[i * rows:(i + 1) * rows] for i in range(len(xs))]


def _tri_inverse_many(mats):
    n = mats[0].shape[0]
    r = lax.broadcasted_iota(jnp.int32, (n, n), 0)
    c = lax.broadcasted_iota(jnp.int32, (n, n), 1)
    eye = jnp.where(r == c, 1.0, 0.0)
    zero = jnp.zeros((n, n), F32)

    def split(x):
        hi = x.astype(BF16).astype(F32)
        return hi, x - hi

    def lhs_of(x):
        hi, lo = split(x)
        return jnp.concatenate([hi, lo, hi, zero], axis=1).astype(BF16)

    def rhs_of(x):
        hi, lo = split(x)
        return jnp.concatenate([hi, hi, lo, zero], axis=0).astype(BF16)

    ps = [eye - a for a in mats]
    aks = list(mats)
    rhs = [rhs_of(a) for a in mats]
    power, level = 2, 0
    while power < n:
        if level < _SPLIT_LEVELS:
            aks = [_dot(lhs_of(ak), rk) for ak, rk in zip(aks, rhs)]
            rhs = [rhs_of(ak) for ak in aks]
            ps = [p + _dot(lhs_of(p), rk) for p, rk in zip(ps, rhs)]
        else:
            akb = [ak.astype(BF16) for ak in aks]
            aks = [_dot(b, b) for b in akb]
            ps = [p + _dot(p.astype(BF16), ak.astype(BF16)) for p, ak in zip(ps, aks)]
        power *= 2
        level += 1
    return ps


def _gdn_kernel(qc_ref, kc_ref, vc_ref, qp_ref, kp_ref, vp_ref, q0_ref, k0_ref, v0_ref,
                wq_ref, wk_ref, wv_ref, ab_ref, alog_ref, dtb_ref, shift_ref, z_ref, gain_ref, s0_ref,
                o_ref, snew_ref, srun_ref,
                val_s, kcd_s, qdec_s, kdec_s, qk_s, eg_s, s_ref,
                *, hg, dk, n_items, n_p, cps, cs, l_valid, width):
    C = CHUNK
    t = pl.program_id(1)
    i = jnp.minimum(t, n_items - 1)
    j = jnp.maximum(t - 1, 0)
    _, c, _, nvalid = _item_info(i, n_p, cps, cs, l_valid)
    _, cj, ncj, _ = _item_info(j, n_p, cps, cs, l_valid)
    first = c == 0
    heads = range(hg)
    sls = [slice(h * dk, (h + 1) * dk) for h in heads]

    @pl.when(t == 0)
    def _():
        for ref in (val_s, kcd_s, qdec_s, kdec_s, qk_s, eg_s):
            ref[...] = jnp.zeros(ref.shape, ref.dtype)

    n_new = n_p // cps
    seq_j = _item_info(j, n_p, cps, cs, l_valid)[0]

    @pl.when(cj == 0)
    def _():
        s_ref[...] = jnp.where(seq_j < n_new, 0.0, s0_ref[0])

    sb = [s_ref[h].astype(BF16) for h in heads]
    ks_s = [_dot(kcd_s[:, sls[h]], sb[h]) for h in heads]
    qs_s = [_dot(qdec_s[:, sls[h]], sb[h]) for h in heads]

    halo = SUBLANES
    kpad = shift_ref.shape[1] // 2
    shift = shift_ref[...]

    def conv_silu(cur_ref, prev_ref, init_ref, w_ref):
        cur = cur_ref[...]
        rows = jnp.concatenate([jnp.where(first, init_ref[0], prev_ref[...]), cur,
                                jnp.zeros((kpad - halo - C, cur.shape[1]), F32)], axis=0)
        hi = rows.astype(BF16)
        lo = (rows - hi.astype(F32)).astype(BF16)
        shifted = _dot(shift, jnp.concatenate([hi, lo], axis=0))
        y = cur * w_ref[width - 1:width, :]
        for t in range(width - 1):
            y = y + shifted[t * C:(t + 1) * C] * w_ref[t:t + 1, :]
        return y * _sigmoid(y)

    qs = conv_silu(qc_ref, qp_ref, q0_ref, wq_ref)
    ks = conv_silu(kc_ref, kp_ref, k0_ref, wk_ref)
    vs = conv_silu(vc_ref, vp_ref, v0_ref, wv_ref)

    ub = [(val_s[:, sls[h]] - ks_s[h]).astype(BF16) for h in heads]
    ds_s = [_dot_tn(kdec_s[:, sls[h]], ub[h]) for h in heads]
    os_s = [qs_s[h] + _dot(qk_s[:, h * C:(h + 1) * C].astype(BF16), ub[h]) for h in heads]

    ab = ab_ref[...]
    rows = lax.broadcasted_iota(jnp.int32, (C, LANES), 0)
    valid = rows < nvalid
    xs = ab + dtb_ref[0]
    softplus = jnp.maximum(xs, 0.0) + jnp.log(1.0 + jnp.exp(-jnp.abs(xs)))
    g = jnp.where(valid, -jnp.exp(alog_ref[0]) * softplus, 0.0)
    beta = jnp.where(valid, _sigmoid(ab), 0.0)
    r = lax.broadcasted_iota(jnp.int32, (C, C), 0)
    cc = lax.broadcasted_iota(jnp.int32, (C, C), 1)
    incl = r >= cc
    strict = r > cc
    gcum = jnp.dot(jnp.where(incl, 1.0, 0.0), g, precision=_HI, preferred_element_type=F32)
    gcum_t = gcum.T

    q_ssq = _row_sums_of_squares([qs[:, sl] for sl in sls])
    k_ssq = _row_sums_of_squares([ks[:, sl] for sl in sls])
    qn = [qs[:, sls[h]] * lax.rsqrt(q_ssq[h] + NORM_EPS) * (dk ** -0.5) for h in heads]
    kn = [ks[:, sls[h]] * lax.rsqrt(k_ssq[h] + NORM_EPS) for h in heads]
    bcol = [beta[:, hg + h:hg + h + 1] for h in heads]
    gcol = [gcum[:, h:h + 1] for h in heads]
    glast = [gcum[C - 1:C, h:h + 1] for h in heads]
    eg = [jnp.exp(gcol[h]) for h in heads]
    decay = [jnp.where(incl, jnp.exp(jnp.where(incl, gcol[h] - gcum_t[h:h + 1, :], 0.0)), 0.0) for h in heads]
    kb = [kn[h] * bcol[h] for h in heads]
    knb = [kn[h].astype(BF16) for h in heads]
    a_mat = [jnp.where(strict, _dot_nt(kb[h].astype(BF16), knb[h]) * decay[h], 0.0) for h in heads]

    gain = gain_ref[...]
    for h in heads:
        s_ref[h] = s_ref[h] * eg_s[h:h + 1, :] + ds_s[h]
    for h in heads:
        o = os_s[h]
        z = z_ref[:, sls[h]]
        ms = jnp.mean(o * o, axis=-1, keepdims=True)
        o_ref[:, sls[h]] = (o * lax.rsqrt(ms + NORM_EPS) * gain * (z * _sigmoid(z))).astype(o_ref.dtype)

    t_mat = _tri_inverse_many(a_mat)
    rhs = [jnp.concatenate([vs[:, sls[h]] * bcol[h], kb[h] * eg[h]], axis=1).astype(BF16) for h in heads]
    tv = [_dot(t_mat[h].astype(BF16), rhs[h]) for h in heads]
    qk = [_dot_nt(qn[h].astype(BF16), knb[h]) * decay[h] for h in heads]
    for h in heads:
        val_s[:, sls[h]] = tv[h][:, :dk]
        kcd_s[:, sls[h]] = tv[h][:, dk:].astype(BF16)
        qdec_s[:, sls[h]] = (qn[h] * eg[h]).astype(BF16)
        kdec_s[:, sls[h]] = (kn[h] * jnp.exp(glast[h] - gcol[h])).astype(BF16)
        eg_s[h:h + 1, :] = jnp.broadcast_to(jnp.exp(glast[h]), (1, LANES))
    qk_s[...] = jnp.concatenate(qk, axis=1)

    done = (cj == ncj - 1) & (t > 0)

    @pl.when(done & (seq_j < n_new))
    def _():
        snew_ref[0] = s_ref[...]

    @pl.when(done & (seq_j >= n_new))
    def _():
        srun_ref[0] = s_ref[...]


def _gdn_mixer(proj, gates, conv0, s0, conv_w, a_log, dt_bias, o_gain, *, heads, dk, hg, n_p, cps, cs, l_valid):
    m = proj.shape[0]
    C = CHUNK
    w = heads * dk
    n_hg = heads // hg
    gw = hg * dk
    n_items = m // C
    n_new, n_run = n_p // cps, s0.shape[0]
    width = conv_w.shape[0]
    info = dict(n_p=n_p, cps=cps, cs=cs, l_valid=l_valid)

    def seq_of(i):
        return _item_info(i, n_p, cps, cs, l_valid)[0]

    prep = lambda t: jnp.minimum(t, n_items - 1)
    scan = lambda t: jnp.maximum(t - 1, 0)
    cur = lambda off: pl.BlockSpec((C, gw), lambda g, t: (prep(t), off * n_hg + g))
    prev = lambda off: pl.BlockSpec(
        (SUBLANES, gw), lambda g, t: (jnp.maximum(prep(t) * (C // SUBLANES) - 1, 0), off * n_hg + g))
    init = lambda off: pl.BlockSpec((1, SUBLANES, gw), lambda g, t: (seq_of(prep(t)), 0, off * n_hg + g))
    wspec = lambda off: pl.BlockSpec((width, gw), lambda g, t: (0, off * n_hg + g))
    gate_vec = pl.BlockSpec((1, 1, LANES), lambda g, t: (g, 0, 0))
    state_new = pl.BlockSpec((1, hg, dk, dk), lambda g, t: (jnp.minimum(seq_of(scan(t)), n_new - 1), g, 0, 0))
    state_run = pl.BlockSpec((1, hg, dk, dk), lambda g, t: (jnp.maximum(seq_of(scan(t)) - n_new, 0), g, 0, 0))

    kpad = _round_up(SUBLANES + C, LANES)
    r = jnp.arange((width - 1) * C)
    src = (r % C) + SUBLANES - (width - 1) + r // C
    col = jnp.arange(2 * kpad)
    shift = ((col[None, :] == src[:, None]) | (col[None, :] == src[:, None] + kpad)).astype(BF16)

    alog_g = jnp.pad(a_log.reshape(n_hg, 1, hg).astype(F32), ((0, 0), (0, 0), (0, LANES - hg)))
    dtb_g = jnp.pad(dt_bias.reshape(n_hg, 1, hg).astype(F32), ((0, 0), (0, 0), (0, LANES - hg)))

    return pl.pallas_call(
        functools.partial(_gdn_kernel, hg=hg, dk=dk, width=width, n_items=n_items, **info),
        out_shape=(jax.ShapeDtypeStruct((m, w), BF16),
                   jax.ShapeDtypeStruct((n_new, heads, dk, dk), F32),
                   jax.ShapeDtypeStruct((n_run, heads, dk, dk), F32)),
        grid=(n_hg, n_items + 1),
        in_specs=[cur(0), cur(1), cur(2), prev(0), prev(1), prev(2), init(0), init(1), init(2),
                  wspec(0), wspec(1), wspec(2),
                  pl.BlockSpec((C, LANES), lambda g, t: (prep(t), g)),
                  gate_vec, gate_vec,
                  pl.BlockSpec(shift.shape, lambda g, t: (0, 0)),
                  pl.BlockSpec((C, gw), lambda g, t: (scan(t), 3 * n_hg + g)),
                  pl.BlockSpec((1, dk), lambda g, t: (0, 0)),
                  state_run],
        out_specs=(pl.BlockSpec((C, gw), lambda g, t: (scan(t), g)), state_new, state_run),
        scratch_shapes=[pltpu.VMEM((C, gw), F32), pltpu.VMEM((C, gw), BF16), pltpu.VMEM((C, gw), BF16),
                        pltpu.VMEM((C, gw), BF16), pltpu.VMEM((C, hg * C), F32), pltpu.VMEM((hg, LANES), F32),
                        pltpu.VMEM((hg, dk, dk), F32)],
        compiler_params=_params(("parallel", "arbitrary")),
        name="gdn_mixer",
    )(proj, proj, proj, proj, proj, proj, conv0, conv0, conv0, conv_w, conv_w, conv_w,
      gates, alog_g, dtb_g, shift, proj, o_gain.reshape(1, dk).astype(F32), s0)


def _split_half_layout(x, half):
    pad = [(0, 0)] * (x.ndim - 1) + [(0, LANES // 2 - half)]
    return jnp.concatenate([jnp.pad(x[..., :half], pad), jnp.pad(x[..., half:], pad)], axis=-1)


def _rope_tables(pos, half):
    inv_freq = 1.0 / (ROPE_THETA ** (jnp.arange(half, dtype=F32) / half))
    ang = pos[:, None] * inv_freq[None, :]
    cos, sin = jnp.cos(ang), jnp.sin(ang)
    cos_t = _split_half_layout(jnp.concatenate([cos, cos], -1), half)
    sin_t = _split_half_layout(jnp.concatenate([-sin, sin], -1), half)
    return cos_t, sin_t


def kernel(x_prompt, x_sample, state_conv, state_delta, cache_ckv, cache_krope, meta_tokens,
           a_w_in, a_conv_w, a_a_log, a_dt_bias, a_o_gain, a_w_o,
           b_w_dq, b_q_gain, b_w_uq, b_w_o,
           kv_w_dkv, kv_gain, kv_w_uk, kv_w_uv,
           mlp_w_up, mlp_w_down, ln_gain, ln_bias):
    B, seq, D = x_prompt.shape
    DB, ts, _ = x_sample.shape
    n_meta = meta_tokens.shape[0]
    depth = ln_gain.shape[0]
    n_a = a_w_in.shape[0]
    H = a_a_log.shape[1]
    dk = a_o_gain.shape[1]
    W = H * dk
    width = a_conv_w.shape[1]
    past = cache_ckv.shape[1]
    KV, MH, nope = kv_w_uk.shape
    vd = kv_w_uv.shape[2]
    rope = cache_krope.shape[2]
    half = rope // 2
    alpha = (2 * depth) ** 0.25
    scale = (nope + rope) ** -0.5
    assert ts % CHUNK == 0 and ts >= width - 1 and half < LANES // 2
    assert nope % LANES == 0 and vd % LANES == 0 and KV % LANES == 0 and dk % LANES == 0

    L = n_meta + seq
    LP = _round_up(L, LANES)
    n_prompt_rows = B * LP
    M = n_prompt_rows + DB * ts
    cps, cs = LP // CHUNK, ts // CHUNK
    n_p = B * cps
    hg = min(32, H)
    n_hg = H // hg

    meta = jnp.broadcast_to(meta_tokens.astype(F32)[None], (B, n_meta, D))
    xp = jnp.concatenate([meta, x_prompt, jnp.zeros((B, LP - L, D), F32)], axis=1)
    x = jnp.concatenate([xp.reshape(n_prompt_rows, D), x_sample.reshape(DB * ts, D)], axis=0)
    xb = x.astype(BF16)

    pos = jnp.concatenate([jnp.tile(jnp.arange(LP, dtype=F32), B),
                           jnp.tile(past + jnp.arange(ts, dtype=F32), DB)])
    cos_t, sin_t = _rope_tables(pos, half)

    w_in_t = jnp.swapaxes(a_w_in, 1, 2)

    conv0 = jnp.concatenate([jnp.zeros((n_a, B, width - 1, 3 * W), F32), state_conv.astype(F32)], axis=1)
    conv0 = jnp.pad(conv0, ((0, 0), (0, 0), (SUBLANES - (width - 1), 0), (0, 0)))
    s0 = state_delta.astype(F32)

    new_conv_p, new_conv_s, new_s = [], [], []
    k_slabs = None
    lat = None
    for layer in range(depth):
        if layer < n_a:
            w_ab = a_w_in[layer, :, 4 * W:]
            gate_w = jnp.concatenate(
                [w_ab[:, :H].reshape(D, n_hg, hg), w_ab[:, H:].reshape(D, n_hg, hg),
                 jnp.zeros((D, n_hg, LANES - 2 * hg), w_ab.dtype)], axis=-1).reshape(D, n_hg * LANES)
            proj = _matmul(xb, w_in_t, F32, layer=layer, n_use=4 * W, b_transposed=True)
            gates = _matmul(xb, gate_w.astype(BF16), F32, bm=328)
            o, s_new, s_run = _gdn_mixer(proj, gates, conv0[layer], s0[layer], a_conv_w[layer].astype(F32),
                                         a_a_log[layer], a_dt_bias[layer], a_o_gain[layer],
                                         heads=H, dk=dk, hg=hg, n_p=n_p, cps=cps, cs=cs, l_valid=L)
            h = _matmul(o, a_w_o, BF16, layer=layer)
            new_conv_p.append(jnp.stack(
                [proj[b * LP + L - (width - 1):b * LP + L, :3 * W] for b in range(B)]))
            new_conv_s.append(proj[n_prompt_rows:].reshape(DB, ts, 4 * W)[:, ts - (width - 1):, :3 * W])
            new_s.append((s_new, s_run))
        else:
            j = layer - n_a
            ql = _matmul(xb, b_w_dq[j].astype(BF16), BF16, epilogue="rms", gain=b_q_gain[j])
            w_uq = b_w_uq[j].reshape(-1, MH, nope + rope)
            w_uq = jnp.concatenate([w_uq[..., :nope], _split_half_layout(w_uq[..., nope:], half)], axis=-1)
            q = _matmul(ql, w_uq.reshape(-1, MH * (nope + LANES)).astype(BF16), BF16, bn=2048)
            kp, vp, ks, ls_t, vs, lkp, lks = k_slabs
            o = _attention_causal(q, cos_t, sin_t, kp, vp, nb=B, lq=LP, lk_pad=lkp, heads=MH, nope=nope,
                                  vd=vd, bq=_pick_block(LP, 1408, LANES), bk=512, scale=scale, n_meta=n_meta)
            o = _attention_full(q, cos_t, sin_t, ks, ls_t, vs, o, nb=DB, lq=ts, lk_pad=lks, row0=n_prompt_rows,
                                heads=MH, hb=min(8, MH), nope=nope, vd=vd, kv=KV, scale=scale,
                                n_keys=past + ts)
            h = _matmul(o, b_w_o, BF16, layer=j)

        x, xb = _deepnorm(x, h, ln_gain[layer, 0], ln_bias[layer, 0], alpha)
        hid = _matmul(xb, mlp_w_up, BF16, layer=layer, epilogue="relu2")
        h = _matmul(hid, mlp_w_down, BF16, layer=layer, bn=1024, bk=2048)
        x, xb = _deepnorm(x, h, ln_gain[layer, 1], ln_bias[layer, 1], alpha)

        if layer == n_a - 1:
            w_dkv = jnp.concatenate([kv_w_dkv[:, :KV], _split_half_layout(kv_w_dkv[:, KV:], half)], axis=1)
            lat = _kv_post(_matmul(xb, w_dkv.astype(BF16), F32, bm=328, bn=640), kv_gain.astype(F32),
                           cos_t, sin_t, KV)
            lane_ids = jnp.arange(LANES)
            eye = jnp.where((lane_ids[:, None] == lane_ids[None, :]) & (lane_ids[:, None] != half), 1.0, 0.0)
            ones_col = jnp.where((lane_ids[:, None] == half) & (lane_ids[None, :] == 0), 1.0, 0.0)
            w_k = jnp.concatenate(
                [jnp.concatenate([kv_w_uk.astype(F32), jnp.zeros((KV, MH, LANES), F32)], axis=-1),
                 jnp.concatenate([jnp.zeros((LANES, MH, nope), F32),
                                  jnp.broadcast_to(eye[:, None, :], (LANES, MH, LANES))], axis=-1)],
                axis=0).reshape(KV + LANES, MH * (nope + LANES)).astype(BF16)
            w_v = jnp.concatenate(
                [jnp.concatenate([kv_w_uv.astype(F32), jnp.zeros((KV, MH, LANES), F32)], axis=-1),
                 jnp.concatenate([jnp.zeros((LANES, MH, vd), F32),
                                  jnp.broadcast_to(ones_col[:, None, :], (LANES, MH, LANES))], axis=-1)],
                axis=0).reshape(KV + LANES, MH * (vd + LANES)).astype(BF16)
            one_lane = jnp.where(jnp.arange(KV + LANES) == KV + half, 1.0, 0.0)
            lkp = _round_up(L, 512)
            lat_p = lat[:n_prompt_rows].reshape(B, LP, KV + LANES)
            if lkp >= LP:
                lat_p = jnp.pad(lat_p, ((0, 0), (0, lkp - LP), (0, 0)))
            else:
                lat_p = lat_p[:, :lkp]
            lat_p = (lat_p.reshape(B * lkp, KV + LANES) + one_lane).astype(BF16)
            lks = _round_up(past + ts, LANES)
            cache = jnp.concatenate([cache_ckv.astype(F32), _split_half_layout(cache_krope.astype(F32), half)],
                                    axis=-1)
            lat_s = jnp.concatenate([cache, lat[n_prompt_rows:].reshape(DB, ts, KV + LANES),
                                     jnp.zeros((DB, lks - past - ts, KV + LANES), F32)], axis=1)
            lat_s = (lat_s.reshape(DB * lks, KV + LANES) + one_lane).astype(BF16)
            w_kn_t = jnp.transpose(kv_w_uk, (1, 2, 0)).reshape(MH * nope, KV).astype(BF16)
            lat_s_t = lat_s.T
            big = dict(bm=2048, bn=2048)
            k_slabs = (_matmul(w_k.T, lat_p.T, BF16, bm=2048, bn=1536), _matmul(lat_p, w_v, BF16, bm=1536, bn=2048),
                       _matmul(w_kn_t, lat_s_t, BF16, **big), lat_s_t,
                       _matmul(lat_s[:, :KV], kv_w_uv.reshape(KV, MH * vd).astype(BF16), BF16, **big),
                       lkp, lks)

    def unsplit(r):
        return jnp.concatenate([r[..., :half], r[..., LANES // 2:LANES // 2 + half]], axis=-1)

    lat_p = lat[:n_prompt_rows].reshape(B, LP, KV + LANES)[:, :L]
    lat_s = lat[n_prompt_rows:].reshape(DB, ts, KV + LANES)
    y_prompt = jnp.stack([x[b * LP + n_meta:b * LP + L] for b in range(B)])
    y_sample = x[n_prompt_rows:].reshape(DB, ts, D)
    return (y_prompt, y_sample,
            jnp.stack(new_conv_p), jnp.stack([s[0] for s in new_s]),
            lat_p[..., :KV], unsplit(lat_p[..., KV:]),
            jnp.stack(new_conv_s), jnp.stack([s[1] for s in new_s]),
            lat_s[..., :KV], unsplit(lat_s[..., KV:]))
```

```python
import functools
import math

import jax
import jax.numpy as jnp
from jax import lax
from jax.experimental import pallas as pl
from jax.experimental.pallas import tpu as pltpu

CHUNK = 64
ROPE_THETA = 10000.0
NORM_EPS = 1e-6
NEG_INF = -1e30
LANES = 128
SUBLANES = 8
VMEM_LIMIT_BYTES = 56 * 1024 * 1024
_SPLIT_LEVELS = 2
_CHUNK_SHIFT = CHUNK.bit_length() - 1
assert 1 << _CHUNK_SHIFT == CHUNK

F32 = jnp.float32
BF16 = jnp.bfloat16
_HI = lax.Precision.HIGHEST


def _round_up(x, m):
    return -(-x // m) * m


def _pick_block(dim, target, align):
    best = None
    for d in range(align, min(dim, target) + 1, align):
        if dim % d == 0:
            best = d
    return best if best is not None else dim


def _dot(a, b):
    return jnp.dot(a, b, preferred_element_type=F32)


def _dot_nt(a, b):
    return lax.dot_general(a, b, (((1,), (1,)), ((), ())), preferred_element_type=F32)


def _dot_tn(a, b):
    return lax.dot_general(a, b, (((0,), (0,)), ((), ())), preferred_element_type=F32)


def _sigmoid(x):
    return 1.0 / (1.0 + jnp.exp(-x))


def _params(sem):
    return pltpu.CompilerParams(dimension_semantics=sem, vmem_limit_bytes=VMEM_LIMIT_BYTES)


def _mm_kernel(*refs, nk, epilogue, b_transposed, rope=None):
    n_extra = {"rms": 1, "rope": 2}.get(epilogue, 0)
    a_ref, b_ref = refs[:2]
    extra = refs[2:2 + n_extra]
    o_ref = refs[2 + n_extra]
    rest = refs[3 + n_extra:]

    def finish(acc):
        if epilogue == "relu2":
            r = jnp.maximum(acc, 0.0)
            acc = r * r
        elif epilogue == "rms":
            ms = jnp.mean(acc * acc, axis=-1, keepdims=True)
            acc = acc * lax.rsqrt(ms + NORM_EPS) * extra[0][...]
        elif epilogue == "rope":
            chunks_per_head, rope_chunk, scale = rope
            cos, sin = extra[0][...], extra[1][...]
            cols = []
            for c in range(acc.shape[1] // LANES):
                x = acc[:, c * LANES:(c + 1) * LANES]
                if c % chunks_per_head == rope_chunk:
                    x = x * cos + pltpu.roll(x, LANES // 2, axis=1) * sin
                cols.append(x * scale)
            acc = jnp.concatenate(cols, axis=1)
        o_ref[...] = acc.astype(o_ref.dtype)

    def product():
        b = b_ref[...].astype(BF16)
        return _dot_nt(a_ref[...], b) if b_transposed else _dot(a_ref[...], b)

    if nk == 1:
        finish(product())
    else:
        acc_ref = rest[0]
        k = pl.program_id(2)

        @pl.when(k == 0)
        def _():
            acc_ref[...] = jnp.zeros_like(acc_ref)

        acc_ref[...] += product()

        @pl.when(k == nk - 1)
        def _():
            finish(acc_ref[...])


def _matmul(a, b, out_dtype, *, layer=None, n_use=None, b_transposed=False, epilogue="none", gain=None,
            rope=None, bm=1312, bn=512, bk=4096):
    m, kdim = a.shape
    n = n_use if n_use is not None else b.shape[-2 if b_transposed else -1]
    bm = _pick_block(m, bm, 16)
    bn = n if epilogue == "rms" else _pick_block(n, bn, LANES)
    bk = _pick_block(kdim, bk, LANES)
    nk = kdim // bk
    a_mode = dict(pipeline_mode=pl.Buffered(1)) if nk == 1 and n // bn >= 4 else {}
    b_blk = (bn, bk) if b_transposed else (bk, bn)
    b_idx = (lambda k, j: (j, k)) if b_transposed else (lambda k, j: (k, j))
    if b.ndim == 3:
        b_spec = pl.BlockSpec((None,) + b_blk, lambda i, j, k: (layer,) + b_idx(k, j))
    else:
        b_spec = pl.BlockSpec(b_blk, lambda i, j, k: b_idx(k, j))
    in_specs = [pl.BlockSpec((bm, bk), lambda i, j, k: (i, k), **a_mode), b_spec]
    args = [a, b]
    if epilogue == "rms":
        in_specs.append(pl.BlockSpec((1, bn), lambda i, j, k: (0, j)))
        args.append(gain.reshape(1, n).astype(F32))
    rope_static = None
    if epilogue == "rope":
        cos_t, sin_t, *rope_static = rope
        in_specs += [pl.BlockSpec((bm, LANES), lambda i, j, k: (i, 0))] * 2
        args += [cos_t, sin_t]
        assert bn % (rope_static[0] * LANES) == 0
    scratch = [pltpu.VMEM((bm, bn), F32)] if nk > 1 else []
    return pl.pallas_call(
        functools.partial(_mm_kernel, nk=nk, epilogue=epilogue, b_transposed=b_transposed,
                          rope=tuple(rope_static) if rope_static else None),
        out_shape=jax.ShapeDtypeStruct((m, n), out_dtype),
        grid=(m // bm, n // bn, nk),
        in_specs=in_specs,
        out_specs=pl.BlockSpec((bm, bn), lambda i, j, k: (i, j)),
        scratch_shapes=scratch,
        compiler_params=_params(("parallel", "parallel", "arbitrary")),
        name="matmul_" + epilogue,
    )(*args)


def _ln_kernel(x_ref, h_ref, g_ref, b_ref, of_ref, ob_ref, *, alpha):
    y = alpha * x_ref[...] + h_ref[...].astype(F32)
    mu = jnp.mean(y, axis=-1, keepdims=True)
    d = y - mu
    var = jnp.mean(d * d, axis=-1, keepdims=True)
    out = d * lax.rsqrt(var + NORM_EPS) * g_ref[...] + b_ref[...]
    of_ref[...] = out
    ob_ref[...] = out.astype(BF16)


def _deepnorm(x, h, gain, bias, alpha):
    m, d = x.shape
    br = _pick_block(m, 256, 16)
    row = pl.BlockSpec((br, d), lambda i: (i, 0))
    vec = pl.BlockSpec((1, d), lambda i: (0, 0))
    return pl.pallas_call(
        functools.partial(_ln_kernel, alpha=alpha),
        out_shape=(jax.ShapeDtypeStruct((m, d), F32), jax.ShapeDtypeStruct((m, d), BF16)),
        grid=(m // br,),
        in_specs=[row, row, vec, vec],
        out_specs=(row, row),
        compiler_params=_params(("parallel",)),
        name="deepnorm",
    )(x, h, gain.reshape(1, d), bias.reshape(1, d))


def _kv_post_kernel(lat_ref, g_ref, cos_ref, sin_ref, o_ref, *, kv):
    c = lat_ref[:, :kv]
    ms = jnp.mean(c * c, axis=-1, keepdims=True)
    o_ref[:, :kv] = c * lax.rsqrt(ms + NORM_EPS) * g_ref[...]
    r = lat_ref[:, kv:]
    o_ref[:, kv:] = r * cos_ref[...] + pltpu.roll(r, LANES // 2, axis=1) * sin_ref[...]


def _kv_post(lat, gain, cos_t, sin_t, kv):
    m, w = lat.shape
    br = _pick_block(m, 256, 8)
    return pl.pallas_call(
        functools.partial(_kv_post_kernel, kv=kv),
        out_shape=jax.ShapeDtypeStruct((m, w), F32),
        grid=(m // br,),
        in_specs=[pl.BlockSpec((br, w), lambda i: (i, 0)),
                  pl.BlockSpec((1, kv), lambda i: (0, 0)),
                  pl.BlockSpec((br, LANES), lambda i: (i, 0)),
                  pl.BlockSpec((br, LANES), lambda i: (i, 0))],
        out_specs=pl.BlockSpec((br, w), lambda i: (i, 0)),
        compiler_params=_params(("parallel",)),
        name="kv_post",
    )(lat, gain.reshape(1, kv), cos_t, sin_t)


def _attn_causal_kernel(q_ref, kt_ref, v_ref, o_ref,
                        s_scr, p_scr, m_scr, a_scr, acc_scr, vis_scr, *, bq, bk, rt, n_groups, vd, n_meta, lk_pad):
    qi = pl.program_id(2)
    tiles = [slice(t * rt, (t + 1) * rt) for t in range(bq // rt)]
    lane_chunks = [slice(c * LANES, (c + 1) * LANES) for c in range(bk // LANES)]

    def visible_end(r):
        chunk_id = lax.shift_right_arithmetic(r - n_meta, _CHUNK_SHIFT) + 1
        return jnp.where(r < n_meta, n_meta, chunk_id * CHUNK + n_meta)

    n_full = visible_end(qi * bq) // bk
    nkb = (jnp.minimum(visible_end(qi * bq + bq - 1), lk_pad) + bk - 1) // bk

    for t, rows in enumerate(tiles):
        vis_scr[rows, :] = visible_end(qi * bq + t * rt + lax.broadcasted_iota(jnp.int32, (rt, LANES), 0))
    m_scr[...] = jnp.full(m_scr.shape, NEG_INF, F32)
    acc_scr[...] = jnp.zeros(acc_scr.shape, F32)

    gsz = bq // n_groups
    groups = [slice(g * gsz, (g + 1) * gsz) for g in range(n_groups)]
    tiles_per_group = gsz // rt
    last_block = lk_pad // bk - 1
    last_g = n_groups - 1

    def block_start(j):
        return pl.multiple_of(j * bk, bk)

    def scores(g, j):
        s_scr[groups[g], :] = _dot(q_ref[groups[g], :], kt_ref[:, pl.ds(block_start(j), bk)])

    def values(g, j):
        rows = groups[g]
        a = a_scr[rows, :]
        acc_scr[rows, :] = (jnp.concatenate([a] * (acc_scr.shape[1] // LANES), axis=1) * acc_scr[rows, :]
                            + _dot(p_scr[rows, :], v_ref[pl.ds(block_start(j), bk), :]))

    p_scr[groups[last_g], :] = jnp.zeros((gsz, bk), BF16)
    a_scr[groups[last_g], :] = jnp.ones((gsz, LANES), F32)
    scores(0, 0)

    def step(j, carry, masked):
        start = block_start(j)

        def softmax(g):
            for t in range(g * tiles_per_group, (g + 1) * tiles_per_group):
                rows = tiles[t]
                s = s_scr[rows, :]
                if masked:
                    vis = vis_scr[rows, :]
                    lane = start + lax.broadcasted_iota(jnp.int32, (1, LANES), 1)
                    s = jnp.concatenate([jnp.where(lane + c * LANES < vis, s[:, lc], NEG_INF)
                                         for c, lc in enumerate(lane_chunks)], axis=1)
                m_old = m_scr[rows, :]
                m_new = jnp.maximum(m_old, jnp.max(s, axis=-1, keepdims=True))
                for lc in lane_chunks:
                    p_scr[rows, lc] = jnp.exp2(s[:, lc] - m_new).astype(BF16)
                a_scr[rows, :] = jnp.exp2(m_old - m_new)
                m_scr[rows, :] = m_new

        values(last_g, jnp.maximum(j - 1, 0))
        for g in range(n_groups):
            if g < last_g:
                scores(g + 1, j)
            else:
                scores(0, jnp.minimum(j + 1, last_block))
            softmax(g)
            if g < last_g:
                values(g, j)
        return carry

    lax.fori_loop(0, n_full, functools.partial(step, masked=False), 0)
    lax.fori_loop(n_full, nkb, functools.partial(step, masked=True), 0)
    values(last_g, nkb - 1)
    for rows in tiles:
        o_ref[rows, :] = (acc_scr[rows, :vd] / acc_scr[rows, vd:vd + 1]).astype(o_ref.dtype)


def _attn_full_kernel(q_ref, knt_ref, krt_ref, v_ref, o_ref, *, hb, nope, vd, n_keys):
    dqk = nope + LANES
    lq = q_ref.shape[0]
    heads = range(hb)
    kpos = lax.broadcasted_iota(jnp.int32, (1, knt_ref.shape[1]), 1)
    qs = [q_ref[:, h * dqk:(h + 1) * dqk] for h in heads]
    s_rope = _dot(jnp.concatenate([q[:, nope:] for q in qs], axis=0), krt_ref[...])
    ss = [jnp.where(kpos < n_keys,
                    _dot(qs[h][:, :nope], knt_ref[h * nope:(h + 1) * nope, :]) + s_rope[h * lq:(h + 1) * lq],
                    NEG_INF) for h in heads]
    es = [jnp.exp2(s - jnp.max(s, axis=-1, keepdims=True)) for s in ss]
    ls = [jnp.sum(e, axis=-1, keepdims=True) for e in es]
    pv = [_dot(es[h].astype(BF16), v_ref[:, h * vd:(h + 1) * vd]) for h in heads]
    for h in heads:
        o_ref[:, h * vd:(h + 1) * vd] = (pv[h] / ls[h]).astype(o_ref.dtype)


def _attention_causal(q, kt, vf, *, nb, lq, lk_pad, heads, nope, vd, bq, bk, n_meta):
    dqk = nope + LANES
    nq = lq // bq
    rt = 32 if bq % 32 == 0 else bq
    n_groups = 4 if bq % (4 * rt) == 0 else 2
    assert bq % (n_groups * rt) == 0
    return pl.pallas_call(
        functools.partial(_attn_causal_kernel, bq=bq, bk=bk, rt=rt, n_groups=n_groups, vd=vd,
                          n_meta=n_meta, lk_pad=lk_pad),
        out_shape=jax.ShapeDtypeStruct((q.shape[0], heads * vd), BF16),
        grid=(nb, heads, nq),
        in_specs=[pl.BlockSpec((bq, dqk), lambda b, h, i: (b * nq + i, h)),
                  pl.BlockSpec((dqk, lk_pad), lambda b, h, i: (h, b)),
                  pl.BlockSpec((lk_pad, vd + LANES), lambda b, h, i: (b, h))],
        out_specs=pl.BlockSpec((bq, vd), lambda b, h, i: (b * nq + i, h)),
        scratch_shapes=[pltpu.VMEM((bq, bk), F32), pltpu.VMEM((bq, bk), BF16),
                        pltpu.VMEM((bq, LANES), F32), pltpu.VMEM((bq, LANES), F32),
                        pltpu.VMEM((bq, vd + LANES), F32), pltpu.VMEM((bq, LANES), jnp.int32)],
        compiler_params=_params(("parallel", "parallel", "arbitrary")),
        name="mla_attention_causal",
    )(q, kt, vf)


def _attention_full(q, knt, lat_t, vf, o_all, *, nb, lq, lk_pad, row0, heads, hb, nope, vd, kv, n_keys):
    dqk = nope + LANES
    blk0 = row0 // lq

    def body(q_ref, knt_ref, krt_ref, v_ref, o_all_ref, o_ref):
        _attn_full_kernel(q_ref, knt_ref, krt_ref, v_ref, o_ref, hb=hb, nope=nope, vd=vd, n_keys=n_keys)

    return pl.pallas_call(
        body,
        out_shape=jax.ShapeDtypeStruct(o_all.shape, o_all.dtype),
        grid=(nb, heads // hb),
        in_specs=[pl.BlockSpec((lq, hb * dqk), lambda b, g: (blk0 + b, g)),
                  pl.BlockSpec((hb * nope, lk_pad), lambda b, g: (g, b)),
                  pl.BlockSpec((LANES, lk_pad), lambda b, g: (kv // LANES, b)),
                  pl.BlockSpec((lk_pad, hb * vd), lambda b, g: (b, g)),
                  pl.BlockSpec(memory_space=pl.ANY)],
        out_specs=pl.BlockSpec((lq, hb * vd), lambda b, g: (blk0 + b, g)),
        input_output_aliases={4: 0},
        compiler_params=_params(("parallel", "parallel")),
        name="mla_attention_full",
    )(q, knt, lat_t, vf, o_all)


def _item_info(i, n_p, cps, cs, l_valid):
    is_p = i < n_p
    j = i - n_p
    seq = jnp.where(is_p, i // cps, n_p // cps + j // cs)
    c = jnp.where(is_p, i % cps, j % cs)
    nc = jnp.where(is_p, cps, cs)
    nvalid = jnp.where(is_p, jnp.clip(l_valid - c * CHUNK, 0, CHUNK), CHUNK)
    return seq, c, nc, nvalid


def _row_sums_of_squares(xs):
    rows, n = xs[0].shape
    parts = []
    for x in xs:
        sq = x * x
        hi = sq.astype(BF16)
        parts.append(jnp.concatenate([hi, (sq - hi.astype(F32)).astype(BF16)], axis=1))
    tot = _dot(jnp.concatenate(parts, axis=0), jnp.ones((2 * n, n), BF16))
    return [tot[i * rows:(i + 1) * rows] for i in range(len(xs))]


def _tri_inverse_many(mats):
    n = mats[0].shape[0]
    r = lax.broadcasted_iota(jnp.int32, (n, n), 0)
    c = lax.broadcasted_iota(jnp.int32, (n, n), 1)
    eye = jnp.where(r == c, 1.0, 0.0)
    zero = jnp.zeros((n, n), F32)

    def split(x):
        hi = x.astype(BF16).astype(F32)
        return hi, x - hi

    def lhs_of(x):
        hi, lo = split(x)
        return jnp.concatenate([hi, lo, hi, zero], axis=1).astype(BF16)

    def rhs_of(x):
        hi, lo = split(x)
        return jnp.concatenate([hi, hi, lo, zero], axis=0).astype(BF16)

    ps = [eye - a for a in mats]
    aks = list(mats)
    rhs = [rhs_of(a) for a in mats]
    power, level = 2, 0
    while power < n:
        if level < _SPLIT_LEVELS:
            aks = [_dot(lhs_of(ak), rk) for ak, rk in zip(aks, rhs)]
            rhs = [rhs_of(ak) for ak in aks]
            ps = [p + _dot(lhs_of(p), rk) for p, rk in zip(ps, rhs)]
        else:
            akb = [ak.astype(BF16) for ak in aks]
            aks = [_dot(b, b) for b in akb]
            ps = [p + _dot(p.astype(BF16), ak.astype(BF16)) for p, ak in zip(ps, aks)]
        power *= 2
        level += 1
    return ps


def _gdn_kernel(qc_ref, kc_ref, vc_ref, qp_ref, kp_ref, vp_ref, q0_ref, k0_ref, v0_ref,
                wq_ref, wk_ref, wv_ref, ab_ref, alog_ref, dtb_ref, shift_ref, z_ref, gain_ref, s0_ref,
                o_ref, snew_ref, srun_ref,
                val_s, kcd_s, qdec_s, kdec_s, qk_s, eg_s, s_ref,
                *, hg, dk, n_items, n_p, cps, cs, l_valid, width):
    C = CHUNK
    t = pl.program_id(1)
    i = jnp.minimum(t, n_items - 1)
    j = jnp.maximum(t - 1, 0)
    _, c, _, nvalid = _item_info(i, n_p, cps, cs, l_valid)
    _, cj, ncj, _ = _item_info(j, n_p, cps, cs, l_valid)
    first = c == 0
    heads = range(hg)
    sls = [slice(h * dk, (h + 1) * dk) for h in heads]

    @pl.when(t == 0)
    def _():
        for ref in (val_s, kcd_s, qdec_s, kdec_s, qk_s, eg_s):
            ref[...] = jnp.zeros(ref.shape, ref.dtype)

    n_new = n_p // cps
    seq_j = _item_info(j, n_p, cps, cs, l_valid)[0]

    @pl.when(cj == 0)
    def _():
        s_ref[...] = jnp.where(seq_j < n_new, 0.0, s0_ref[0])

    sb = [s_ref[h].astype(BF16) for h in heads]
    ks_s = [_dot(kcd_s[:, sls[h]], sb[h]) for h in heads]
    qs_s = [_dot(qdec_s[:, sls[h]], sb[h]) for h in heads]

    halo = SUBLANES
    kpad = shift_ref.shape[1] // 2
    shift = shift_ref[...]

    def conv_silu(cur_ref, prev_ref, init_ref, w_ref):
        cur = cur_ref[...]
        rows = jnp.concatenate([jnp.where(first, init_ref[0], prev_ref[...]), cur,
                                jnp.zeros((kpad - halo - C, cur.shape[1]), F32)], axis=0)
        hi = rows.astype(BF16)
        lo = (rows - hi.astype(F32)).astype(BF16)
        shifted = _dot(shift, jnp.concatenate([hi, lo], axis=0))
        y = cur * w_ref[width - 1:width, :]
        for t in range(width - 1):
            y = y + shifted[t * C:(t + 1) * C] * w_ref[t:t + 1, :]
        return y * _sigmoid(y)

    qs = conv_silu(qc_ref, qp_ref, q0_ref, wq_ref)
    ks = conv_silu(kc_ref, kp_ref, k0_ref, wk_ref)
    vs = conv_silu(vc_ref, vp_ref, v0_ref, wv_ref)

    ub = [(val_s[:, sls[h]] - ks_s[h]).astype(BF16) for h in heads]
    ds_s = [_dot_tn(kdec_s[:, sls[h]], ub[h]) for h in heads]
    os_s = [qs_s[h] + _dot(qk_s[:, h * C:(h + 1) * C].astype(BF16), ub[h]) for h in heads]

    ab = ab_ref[...]
    rows = lax.broadcasted_iota(jnp.int32, (C, LANES), 0)
    valid = rows < nvalid
    xs = ab + dtb_ref[0]
    softplus = jnp.maximum(xs, 0.0) + jnp.log(1.0 + jnp.exp(-jnp.abs(xs)))
    g = jnp.where(valid, -jnp.exp(alog_ref[0]) * softplus, 0.0)
    beta = jnp.where(valid, _sigmoid(ab), 0.0)
    r = lax.broadcasted_iota(jnp.int32, (C, C), 0)
    cc = lax.broadcasted_iota(jnp.int32, (C, C), 1)
    incl = r >= cc
    strict = r > cc
    gcum = jnp.dot(jnp.where(incl, 1.0, 0.0), g, precision=_HI, preferred_element_type=F32)
    gcum_t = gcum.T

    q_ssq = _row_sums_of_squares([qs[:, sl] for sl in sls])
    k_ssq = _row_sums_of_squares([ks[:, sl] for sl in sls])
    qn = [qs[:, sls[h]] * lax.rsqrt(q_ssq[h] + NORM_EPS) * (dk ** -0.5) for h in heads]
    kn = [ks[:, sls[h]] * lax.rsqrt(k_ssq[h] + NORM_EPS) for h in heads]
    bcol = [beta[:, hg + h:hg + h + 1] for h in heads]
    gcol = [gcum[:, h:h + 1] for h in heads]
    glast = [gcum[C - 1:C, h:h + 1] for h in heads]
    eg = [jnp.exp(gcol[h]) for h in heads]
    decay = [jnp.where(incl, jnp.exp(jnp.where(incl, gcol[h] - gcum_t[h:h + 1, :], 0.0)), 0.0) for h in heads]
    kb = [kn[h] * bcol[h] for h in heads]
    knb = [kn[h].astype(BF16) for h in heads]
    a_mat = [jnp.where(strict, _dot_nt(kb[h].astype(BF16), knb[h]) * decay[h], 0.0) for h in heads]

    gain = gain_ref[...]
    for h in heads:
        s_ref[h] = s_ref[h] * eg_s[h:h + 1, :] + ds_s[h]
    for h in heads:
        o = os_s[h]
        z = z_ref[:, sls[h]]
        ms = jnp.mean(o * o, axis=-1, keepdims=True)
        o_ref[:, sls[h]] = (o * lax.rsqrt(ms + NORM_EPS) * gain * (z * _sigmoid(z))).astype(o_ref.dtype)

    t_mat = _tri_inverse_many(a_mat)
    rhs = [jnp.concatenate([vs[:, sls[h]] * bcol[h], kb[h] * eg[h]], axis=1).astype(BF16) for h in heads]
    tv = [_dot(t_mat[h].astype(BF16), rhs[h]) for h in heads]
    qk = [_dot_nt(qn[h].astype(BF16), knb[h]) * decay[h] for h in heads]
    for h in heads:
        val_s[:, sls[h]] = tv[h][:, :dk]
        kcd_s[:, sls[h]] = tv[h][:, dk:].astype(BF16)
        qdec_s[:, sls[h]] = (qn[h] * eg[h]).astype(BF16)
        kdec_s[:, sls[h]] = (kn[h] * jnp.exp(glast[h] - gcol[h])).astype(BF16)
        eg_s[h:h + 1, :] = jnp.broadcast_to(jnp.exp(glast[h]), (1, LANES))
    qk_s[...] = jnp.concatenate(qk, axis=1)

    done = (cj == ncj - 1) & (t > 0)

    @pl.when(done & (seq_j < n_new))
    def _():
        snew_ref[0] = s_ref[...]

    @pl.when(done & (seq_j >= n_new))
    def _():
        srun_ref[0] = s_ref[...]


def _gdn_mixer(proj, gates, conv0, s0, conv_w, a_log, dt_bias, o_gain, *, heads, dk, hg, n_p, cps, cs, l_valid):
    m = proj.shape[0]
    C = CHUNK
    w = heads * dk
    n_hg = heads // hg
    gw = hg * dk
    n_items = m // C
    n_new, n_run = n_p // cps, s0.shape[0]
    width = conv_w.shape[0]
    info = dict(n_p=n_p, cps=cps, cs=cs, l_valid=l_valid)

    def seq_of(i):
        return _item_info(i, n_p, cps, cs, l_valid)[0]

    prep = lambda t: jnp.minimum(t, n_items - 1)
    scan = lambda t: jnp.maximum(t - 1, 0)
    cur = lambda off: pl.BlockSpec((C, gw), lambda g, t: (prep(t), off * n_hg + g))
    prev = lambda off: pl.BlockSpec(
        (SUBLANES, gw), lambda g, t: (jnp.maximum(prep(t) * (C // SUBLANES) - 1, 0), off * n_hg + g))
    init = lambda off: pl.BlockSpec((1, SUBLANES, gw), lambda g, t: (seq_of(prep(t)), 0, off * n_hg + g))
    wspec = lambda off: pl.BlockSpec((width, gw), lambda g, t: (0, off * n_hg + g))
    gate_vec = pl.BlockSpec((1, 1, LANES), lambda g, t: (g, 0, 0))
    state_new = pl.BlockSpec((1, hg, dk, dk), lambda g, t: (jnp.minimum(seq_of(scan(t)), n_new - 1), g, 0, 0))
    state_run = pl.BlockSpec((1, hg, dk, dk), lambda g, t: (jnp.maximum(seq_of(scan(t)) - n_new, 0), g, 0, 0))

    kpad = _round_up(SUBLANES + C, LANES)
    r = jnp.arange((width - 1) * C)
    src = (r % C) + SUBLANES - (width - 1) + r // C
    col = jnp.arange(2 * kpad)
    shift = ((col[None, :] == src[:, None]) | (col[None, :] == src[:, None] + kpad)).astype(BF16)

    alog_g = jnp.pad(a_log.reshape(n_hg, 1, hg).astype(F32), ((0, 0), (0, 0), (0, LANES - hg)))
    dtb_g = jnp.pad(dt_bias.reshape(n_hg, 1, hg).astype(F32), ((0, 0), (0, 0), (0, LANES - hg)))

    return pl.pallas_call(
        functools.partial(_gdn_kernel, hg=hg, dk=dk, width=width, n_items=n_items, **info),
        out_shape=(jax.ShapeDtypeStruct((m, w), BF16),
                   jax.ShapeDtypeStruct((n_new, heads, dk, dk), F32),
                   jax.ShapeDtypeStruct((n_run, heads, dk, dk), F32)),
        grid=(n_hg, n_items + 1),
        in_specs=[cur(0), cur(1), cur(2), prev(0), prev(1), prev(2), init(0), init(1), init(2),
                  wspec(0), wspec(1), wspec(2),
                  pl.BlockSpec((C, LANES), lambda g, t: (prep(t), g)),
                  gate_vec, gate_vec,
                  pl.BlockSpec(shift.shape, lambda g, t: (0, 0)),
                  pl.BlockSpec((C, gw), lambda g, t: (scan(t), 3 * n_hg + g)),
                  pl.BlockSpec((1, dk), lambda g, t: (0, 0)),
                  state_run],
        out_specs=(pl.BlockSpec((C, gw), lambda g, t: (scan(t), g)), state_new, state_run),
        scratch_shapes=[pltpu.VMEM((C, gw), F32), pltpu.VMEM((C, gw), BF16), pltpu.VMEM((C, gw), BF16),
                        pltpu.VMEM((C, gw), BF16), pltpu.VMEM((C, hg * C), F32), pltpu.VMEM((hg, LANES), F32),
                        pltpu.VMEM((hg, dk, dk), F32)],
        compiler_params=_params(("parallel", "arbitrary")),
        name="gdn_mixer",
    )(proj, proj, proj, proj, proj, proj, conv0, conv0, conv0, conv_w, conv_w, conv_w,
      gates, alog_g, dtb_g, shift, proj, o_gain.reshape(1, dk).astype(F32), s0)


def _split_half_layout(x, half):
    pad = [(0, 0)] * (x.ndim - 1) + [(0, LANES // 2 - half)]
    return jnp.concatenate([jnp.pad(x[..., :half], pad), jnp.pad(x[..., half:], pad)], axis=-1)


def _rope_tables(pos, half):
    inv_freq = 1.0 / (ROPE_THETA ** (jnp.arange(half, dtype=F32) / half))
    ang = pos[:, None] * inv_freq[None, :]
    cos, sin = jnp.cos(ang), jnp.sin(ang)
    cos_t = _split_half_layout(jnp.concatenate([cos, cos], -1), half)
    sin_t = _split_half_layout(jnp.concatenate([-sin, sin], -1), half)
    return cos_t, sin_t


def kernel(x_prompt, x_sample, state_conv, state_delta, cache_ckv, cache_krope, meta_tokens,
           a_w_in, a_conv_w, a_a_log, a_dt_bias, a_o_gain, a_w_o,
           b_w_dq, b_q_gain, b_w_uq, b_w_o,
           kv_w_dkv, kv_gain, kv_w_uk, kv_w_uv,
           mlp_w_up, mlp_w_down, ln_gain, ln_bias):
    B, seq, D = x_prompt.shape
    DB, ts, _ = x_sample.shape
    n_meta = meta_tokens.shape[0]
    depth = ln_gain.shape[0]
    n_a = a_w_in.shape[0]
    H = a_a_log.shape[1]
    dk = a_o_gain.shape[1]
    W = H * dk
    width = a_conv_w.shape[1]
    past = cache_ckv.shape[1]
    KV, MH, nope = kv_w_uk.shape
    vd = kv_w_uv.shape[2]
    rope = cache_krope.shape[2]
    half = rope // 2
    alpha = (2 * depth) ** 0.25
    scale = (nope + rope) ** -0.5
    assert ts % CHUNK == 0 and ts >= width - 1 and half < LANES // 2
    assert nope % LANES == 0 and vd % LANES == 0 and KV % LANES == 0 and dk % LANES == 0

    L = n_meta + seq
    LP = _round_up(L, LANES)
    n_prompt_rows = B * LP
    M = n_prompt_rows + DB * ts
    cps, cs = LP // CHUNK, ts // CHUNK
    n_p = B * cps
    hg = min(32, H)
    n_hg = H // hg

    meta = jnp.broadcast_to(meta_tokens.astype(F32)[None], (B, n_meta, D))
    xp = jnp.concatenate([meta, x_prompt, jnp.zeros((B, LP - L, D), F32)], axis=1)
    x = jnp.concatenate([xp.reshape(n_prompt_rows, D), x_sample.reshape(DB * ts, D)], axis=0)
    xb = x.astype(BF16)

    pos = jnp.concatenate([jnp.tile(jnp.arange(LP, dtype=F32), B),
                           jnp.tile(past + jnp.arange(ts, dtype=F32), DB)])
    cos_t, sin_t = _rope_tables(pos, half)

    w_in_t = jnp.swapaxes(a_w_in, 1, 2)

    conv0 = jnp.concatenate([jnp.zeros((n_a, B, width - 1, 3 * W), F32), state_conv.astype(F32)], axis=1)
    conv0 = jnp.pad(conv0, ((0, 0), (0, 0), (SUBLANES - (width - 1), 0), (0, 0)))
    s0 = state_delta.astype(F32)

    new_conv_p, new_conv_s, new_s = [], [], []
    k_slabs = None
    lat = None
    for layer in range(depth):
        if layer < n_a:
            w_ab = a_w_in[layer, :, 4 * W:]
            gate_w = jnp.concatenate(
                [w_ab[:, :H].reshape(D, n_hg, hg), w_ab[:, H:].reshape(D, n_hg, hg),
                 jnp.zeros((D, n_hg, LANES - 2 * hg), w_ab.dtype)], axis=-1).reshape(D, n_hg * LANES)
            proj = _matmul(xb, w_in_t, F32, layer=layer, n_use=4 * W, b_transposed=True)
            gates = _matmul(xb, gate_w.astype(BF16), F32, bm=328)
            o, s_new, s_run = _gdn_mixer(proj, gates, conv0[layer], s0[layer], a_conv_w[layer].astype(F32),
                                         a_a_log[layer], a_dt_bias[layer], a_o_gain[layer],
                                         heads=H, dk=dk, hg=hg, n_p=n_p, cps=cps, cs=cs, l_valid=L)
            h = _matmul(o, a_w_o, BF16, layer=layer)
            new_conv_p.append(jnp.stack(
                [proj[b * LP + L - (width - 1):b * LP + L, :3 * W] for b in range(B)]))
            new_conv_s.append(jnp.stack(
                [proj[n_prompt_rows + ts - (width - 1) + t::ts, :3 * W] for t in range(width - 1)], axis=1))
            new_s.append((s_new, s_run))
        else:
            j = layer - n_a
            ql = _matmul(xb, b_w_dq[j].astype(BF16), BF16, epilogue="rms", gain=b_q_gain[j])
            w_uq = b_w_uq[j].reshape(-1, MH, nope + rope)
            w_uq = jnp.concatenate([w_uq[..., :nope], _split_half_layout(w_uq[..., nope:], half)], axis=-1)
            q = _matmul(ql, w_uq.reshape(-1, MH * (nope + LANES)).astype(BF16), BF16, bn=2048, epilogue="rope",
                        rope=(cos_t, sin_t, nope // LANES + 1, nope // LANES, scale * math.log2(math.e)))
            kp, vp, ks, ls_t, vs, lkp, lks = k_slabs
            o = _attention_causal(q, kp, vp, nb=B, lq=LP, lk_pad=lkp, heads=MH, nope=nope,
                                  vd=vd, bq=_pick_block(LP, 1408, LANES), bk=512, n_meta=n_meta)
            o = _attention_full(q, ks, ls_t, vs, o, nb=DB, lq=ts, lk_pad=lks, row0=n_prompt_rows,
                                heads=MH, hb=min(8, MH), nope=nope, vd=vd, kv=KV, n_keys=past + ts)
            h = _matmul(o, b_w_o, BF16, layer=j)

        x, xb = _deepnorm(x, h, ln_gain[layer, 0], ln_bias[layer, 0], alpha)
        hid = _matmul(xb, mlp_w_up, BF16, layer=layer, epilogue="relu2")
        h = _matmul(hid, mlp_w_down, BF16, layer=layer, bn=1024, bk=2048)
        x, xb = _deepnorm(x, h, ln_gain[layer, 1], ln_bias[layer, 1], alpha)

        if layer == n_a - 1:
            w_dkv = jnp.concatenate([kv_w_dkv[:, :KV], _split_half_layout(kv_w_dkv[:, KV:], half)], axis=1)
            lat = _kv_post(_matmul(xb, w_dkv.astype(BF16), F32, bm=328, bn=640), kv_gain.astype(F32),
                           cos_t, sin_t, KV)
            lane_ids = jnp.arange(LANES)
            eye = jnp.where((lane_ids[:, None] == lane_ids[None, :]) & (lane_ids[:, None] != half), 1.0, 0.0)
            ones_col = jnp.where((lane_ids[:, None] == half) & (lane_ids[None, :] == 0), 1.0, 0.0)
            w_k = jnp.concatenate(
                [jnp.concatenate([kv_w_uk.astype(F32), jnp.zeros((KV, MH, LANES), F32)], axis=-1),
                 jnp.concatenate([jnp.zeros((LANES, MH, nope), F32),
                                  jnp.broadcast_to(eye[:, None, :], (LANES, MH, LANES))], axis=-1)],
                axis=0).reshape(KV + LANES, MH * (nope + LANES)).astype(BF16)
            w_v = jnp.concatenate(
                [jnp.concatenate([kv_w_uv.astype(F32), jnp.zeros((KV, MH, LANES), F32)], axis=-1),
                 jnp.concatenate([jnp.zeros((LANES, MH, vd), F32),
                                  jnp.broadcast_to(ones_col[:, None, :], (LANES, MH, LANES))], axis=-1)],
                axis=0).reshape(KV + LANES, MH * (vd + LANES)).astype(BF16)
            one_lane = jnp.where(jnp.arange(KV + LANES) == KV + half, 1.0, 0.0)
            lkp = _round_up(L, 512)
            lat_p = lat[:n_prompt_rows].reshape(B, LP, KV + LANES)
            if lkp >= LP:
                lat_p = jnp.pad(lat_p, ((0, 0), (0, lkp - LP), (0, 0)))
            else:
                lat_p = lat_p[:, :lkp]
            lat_p = (lat_p.reshape(B * lkp, KV + LANES) + one_lane).astype(BF16)
            lks = _round_up(past + ts, LANES)
            cache = jnp.concatenate([cache_ckv.astype(F32), _split_half_layout(cache_krope.astype(F32), half)],
                                    axis=-1)
            lat_s = jnp.concatenate([cache, lat[n_prompt_rows:].reshape(DB, ts, KV + LANES),
                                     jnp.zeros((DB, lks - past - ts, KV + LANES), F32)], axis=1)
            lat_s = (lat_s.reshape(DB * lks, KV + LANES) + one_lane).astype(BF16)
            w_kn_t = jnp.transpose(kv_w_uk, (1, 2, 0)).reshape(MH * nope, KV).astype(BF16)
            lat_s_t = lat_s.T
            big = dict(bm=2048, bn=2048)
            k_slabs = (_matmul(w_k.T, lat_p.T, BF16, bm=2048, bn=1536), _matmul(lat_p, w_v, BF16, bm=1536, bn=2048),
                       _matmul(w_kn_t, lat_s_t, BF16, **big), lat_s_t,
                       _matmul(lat_s[:, :KV], kv_w_uv.reshape(KV, MH * vd).astype(BF16), BF16, **big),
                       lkp, lks)

    def unsplit(r):
        return jnp.concatenate([r[..., :half], r[..., LANES // 2:LANES // 2 + half]], axis=-1)

    lat_p = lat[:n_prompt_rows].reshape(B, LP, KV + LANES)[:, :L]
    lat_s = lat[n_prompt_rows:].reshape(DB, ts, KV + LANES)
    y_prompt = jnp.stack([x[b * LP + n_meta:b * LP + L] for b in range(B)])
    y_sample = x[n_prompt_rows:].reshape(DB, ts, D)
    return (y_prompt, y_sample,
            jnp.stack(new_conv_p), jnp.stack([s[0] for s in new_s]),
            lat_p[..., :KV], unsplit(lat_p[..., KV:]),
            jnp.stack(new_conv_s), jnp.stack([s[1] for s in new_s]),
            lat_s[..., :KV], unsplit(lat_s[..., KV:]))
```

```python
import functools
import math

import jax
import jax.numpy as jnp
from jax import lax
from jax.experimental import pallas as pl
from jax.experimental.pallas import tpu as pltpu

CHUNK = 64
ROPE_THETA = 10000.0
NORM_EPS = 1e-6
NEG_INF = -1e30
LANES = 128
SUBLANES = 8
VMEM_LIMIT_BYTES = 56 * 1024 * 1024
_SPLIT_LEVELS = 2
_CHUNK_SHIFT = CHUNK.bit_length() - 1
assert 1 << _CHUNK_SHIFT == CHUNK

F32 = jnp.float32
BF16 = jnp.bfloat16
_HI = lax.Precision.HIGHEST


def _round_up(x, m):
    return -(-x // m) * m


def _pick_block(dim, target, align):
    best = None
    for d in range(align, min(dim, target) + 1, align):
        if dim % d == 0:
            best = d
    return best if best is not None else dim


def _dot(a, b):
    return jnp.dot(a, b, preferred_element_type=F32)


def _dot_nt(a, b):
    return lax.dot_general(a, b, (((1,), (1,)), ((), ())), preferred_element_type=F32)


def _dot_tn(a, b):
    return lax.dot_general(a, b, (((0,), (0,)), ((), ())), preferred_element_type=F32)


def _sigmoid(x):
    return 1.0 / (1.0 + jnp.exp(-x))


def _params(sem):
    return pltpu.CompilerParams(dimension_semantics=sem, vmem_limit_bytes=VMEM_LIMIT_BYTES)


def _mm_kernel(*refs, nk, epilogue, b_transposed, rope=None):
    n_extra = {"rms": 1, "rope": 2}.get(epilogue, 0)
    a_ref, b_ref = refs[:2]
    extra = refs[2:2 + n_extra]
    o_ref = refs[2 + n_extra]
    rest = refs[3 + n_extra:]

    def finish(acc):
        if epilogue == "relu2":
            r = jnp.maximum(acc, 0.0)
            acc = r * r
        elif epilogue == "rms":
            ms = jnp.mean(acc * acc, axis=-1, keepdims=True)
            acc = acc * lax.rsqrt(ms + NORM_EPS) * extra[0][...]
        elif epilogue == "rope":
            chunks_per_head, rope_chunk, scale = rope
            cos, sin = extra[0][...], extra[1][...]
            cols = []
            for c in range(acc.shape[1] // LANES):
                x = acc[:, c * LANES:(c + 1) * LANES]
                if c % chunks_per_head == rope_chunk:
                    x = x * cos + pltpu.roll(x, LANES // 2, axis=1) * sin
                cols.append(x * scale)
            acc = jnp.concatenate(cols, axis=1)
        o_ref[...] = acc.astype(o_ref.dtype)

    def product():
        b = b_ref[...].astype(BF16)
        return _dot_nt(a_ref[...], b) if b_transposed else _dot(a_ref[...], b)

    if nk == 1:
        finish(product())
    else:
        acc_ref = rest[0]
        k = pl.program_id(2)

        @pl.when(k == 0)
        def _():
            acc_ref[...] = jnp.zeros_like(acc_ref)

        acc_ref[...] += product()

        @pl.when(k == nk - 1)
        def _():
            finish(acc_ref[...])


def _matmul(a, b, out_dtype, *, layer=None, n_use=None, b_transposed=False, epilogue="none", gain=None,
            rope=None, bm=1312, bn=512, bk=4096):
    m = a.shape[0]
    kdim = min(a.shape[1], b.shape[-1 if b_transposed else -2])
    n = n_use if n_use is not None else b.shape[-2 if b_transposed else -1]
    bm = _pick_block(m, bm, 16)
    bn = n if epilogue == "rms" else _pick_block(n, bn, LANES)
    bk = _pick_block(kdim, bk, LANES)
    nk = kdim // bk
    a_mode = dict(pipeline_mode=pl.Buffered(1)) if nk == 1 and n // bn >= 4 else {}
    b_blk = (bn, bk) if b_transposed else (bk, bn)
    b_idx = (lambda k, j: (j, k)) if b_transposed else (lambda k, j: (k, j))
    if b.ndim == 3:
        b_spec = pl.BlockSpec((None,) + b_blk, lambda i, j, k: (layer,) + b_idx(k, j))
    else:
        b_spec = pl.BlockSpec(b_blk, lambda i, j, k: b_idx(k, j))
    in_specs = [pl.BlockSpec((bm, bk), lambda i, j, k: (i, k), **a_mode), b_spec]
    args = [a, b]
    if epilogue == "rms":
        in_specs.append(pl.BlockSpec((1, bn), lambda i, j, k: (0, j)))
        args.append(gain.reshape(1, n).astype(F32))
    rope_static = None
    if epilogue == "rope":
        cos_t, sin_t, *rope_static = rope
        in_specs += [pl.BlockSpec((bm, LANES), lambda i, j, k: (i, 0))] * 2
        args += [cos_t, sin_t]
        assert bn % (rope_static[0] * LANES) == 0
    scratch = [pltpu.VMEM((bm, bn), F32)] if nk > 1 else []
    return pl.pallas_call(
        functools.partial(_mm_kernel, nk=nk, epilogue=epilogue, b_transposed=b_transposed,
                          rope=tuple(rope_static) if rope_static else None),
        out_shape=jax.ShapeDtypeStruct((m, n), out_dtype),
        grid=(m // bm, n // bn, nk),
        in_specs=in_specs,
        out_specs=pl.BlockSpec((bm, bn), lambda i, j, k: (i, j)),
        scratch_shapes=scratch,
        compiler_params=_params(("parallel", "parallel", "arbitrary")),
        name="matmul_" + epilogue,
    )(*args)


def _ln_kernel(x_ref, h_ref, g_ref, b_ref, of_ref, ob_ref, *, alpha):
    y = alpha * x_ref[...] + h_ref[...].astype(F32)
    mu = jnp.mean(y, axis=-1, keepdims=True)
    d = y - mu
    var = jnp.mean(d * d, axis=-1, keepdims=True)
    out = d * lax.rsqrt(var + NORM_EPS) * g_ref[...] + b_ref[...]
    of_ref[...] = out
    ob_ref[...] = out.astype(BF16)


def _deepnorm(x, h, gain, bias, alpha):
    m, d = x.shape
    br = _pick_block(m, 256, 16)
    row = pl.BlockSpec((br, d), lambda i: (i, 0))
    vec = pl.BlockSpec((1, d), lambda i: (0, 0))
    return pl.pallas_call(
        functools.partial(_ln_kernel, alpha=alpha),
        out_shape=(jax.ShapeDtypeStruct((m, d), F32), jax.ShapeDtypeStruct((m, d), BF16)),
        grid=(m // br,),
        in_specs=[row, row, vec, vec],
        out_specs=(row, row),
        compiler_params=_params(("parallel",)),
        name="deepnorm",
    )(x, h, gain.reshape(1, d), bias.reshape(1, d))


def _kv_post_kernel(lat_ref, g_ref, cos_ref, sin_ref, o_ref, *, kv):
    c = lat_ref[:, :kv]
    ms = jnp.mean(c * c, axis=-1, keepdims=True)
    o_ref[:, :kv] = c * lax.rsqrt(ms + NORM_EPS) * g_ref[...]
    r = lat_ref[:, kv:]
    o_ref[:, kv:] = r * cos_ref[...] + pltpu.roll(r, LANES // 2, axis=1) * sin_ref[...]


def _kv_post(lat, gain, cos_t, sin_t, kv):
    m, w = lat.shape
    br = _pick_block(m, 256, 8)
    return pl.pallas_call(
        functools.partial(_kv_post_kernel, kv=kv),
        out_shape=jax.ShapeDtypeStruct((m, w), F32),
        grid=(m // br,),
        in_specs=[pl.BlockSpec((br, w), lambda i: (i, 0)),
                  pl.BlockSpec((1, kv), lambda i: (0, 0)),
                  pl.BlockSpec((br, LANES), lambda i: (i, 0)),
                  pl.BlockSpec((br, LANES), lambda i: (i, 0))],
        out_specs=pl.BlockSpec((br, w), lambda i: (i, 0)),
        compiler_params=_params(("parallel",)),
        name="kv_post",
    )(lat, gain.reshape(1, kv), cos_t, sin_t)


def _attn_causal_kernel(q_ref, kt_ref, v_ref, o_ref,
                        s_scr, p_scr, m_scr, a_scr, acc_scr, vis_scr, *, bq, bk, rt, n_groups, vd, n_meta, lk_pad):
    qi = pl.program_id(2)
    tiles = [slice(t * rt, (t + 1) * rt) for t in range(bq // rt)]
    lane_chunks = [slice(c * LANES, (c + 1) * LANES) for c in range(bk // LANES)]

    def visible_end(r):
        chunk_id = lax.shift_right_arithmetic(r - n_meta, _CHUNK_SHIFT) + 1
        return jnp.where(r < n_meta, n_meta, chunk_id * CHUNK + n_meta)

    n_full = visible_end(qi * bq) // bk
    nkb = (jnp.minimum(visible_end(qi * bq + bq - 1), lk_pad) + bk - 1) // bk

    for t, rows in enumerate(tiles):
        vis_scr[rows, :] = visible_end(qi * bq + t * rt + lax.broadcasted_iota(jnp.int32, (rt, LANES), 0))
    m_scr[...] = jnp.full(m_scr.shape, NEG_INF, F32)
    acc_scr[...] = jnp.zeros(acc_scr.shape, F32)

    gsz = bq // n_groups
    groups = [slice(g * gsz, (g + 1) * gsz) for g in range(n_groups)]
    tiles_per_group = gsz // rt
    last_block = lk_pad // bk - 1
    last_g = n_groups - 1

    def block_start(j):
        return pl.multiple_of(j * bk, bk)

    def scores(g, j):
        s_scr[groups[g], :] = _dot(q_ref[groups[g], :], kt_ref[:, pl.ds(block_start(j), bk)])

    def values(g, j):
        rows = groups[g]
        a = a_scr[rows, :]
        acc_scr[rows, :] = (jnp.concatenate([a] * (acc_scr.shape[1] // LANES), axis=1) * acc_scr[rows, :]
                            + _dot(p_scr[rows, :], v_ref[pl.ds(block_start(j), bk), :]))

    p_scr[groups[last_g], :] = jnp.zeros((gsz, bk), BF16)
    a_scr[groups[last_g], :] = jnp.ones((gsz, LANES), F32)
    scores(0, 0)

    def step(j, carry, masked):
        start = block_start(j)

        def softmax(g):
            for t in range(g * tiles_per_group, (g + 1) * tiles_per_group):
                rows = tiles[t]
                s = s_scr[rows, :]
                if masked:
                    vis = vis_scr[rows, :]
                    lane = start + lax.broadcasted_iota(jnp.int32, (1, LANES), 1)
                    s = jnp.concatenate([jnp.where(lane + c * LANES < vis, s[:, lc], NEG_INF)
                                         for c, lc in enumerate(lane_chunks)], axis=1)
                m_old = m_scr[rows, :]
                m_new = jnp.maximum(m_old, jnp.max(s, axis=-1, keepdims=True))
                for lc in lane_chunks:
                    p_scr[rows, lc] = jnp.exp2(s[:, lc] - m_new).astype(BF16)
                a_scr[rows, :] = jnp.exp2(m_old - m_new)
                m_scr[rows, :] = m_new

        values(last_g, jnp.maximum(j - 1, 0))
        for g in range(n_groups):
            if g < last_g:
                scores(g + 1, j)
            else:
                scores(0, jnp.minimum(j + 1, last_block))
            softmax(g)
            if g < last_g:
                values(g, j)
        return carry

    lax.fori_loop(0, n_full, functools.partial(step, masked=False), 0)
    lax.fori_loop(n_full, nkb, functools.partial(step, masked=True), 0)
    values(last_g, nkb - 1)
    for rows in tiles:
        o_ref[rows, :] = (acc_scr[rows, :vd] / acc_scr[rows, vd:vd + 1]).astype(o_ref.dtype)


def _attn_full_kernel(q_ref, knt_ref, krt_ref, v_ref, o_ref, *, hb, nope, vd, n_keys):
    dqk = nope + LANES
    lq = q_ref.shape[0]
    heads = range(hb)
    kpos = lax.broadcasted_iota(jnp.int32, (1, knt_ref.shape[1]), 1)
    qs = [q_ref[:, h * dqk:(h + 1) * dqk] for h in heads]
    s_rope = _dot(jnp.concatenate([q[:, nope:] for q in qs], axis=0), krt_ref[...])
    ss = [jnp.where(kpos < n_keys,
                    _dot(qs[h][:, :nope], knt_ref[h * nope:(h + 1) * nope, :]) + s_rope[h * lq:(h + 1) * lq],
                    NEG_INF) for h in heads]
    es = [jnp.exp2(s - jnp.max(s, axis=-1, keepdims=True)) for s in ss]
    ls = [jnp.sum(e, axis=-1, keepdims=True) for e in es]
    pv = [_dot(es[h].astype(BF16), v_ref[:, h * vd:(h + 1) * vd]) for h in heads]
    for h in heads:
        o_ref[:, h * vd:(h + 1) * vd] = (pv[h] / ls[h]).astype(o_ref.dtype)


def _attention_causal(q, kt, vf, *, nb, lq, lk_pad, heads, nope, vd, bq, bk, n_meta):
    dqk = nope + LANES
    nq = lq // bq
    rt = 32 if bq % 32 == 0 else bq
    n_groups = 4 if bq % (4 * rt) == 0 else 2
    assert bq % (n_groups * rt) == 0
    return pl.pallas_call(
        functools.partial(_attn_causal_kernel, bq=bq, bk=bk, rt=rt, n_groups=n_groups, vd=vd,
                          n_meta=n_meta, lk_pad=lk_pad),
        out_shape=jax.ShapeDtypeStruct((q.shape[0], heads * vd), BF16),
        grid=(nb, heads, nq),
        in_specs=[pl.BlockSpec((bq, dqk), lambda b, h, i: (b * nq + i, h)),
                  pl.BlockSpec((dqk, lk_pad), lambda b, h, i: (h, b)),
                  pl.BlockSpec((lk_pad, vd + LANES), lambda b, h, i: (b, h))],
        out_specs=pl.BlockSpec((bq, vd), lambda b, h, i: (b * nq + i, h)),
        scratch_shapes=[pltpu.VMEM((bq, bk), F32), pltpu.VMEM((bq, bk), BF16),
                        pltpu.VMEM((bq, LANES), F32), pltpu.VMEM((bq, LANES), F32),
                        pltpu.VMEM((bq, vd + LANES), F32), pltpu.VMEM((bq, LANES), jnp.int32)],
        compiler_params=_params(("parallel", "parallel", "arbitrary")),
        name="mla_attention_causal",
    )(q, kt, vf)


def _attention_full(q, knt, lat_t, vf, o_all, *, nb, lq, lk_pad, row0, heads, hb, nope, vd, kv, n_keys):
    dqk = nope + LANES
    blk0 = row0 // lq

    def body(q_ref, knt_ref, krt_ref, v_ref, o_all_ref, o_ref):
        _attn_full_kernel(q_ref, knt_ref, krt_ref, v_ref, o_ref, hb=hb, nope=nope, vd=vd, n_keys=n_keys)

    return pl.pallas_call(
        body,
        out_shape=jax.ShapeDtypeStruct(o_all.shape, o_all.dtype),
        grid=(nb, heads // hb),
        in_specs=[pl.BlockSpec((lq, hb * dqk), lambda b, g: (blk0 + b, g)),
                  pl.BlockSpec((hb * nope, lk_pad), lambda b, g: (g, b)),
                  pl.BlockSpec((LANES, lk_pad), lambda b, g: (kv // LANES, b)),
                  pl.BlockSpec((lk_pad, hb * vd), lambda b, g: (b, g)),
                  pl.BlockSpec(memory_space=pl.ANY)],
        out_specs=pl.BlockSpec((lq, hb * vd), lambda b, g: (blk0 + b, g)),
        input_output_aliases={4: 0},
        compiler_params=_params(("parallel", "parallel")),
        name="mla_attention_full",
    )(q, knt, lat_t, vf, o_all)


def _item_info(i, n_p, cps, cs, l_valid):
    is_p = i < n_p
    j = i - n_p
    seq = jnp.where(is_p, i // cps, n_p // cps + j // cs)
    c = jnp.where(is_p, i % cps, j % cs)
    nc = jnp.where(is_p, cps, cs)
    nvalid = jnp.where(is_p, jnp.clip(l_valid - c * CHUNK, 0, CHUNK), CHUNK)
    return seq, c, nc, nvalid


def _row_sums_of_squares(xs):
    rows, n = xs[0].shape
    parts = []
    for x in xs:
        sq = x * x
        hi = sq.astype(BF16)
        parts.append(jnp.concatenate([hi, (sq - hi.astype(F32)).astype(BF16)], axis=1))
    tot = _dot(jnp.concatenate(parts, axis=0), jnp.ones((2 * n, n), BF16))
    return [tot[i * rows:(i + 1) * rows] for i in range(len(xs))]


def _tri_inverse_many(mats):
    n = mats[0].shape[0]
    r = lax.broadcasted_iota(jnp.int32, (n, n), 0)
    c = lax.broadcasted_iota(jnp.int32, (n, n), 1)
    eye = jnp.where(r == c, 1.0, 0.0)
    zero = jnp.zeros((n, n), F32)

    def split(x):
        hi = x.astype(BF16).astype(F32)
        return hi, x - hi

    def lhs_of(x):
        hi, lo = split(x)
        return jnp.concatenate([hi, lo, hi, zero], axis=1).astype(BF16)

    def rhs_of(x):
        hi, lo = split(x)
        return jnp.concatenate([hi, hi, lo, zero], axis=0).astype(BF16)

    ps = [eye - a for a in mats]
    aks = list(mats)
    rhs = [rhs_of(a) for a in mats]
    power, level = 2, 0
    while power < n:
        if level < _SPLIT_LEVELS:
            aks = [_dot(lhs_of(ak), rk) for ak, rk in zip(aks, rhs)]
            rhs = [rhs_of(ak) for ak in aks]
            ps = [p + _dot(lhs_of(p), rk) for p, rk in zip(ps, rhs)]
        else:
            akb = [ak.astype(BF16) for ak in aks]
            aks = [_dot(b, b) for b in akb]
            ps = [p + _dot(p.astype(BF16), ak.astype(BF16)) for p, ak in zip(ps, aks)]
        power *= 2
        level += 1
    return ps


def _gdn_kernel(qc_ref, kc_ref, vc_ref, qp_ref, kp_ref, vp_ref, q0_ref, k0_ref, v0_ref,
                wq_ref, wk_ref, wv_ref, ab_ref, alog_ref, dtb_ref, shift_ref, z_ref, gain_ref, s0_ref,
                o_ref, snew_ref, srun_ref,
                val_s, kcd_s, qdec_s, kdec_s, qk_s, eg_s, s_ref,
                *, hg, dk, n_items, n_p, cps, cs, l_valid, width):
    C = CHUNK
    t = pl.program_id(1)
    i = jnp.minimum(t, n_items - 1)
    j = jnp.maximum(t - 1, 0)
    _, c, _, nvalid = _item_info(i, n_p, cps, cs, l_valid)
    _, cj, ncj, _ = _item_info(j, n_p, cps, cs, l_valid)
    first = c == 0
    heads = range(hg)
    sls = [slice(h * dk, (h + 1) * dk) for h in heads]

    @pl.when(t == 0)
    def _():
        for ref in (val_s, kcd_s, qdec_s, kdec_s, qk_s, eg_s):
            ref[...] = jnp.zeros(ref.shape, ref.dtype)

    n_new = n_p // cps
    seq_j = _item_info(j, n_p, cps, cs, l_valid)[0]

    @pl.when(cj == 0)
    def _():
        s_ref[...] = jnp.where(seq_j < n_new, 0.0, s0_ref[0])

    sb = [s_ref[h].astype(BF16) for h in heads]
    ks_s = [_dot(kcd_s[:, sls[h]], sb[h]) for h in heads]
    qs_s = [_dot(qdec_s[:, sls[h]], sb[h]) for h in heads]

    halo = SUBLANES
    kpad = shift_ref.shape[1] // 2
    shift = shift_ref[...]

    def conv_silu(cur_ref, prev_ref, init_ref, w_ref):
        cur = cur_ref[...]
        rows = jnp.concatenate([jnp.where(first, init_ref[0], prev_ref[...]), cur,
                                jnp.zeros((kpad - halo - C, cur.shape[1]), F32)], axis=0)
        hi = rows.astype(BF16)
        lo = (rows - hi.astype(F32)).astype(BF16)
        shifted = _dot(shift, jnp.concatenate([hi, lo], axis=0))
        y = cur * w_ref[width - 1:width, :]
        for t in range(width - 1):
            y = y + shifted[t * C:(t + 1) * C] * w_ref[t:t + 1, :]
        return y * _sigmoid(y)

    qs = conv_silu(qc_ref, qp_ref, q0_ref, wq_ref)
    ks = conv_silu(kc_ref, kp_ref, k0_ref, wk_ref)
    vs = conv_silu(vc_ref, vp_ref, v0_ref, wv_ref)

    ub = [(val_s[:, sls[h]] - ks_s[h]).astype(BF16) for h in heads]
    ds_s = [_dot_tn(kdec_s[:, sls[h]], ub[h]) for h in heads]
    os_s = [qs_s[h] + _dot(qk_s[:, h * C:(h + 1) * C].astype(BF16), ub[h]) for h in heads]

    ab = ab_ref[...]
    rows = lax.broadcasted_iota(jnp.int32, (C, LANES), 0)
    valid = rows < nvalid
    xs = ab + dtb_ref[0]
    softplus = jnp.maximum(xs, 0.0) + jnp.log(1.0 + jnp.exp(-jnp.abs(xs)))
    g = jnp.where(valid, -jnp.exp(alog_ref[0]) * softplus, 0.0)
    beta = jnp.where(valid, _sigmoid(ab), 0.0)
    r = lax.broadcasted_iota(jnp.int32, (C, C), 0)
    cc = lax.broadcasted_iota(jnp.int32, (C, C), 1)
    incl = r >= cc
    strict = r > cc
    gcum = jnp.dot(jnp.where(incl, 1.0, 0.0), g, precision=_HI, preferred_element_type=F32)
    gcum_t = gcum.T

    q_ssq = _row_sums_of_squares([qs[:, sl] for sl in sls])
    k_ssq = _row_sums_of_squares([ks[:, sl] for sl in sls])
    qn = [qs[:, sls[h]] * lax.rsqrt(q_ssq[h] + NORM_EPS) * (dk ** -0.5) for h in heads]
    kn = [ks[:, sls[h]] * lax.rsqrt(k_ssq[h] + NORM_EPS) for h in heads]
    bcol = [beta[:, hg + h:hg + h + 1] for h in heads]
    gcol = [gcum[:, h:h + 1] for h in heads]
    glast = [gcum[C - 1:C, h:h + 1] for h in heads]
    eg = [jnp.exp(gcol[h]) for h in heads]
    decay = [jnp.where(incl, jnp.exp(jnp.where(incl, gcol[h] - gcum_t[h:h + 1, :], 0.0)), 0.0) for h in heads]
    kb = [kn[h] * bcol[h] for h in heads]
    knb = [kn[h].astype(BF16) for h in heads]
    a_mat = [jnp.where(strict, _dot_nt(kb[h].astype(BF16), knb[h]) * decay[h], 0.0) for h in heads]

    gain = gain_ref[...]
    for h in heads:
        s_ref[h] = s_ref[h] * eg_s[h:h + 1, :] + ds_s[h]
    for h in heads:
        o = os_s[h]
        z = z_ref[:, sls[h]]
        ms = jnp.mean(o * o, axis=-1, keepdims=True)
        o_ref[:, sls[h]] = (o * lax.rsqrt(ms + NORM_EPS) * gain * (z * _sigmoid(z))).astype(o_ref.dtype)

    t_mat = _tri_inverse_many(a_mat)
    rhs = [jnp.concatenate([vs[:, sls[h]] * bcol[h], kb[h] * eg[h]], axis=1).astype(BF16) for h in heads]
    tv = [_dot(t_mat[h].astype(BF16), rhs[h]) for h in heads]
    qk = [_dot_nt(qn[h].astype(BF16), knb[h]) * decay[h] for h in heads]
    for h in heads:
        val_s[:, sls[h]] = tv[h][:, :dk]
        kcd_s[:, sls[h]] = tv[h][:, dk:].astype(BF16)
        qdec_s[:, sls[h]] = (qn[h] * eg[h]).astype(BF16)
        kdec_s[:, sls[h]] = (kn[h] * jnp.exp(glast[h] - gcol[h])).astype(BF16)
        eg_s[h:h + 1, :] = jnp.broadcast_to(jnp.exp(glast[h]), (1, LANES))
    qk_s[...] = jnp.concatenate(qk, axis=1)

    done = (cj == ncj - 1) & (t > 0)

    @pl.when(done & (seq_j < n_new))
    def _():
        snew_ref[0] = s_ref[...]

    @pl.when(done & (seq_j >= n_new))
    def _():
        srun_ref[0] = s_ref[...]


def _gdn_mixer(proj, gates, conv0, s0, conv_w, a_log, dt_bias, o_gain, *, layer, prev_states, heads, dk, hg,
               n_p, cps, cs, l_valid):
    m = proj.shape[0]
    C = CHUNK
    w = heads * dk
    n_hg = heads // hg
    gw = hg * dk
    n_items = m // C
    n_layers, n_run = s0.shape[:2]
    n_new = n_p // cps
    width = conv_w.shape[0]
    info = dict(n_p=n_p, cps=cps, cs=cs, l_valid=l_valid)

    def seq_of(i):
        return _item_info(i, n_p, cps, cs, l_valid)[0]

    prep = lambda t: jnp.minimum(t, n_items - 1)
    scan = lambda t: jnp.maximum(t - 1, 0)
    cur = lambda off: pl.BlockSpec((C, gw), lambda g, t: (prep(t), off * n_hg + g))
    prev = lambda off: pl.BlockSpec(
        (SUBLANES, gw), lambda g, t: (jnp.maximum(prep(t) * (C // SUBLANES) - 1, 0), off * n_hg + g))
    init = lambda off: pl.BlockSpec((1, SUBLANES, gw), lambda g, t: (seq_of(prep(t)), 0, off * n_hg + g))
    wspec = lambda off: pl.BlockSpec((width, gw), lambda g, t: (0, off * n_hg + g))
    gate_vec = pl.BlockSpec((1, 1, LANES), lambda g, t: (g, 0, 0))
    state_new = pl.BlockSpec((None, 1, hg, dk, dk),
                             lambda g, t: (layer, jnp.minimum(seq_of(scan(t)), n_new - 1), g, 0, 0))
    state_run = pl.BlockSpec((None, 1, hg, dk, dk),
                             lambda g, t: (layer, jnp.maximum(seq_of(scan(t)) - n_new, 0), g, 0, 0))

    kpad = _round_up(SUBLANES + C, LANES)
    r = jnp.arange((width - 1) * C)
    src = (r % C) + SUBLANES - (width - 1) + r // C
    col = jnp.arange(2 * kpad)
    shift = ((col[None, :] == src[:, None]) | (col[None, :] == src[:, None] + kpad)).astype(BF16)

    alog_g = jnp.pad(a_log.reshape(n_hg, 1, hg).astype(F32), ((0, 0), (0, 0), (0, LANES - hg)))
    dtb_g = jnp.pad(dt_bias.reshape(n_hg, 1, hg).astype(F32), ((0, 0), (0, 0), (0, LANES - hg)))

    n_in = 19
    carried = list(prev_states) if prev_states is not None else []

    def body(*refs):
        _gdn_kernel(*refs[:n_in], *refs[n_in + len(carried):],
                    hg=hg, dk=dk, width=width, n_items=n_items, **info)

    return pl.pallas_call(
        body,
        out_shape=(jax.ShapeDtypeStruct((m, w), BF16),
                   jax.ShapeDtypeStruct((n_layers, n_new, heads, dk, dk), F32),
                   jax.ShapeDtypeStruct((n_layers, n_run, heads, dk, dk), F32)),
        input_output_aliases={n_in + i: 1 + i for i in range(len(carried))},
        grid=(n_hg, n_items + 1),
        in_specs=[cur(0), cur(1), cur(2), prev(0), prev(1), prev(2), init(0), init(1), init(2),
                  wspec(0), wspec(1), wspec(2),
                  pl.BlockSpec((C, LANES), lambda g, t: (prep(t), g)),
                  gate_vec, gate_vec,
                  pl.BlockSpec(shift.shape, lambda g, t: (0, 0)),
                  pl.BlockSpec((C, gw), lambda g, t: (scan(t), 3 * n_hg + g)),
                  pl.BlockSpec((1, dk), lambda g, t: (0, 0)),
                  state_run] + [pl.BlockSpec(memory_space=pl.ANY)] * len(carried),
        out_specs=(pl.BlockSpec((C, gw), lambda g, t: (scan(t), g)), state_new, state_run),
        scratch_shapes=[pltpu.VMEM((C, gw), F32), pltpu.VMEM((C, gw), BF16), pltpu.VMEM((C, gw), BF16),
                        pltpu.VMEM((C, gw), BF16), pltpu.VMEM((C, hg * C), F32), pltpu.VMEM((hg, LANES), F32),
                        pltpu.VMEM((hg, dk, dk), F32)],
        compiler_params=_params(("parallel", "arbitrary")),
        name="gdn_mixer",
    )(proj, proj, proj, proj, proj, proj, conv0, conv0, conv0, conv_w, conv_w, conv_w,
      gates, alog_g, dtb_g, shift, proj, o_gain.reshape(1, dk).astype(F32), s0, *carried)


def _split_half_layout(x, half):
    pad = [(0, 0)] * (x.ndim - 1) + [(0, LANES // 2 - half)]
    return jnp.concatenate([jnp.pad(x[..., :half], pad), jnp.pad(x[..., half:], pad)], axis=-1)


def _rope_tables(pos, half):
    inv_freq = 1.0 / (ROPE_THETA ** (jnp.arange(half, dtype=F32) / half))
    ang = pos[:, None] * inv_freq[None, :]
    cos, sin = jnp.cos(ang), jnp.sin(ang)
    cos_t = _split_half_layout(jnp.concatenate([cos, cos], -1), half)
    sin_t = _split_half_layout(jnp.concatenate([-sin, sin], -1), half)
    return cos_t, sin_t


def kernel(x_prompt, x_sample, state_conv, state_delta, cache_ckv, cache_krope, meta_tokens,
           a_w_in, a_conv_w, a_a_log, a_dt_bias, a_o_gain, a_w_o,
           b_w_dq, b_q_gain, b_w_uq, b_w_o,
           kv_w_dkv, kv_gain, kv_w_uk, kv_w_uv,
           mlp_w_up, mlp_w_down, ln_gain, ln_bias):
    B, seq, D = x_prompt.shape
    DB, ts, _ = x_sample.shape
    n_meta = meta_tokens.shape[0]
    depth = ln_gain.shape[0]
    n_a = a_w_in.shape[0]
    H = a_a_log.shape[1]
    dk = a_o_gain.shape[1]
    W = H * dk
    width = a_conv_w.shape[1]
    past = cache_ckv.shape[1]
    KV, MH, nope = kv_w_uk.shape
    vd = kv_w_uv.shape[2]
    rope = cache_krope.shape[2]
    half = rope // 2
    alpha = (2 * depth) ** 0.25
    scale = (nope + rope) ** -0.5
    assert ts % CHUNK == 0 and ts >= width - 1 and half < LANES // 2
    assert nope % LANES == 0 and vd % LANES == 0 and KV % LANES == 0 and dk % LANES == 0

    L = n_meta + seq
    LP = _round_up(L, LANES)
    n_prompt_rows = B * LP
    M = n_prompt_rows + DB * ts
    cps, cs = LP // CHUNK, ts // CHUNK
    n_p = B * cps
    hg = min(32, H)
    n_hg = H // hg

    meta = jnp.broadcast_to(meta_tokens.astype(F32)[None], (B, n_meta, D))
    xp = jnp.concatenate([meta, x_prompt, jnp.zeros((B, LP - L, D), F32)], axis=1)
    x = jnp.concatenate([xp.reshape(n_prompt_rows, D), x_sample.reshape(DB * ts, D)], axis=0)
    xb = x.astype(BF16)

    pos = jnp.concatenate([jnp.tile(jnp.arange(LP, dtype=F32), B),
                           jnp.tile(past + jnp.arange(ts, dtype=F32), DB)])
    cos_t, sin_t = _rope_tables(pos, half)

    w_in_t = jnp.swapaxes(a_w_in, 1, 2)

    conv0 = jnp.concatenate([jnp.zeros((n_a, B, width - 1, 3 * W), F32), state_conv.astype(F32)], axis=1)
    conv0 = jnp.pad(conv0, ((0, 0), (0, 0), (SUBLANES - (width - 1), 0), (0, 0)))
    s0 = state_delta.astype(F32)

    new_conv_p, new_conv_s, states = [], [], None
    k_slabs = None
    lat = None
    for layer in range(depth):
        if layer < n_a:
            w_ab = a_w_in[layer, :, 4 * W:]
            gate_w = jnp.concatenate(
                [w_ab[:, :H].reshape(D, n_hg, hg), w_ab[:, H:].reshape(D, n_hg, hg),
                 jnp.zeros((D, n_hg, LANES - 2 * hg), w_ab.dtype)], axis=-1).reshape(D, n_hg * LANES)
            proj = _matmul(xb, w_in_t, F32, layer=layer, n_use=4 * W, b_transposed=True)
            gates = _matmul(xb, gate_w.astype(BF16), F32, bm=328)
            o, *states = _gdn_mixer(proj, gates, conv0[layer], s0, a_conv_w[layer].astype(F32),
                                    a_a_log[layer], a_dt_bias[layer], a_o_gain[layer],
                                    layer=layer, prev_states=states, heads=H, dk=dk, hg=hg,
                                    n_p=n_p, cps=cps, cs=cs, l_valid=L)
            h = _matmul(o, a_w_o, BF16, layer=layer)
            new_conv_p.append(jnp.stack(
                [proj[b * LP + L - (width - 1):b * LP + L, :3 * W] for b in range(B)]))
            new_conv_s.append(jnp.stack(
                [proj[n_prompt_rows + ts - (width - 1) + t::ts, :3 * W] for t in range(width - 1)], axis=1))
        else:
            j = layer - n_a
            ql = _matmul(xb, b_w_dq[j].astype(BF16), BF16, epilogue="rms", gain=b_q_gain[j])
            w_uq = b_w_uq[j].reshape(-1, MH, nope + rope)
            w_uq = jnp.concatenate([w_uq[..., :nope], _split_half_layout(w_uq[..., nope:], half)], axis=-1)
            q = _matmul(ql, w_uq.reshape(-1, MH * (nope + LANES)).astype(BF16), BF16, bn=2048, epilogue="rope",
                        rope=(cos_t, sin_t, nope // LANES + 1, nope // LANES, scale * math.log2(math.e)))
            kp, vp, ks, ls_t, vs, lkp, lks = k_slabs
            o = _attention_causal(q, kp, vp, nb=B, lq=LP, lk_pad=lkp, heads=MH, nope=nope,
                                  vd=vd, bq=_pick_block(LP, 1408, LANES), bk=512, n_meta=n_meta)
            o = _attention_full(q, ks, ls_t, vs, o, nb=DB, lq=ts, lk_pad=lks, row0=n_prompt_rows,
                                heads=MH, hb=min(8, MH), nope=nope, vd=vd, kv=KV, n_keys=past + ts)
            h = _matmul(o, b_w_o, BF16, layer=j)

        x, xb = _deepnorm(x, h, ln_gain[layer, 0], ln_bias[layer, 0], alpha)
        hid = _matmul(xb, mlp_w_up, BF16, layer=layer, epilogue="relu2")
        h = _matmul(hid, mlp_w_down, BF16, layer=layer, bn=1024, bk=2048)
        x, xb = _deepnorm(x, h, ln_gain[layer, 1], ln_bias[layer, 1], alpha)

        if layer == n_a - 1:
            w_dkv = jnp.concatenate([kv_w_dkv[:, :KV], _split_half_layout(kv_w_dkv[:, KV:], half)], axis=1)
            lat = _kv_post(_matmul(xb, w_dkv.astype(BF16), F32, bm=328, bn=640), kv_gain.astype(F32),
                           cos_t, sin_t, KV)
            lane_ids = jnp.arange(LANES)
            eye = jnp.where((lane_ids[:, None] == lane_ids[None, :]) & (lane_ids[:, None] != half), 1.0, 0.0)
            ones_col = jnp.where((lane_ids[:, None] == half) & (lane_ids[None, :] == 0), 1.0, 0.0)
            w_k = jnp.concatenate(
                [jnp.concatenate([kv_w_uk.astype(F32), jnp.zeros((KV, MH, LANES), F32)], axis=-1),
                 jnp.concatenate([jnp.zeros((LANES, MH, nope), F32),
                                  jnp.broadcast_to(eye[:, None, :], (LANES, MH, LANES))], axis=-1)],
                axis=0).reshape(KV + LANES, MH * (nope + LANES)).astype(BF16)
            w_v = jnp.concatenate(
                [jnp.concatenate([kv_w_uv.astype(F32), jnp.zeros((KV, MH, LANES), F32)], axis=-1),
                 jnp.concatenate([jnp.zeros((LANES, MH, vd), F32),
                                  jnp.broadcast_to(ones_col[:, None, :], (LANES, MH, LANES))], axis=-1)],
                axis=0).reshape(KV + LANES, MH * (vd + LANES)).astype(BF16)
            one_lane = jnp.where(jnp.arange(KV + LANES) == KV + half, 1.0, 0.0)
            lkp = _round_up(L, 512)
            lat_p = lat[:n_prompt_rows].reshape(B, LP, KV + LANES)
            if lkp >= LP:
                lat_p = jnp.pad(lat_p, ((0, 0), (0, lkp - LP), (0, 0)))
            else:
                lat_p = lat_p[:, :lkp]
            lat_p = (lat_p.reshape(B * lkp, KV + LANES) + one_lane).astype(BF16)
            lks = _round_up(past + ts, LANES)
            cache = jnp.concatenate([cache_ckv.astype(F32), _split_half_layout(cache_krope.astype(F32), half)],
                                    axis=-1)
            lat_s = jnp.concatenate([cache, lat[n_prompt_rows:].reshape(DB, ts, KV + LANES),
                                     jnp.zeros((DB, lks - past - ts, KV + LANES), F32)], axis=1)
            lat_s = (lat_s.reshape(DB * lks, KV + LANES) + one_lane).astype(BF16)
            w_kn_t = jnp.transpose(kv_w_uk, (1, 2, 0)).reshape(MH * nope, KV).astype(BF16)
            lat_s_t = lat_s.T
            big = dict(bm=2048, bn=2048)
            k_slabs = (_matmul(w_k.T, lat_p.T, BF16, bm=2048, bn=1536), _matmul(lat_p, w_v, BF16, bm=1536, bn=2048),
                       _matmul(w_kn_t, lat_s_t, BF16, **big), lat_s_t,
                       _matmul(lat_s, kv_w_uv.reshape(KV, MH * vd).astype(BF16), BF16, **big),
                       lkp, lks)

    def unsplit(r):
        return jnp.concatenate([r[..., :half], r[..., LANES // 2:LANES // 2 + half]], axis=-1)

    lat_p = lat[:n_prompt_rows].reshape(B, LP, KV + LANES)[:, :L]
    lat_s = lat[n_prompt_rows:].reshape(DB, ts, KV + LANES)
    y_prompt = jnp.stack([x[b * LP + n_meta:b * LP + L] for b in range(B)])
    y_sample = x[n_prompt_rows:].reshape(DB, ts, D)
    return (y_prompt, y_sample,
            jnp.stack(new_conv_p), states[0],
            lat_p[..., :KV], unsplit(lat_p[..., KV:]),
            jnp.stack(new_conv_s), states[1],
            lat_s[..., :KV], unsplit(lat_s[..., KV:]))
```

```python
import functools
import math

import jax
import jax.numpy as jnp
from jax import lax
from jax.experimental import pallas as pl
from jax.experimental.pallas import tpu as pltpu

CHUNK = 64
ROPE_THETA = 10000.0
NORM_EPS = 1e-6
NEG_INF = -1e30
LANES = 128
SUBLANES = 8
VMEM_LIMIT_BYTES = 56 * 1024 * 1024
_SPLIT_LEVELS = 2
_CHUNK_SHIFT = CHUNK.bit_length() - 1
assert 1 << _CHUNK_SHIFT == CHUNK

F32 = jnp.float32
BF16 = jnp.bfloat16
_HI = lax.Precision.HIGHEST


def _round_up(x, m):
    return -(-x // m) * m


def _pick_block(dim, target, align):
    best = None
    for d in range(align, min(dim, target) + 1, align):
        if dim % d == 0:
            best = d
    return best if best is not None else dim


def _dot(a, b):
    return jnp.dot(a, b, preferred_element_type=F32)


def _dot_nt(a, b):
    return lax.dot_general(a, b, (((1,), (1,)), ((), ())), preferred_element_type=F32)


def _dot_tn(a, b):
    return lax.dot_general(a, b, (((0,), (0,)), ((), ())), preferred_element_type=F32)


def _sigmoid(x):
    return 1.0 / (1.0 + jnp.exp(-x))


def _params(sem):
    return pltpu.CompilerParams(dimension_semantics=sem, vmem_limit_bytes=VMEM_LIMIT_BYTES)


def _mm_kernel(*refs, nk, epilogue, b_transposed, rope=None):
    n_extra = {"rms": 1, "rope": 2}.get(epilogue, 0)
    a_ref, b_ref = refs[:2]
    extra = refs[2:2 + n_extra]
    o_ref = refs[2 + n_extra]
    rest = refs[3 + n_extra:]

    def finish(acc):
        if epilogue == "relu2":
            r = jnp.maximum(acc, 0.0)
            acc = r * r
        elif epilogue == "rms":
            ms = jnp.mean(acc * acc, axis=-1, keepdims=True)
            acc = acc * lax.rsqrt(ms + NORM_EPS) * extra[0][...]
        elif epilogue == "rope":
            chunks_per_head, rope_chunk, scale = rope
            cos, sin = extra[0][...], extra[1][...]
            cols = []
            for c in range(acc.shape[1] // LANES):
                x = acc[:, c * LANES:(c + 1) * LANES]
                if c % chunks_per_head == rope_chunk:
                    x = x * cos + pltpu.roll(x, LANES // 2, axis=1) * sin
                cols.append(x * scale)
            acc = jnp.concatenate(cols, axis=1)
        o_ref[...] = acc.astype(o_ref.dtype)

    def product():
        b = b_ref[...].astype(BF16)
        return _dot_nt(a_ref[...], b) if b_transposed else _dot(a_ref[...], b)

    if nk == 1:
        finish(product())
    else:
        acc_ref = rest[0]
        k = pl.program_id(2)

        @pl.when(k == 0)
        def _():
            acc_ref[...] = jnp.zeros_like(acc_ref)

        acc_ref[...] += product()

        @pl.when(k == nk - 1)
        def _():
            finish(acc_ref[...])


def _matmul(a, b, out_dtype, *, layer=None, n_use=None, b_transposed=False, epilogue="none", gain=None,
            rope=None, bm=1312, bn=512, bk=4096):
    m = a.shape[0]
    kdim = min(a.shape[1], b.shape[-1 if b_transposed else -2])
    n = n_use if n_use is not None else b.shape[-2 if b_transposed else -1]
    bm = _pick_block(m, bm, 16)
    bn = n if epilogue == "rms" else _pick_block(n, bn, LANES)
    bk = _pick_block(kdim, bk, LANES)
    nk = kdim // bk
    a_mode = dict(pipeline_mode=pl.Buffered(1)) if nk == 1 and n // bn >= 4 else {}
    b_blk = (bn, bk) if b_transposed else (bk, bn)
    b_idx = (lambda k, j: (j, k)) if b_transposed else (lambda k, j: (k, j))
    if b.ndim == 3:
        b_spec = pl.BlockSpec((None,) + b_blk, lambda i, j, k: (layer,) + b_idx(k, j))
    else:
        b_spec = pl.BlockSpec(b_blk, lambda i, j, k: b_idx(k, j))
    in_specs = [pl.BlockSpec((bm, bk), lambda i, j, k: (i, k), **a_mode), b_spec]
    args = [a, b]
    if epilogue == "rms":
        in_specs.append(pl.BlockSpec((1, bn), lambda i, j, k: (0, j)))
        args.append(gain.reshape(1, n).astype(F32))
    rope_static = None
    if epilogue == "rope":
        cos_t, sin_t, *rope_static = rope
        in_specs += [pl.BlockSpec((bm, LANES), lambda i, j, k: (i, 0))] * 2
        args += [cos_t, sin_t]
        assert bn % (rope_static[0] * LANES) == 0
    scratch = [pltpu.VMEM((bm, bn), F32)] if nk > 1 else []
    return pl.pallas_call(
        functools.partial(_mm_kernel, nk=nk, epilogue=epilogue, b_transposed=b_transposed,
                          rope=tuple(rope_static) if rope_static else None),
        out_shape=jax.ShapeDtypeStruct((m, n), out_dtype),
        grid=(m // bm, n // bn, nk),
        in_specs=in_specs,
        out_specs=pl.BlockSpec((bm, bn), lambda i, j, k: (i, j)),
        scratch_shapes=scratch,
        compiler_params=_params(("parallel", "parallel", "arbitrary")),
        name="matmul_" + epilogue,
    )(*args)


def _ln_kernel(x_ref, h_ref, g_ref, b_ref, of_ref, ob_ref, *, alpha):
    y = alpha * x_ref[...] + h_ref[...].astype(F32)
    mu = jnp.mean(y, axis=-1, keepdims=True)
    d = y - mu
    var = jnp.mean(d * d, axis=-1, keepdims=True)
    out = d * lax.rsqrt(var + NORM_EPS) * g_ref[...] + b_ref[...]
    of_ref[...] = out
    ob_ref[...] = out.astype(BF16)


def _deepnorm(x, h, gain, bias, alpha):
    m, d = x.shape
    br = _pick_block(m, 256, 16)
    row = pl.BlockSpec((br, d), lambda i: (i, 0))
    vec = pl.BlockSpec((1, d), lambda i: (0, 0))
    return pl.pallas_call(
        functools.partial(_ln_kernel, alpha=alpha),
        out_shape=(jax.ShapeDtypeStruct((m, d), F32), jax.ShapeDtypeStruct((m, d), BF16)),
        grid=(m // br,),
        in_specs=[row, row, vec, vec],
        out_specs=(row, row),
        compiler_params=_params(("parallel",)),
        name="deepnorm",
    )(x, h, gain.reshape(1, d), bias.reshape(1, d))


def _kv_post_kernel(lat_ref, g_ref, cos_ref, sin_ref, o_ref, *, kv):
    c = lat_ref[:, :kv]
    ms = jnp.mean(c * c, axis=-1, keepdims=True)
    o_ref[:, :kv] = c * lax.rsqrt(ms + NORM_EPS) * g_ref[...]
    r = lat_ref[:, kv:]
    o_ref[:, kv:] = r * cos_ref[...] + pltpu.roll(r, LANES // 2, axis=1) * sin_ref[...]


def _kv_post(lat, gain, cos_t, sin_t, kv):
    m, w = lat.shape
    br = _pick_block(m, 256, 8)
    return pl.pallas_call(
        functools.partial(_kv_post_kernel, kv=kv),
        out_shape=jax.ShapeDtypeStruct((m, w), F32),
        grid=(m // br,),
        in_specs=[pl.BlockSpec((br, w), lambda i: (i, 0)),
                  pl.BlockSpec((1, kv), lambda i: (0, 0)),
                  pl.BlockSpec((br, LANES), lambda i: (i, 0)),
                  pl.BlockSpec((br, LANES), lambda i: (i, 0))],
        out_specs=pl.BlockSpec((br, w), lambda i: (i, 0)),
        compiler_params=_params(("parallel",)),
        name="kv_post",
    )(lat, gain.reshape(1, kv), cos_t, sin_t)


def _attn_causal_kernel(q_ref, kt_ref, v_ref, o_ref,
                        s_scr, p_scr, m_scr, a_scr, acc_scr, vis_scr, *, bq, bk, rt, n_groups, vd, n_meta, lk_pad):
    qi = pl.program_id(2)
    tiles = [slice(t * rt, (t + 1) * rt) for t in range(bq // rt)]
    lane_chunks = [slice(c * LANES, (c + 1) * LANES) for c in range(bk // LANES)]

    def visible_end(r):
        chunk_id = lax.shift_right_arithmetic(r - n_meta, _CHUNK_SHIFT) + 1
        return jnp.where(r < n_meta, n_meta, chunk_id * CHUNK + n_meta)

    n_full = visible_end(qi * bq) // bk
    nkb = (jnp.minimum(visible_end(qi * bq + bq - 1), lk_pad) + bk - 1) // bk

    for t, rows in enumerate(tiles):
        vis_scr[rows, :] = visible_end(qi * bq + t * rt + lax.broadcasted_iota(jnp.int32, (rt, LANES), 0))
    m_scr[...] = jnp.full(m_scr.shape, NEG_INF, F32)
    acc_scr[...] = jnp.zeros(acc_scr.shape, F32)

    gsz = bq // n_groups
    groups = [slice(g * gsz, (g + 1) * gsz) for g in range(n_groups)]
    tiles_per_group = gsz // rt
    last_block = lk_pad // bk - 1
    last_g = n_groups - 1

    def block_start(j):
        return pl.multiple_of(j * bk, bk)

    def scores(g, j):
        s_scr[groups[g], :] = _dot(q_ref[groups[g], :], kt_ref[:, pl.ds(block_start(j), bk)])

    def values(g, j):
        rows = groups[g]
        a = a_scr[rows, :]
        acc_scr[rows, :] = (jnp.concatenate([a] * (acc_scr.shape[1] // LANES), axis=1) * acc_scr[rows, :]
                            + _dot(p_scr[rows, :], v_ref[pl.ds(block_start(j), bk), :]))

    p_scr[groups[last_g], :] = jnp.zeros((gsz, bk), BF16)
    a_scr[groups[last_g], :] = jnp.ones((gsz, LANES), F32)
    scores(0, 0)

    def step(j, carry, masked):
        start = block_start(j)

        def softmax(g):
            for t in range(g * tiles_per_group, (g + 1) * tiles_per_group):
                rows = tiles[t]
                s = s_scr[rows, :]
                if masked:
                    vis = vis_scr[rows, :]
                    lane = start + lax.broadcasted_iota(jnp.int32, (1, LANES), 1)
                    s = jnp.concatenate([jnp.where(lane + c * LANES < vis, s[:, lc], NEG_INF)
                                         for c, lc in enumerate(lane_chunks)], axis=1)
                m_old = m_scr[rows, :]
                m_new = jnp.maximum(m_old, jnp.max(s, axis=-1, keepdims=True))
                for lc in lane_chunks:
                    p_scr[rows, lc] = jnp.exp2(s[:, lc] - m_new).astype(BF16)
                a_scr[rows, :] = jnp.exp2(m_old - m_new)
                m_scr[rows, :] = m_new

        values(last_g, jnp.maximum(j - 1, 0))
        for g in range(n_groups):
            if g < last_g:
                scores(g + 1, j)
            else:
                scores(0, jnp.minimum(j + 1, last_block))
            softmax(g)
            if g < last_g:
                values(g, j)
        return carry

    lax.fori_loop(0, n_full, functools.partial(step, masked=False), 0)
    lax.fori_loop(n_full, nkb, functools.partial(step, masked=True), 0)
    values(last_g, nkb - 1)
    for rows in tiles:
        o_ref[rows, :] = (acc_scr[rows, :vd] / acc_scr[rows, vd:vd + 1]).astype(o_ref.dtype)


def _attn_full_kernel(q_ref, knt_ref, krt_ref, v_ref, o_ref, *, hb, nope, vd, n_keys):
    dqk = nope + LANES
    lq = q_ref.shape[0]
    heads = range(hb)
    kpos = lax.broadcasted_iota(jnp.int32, (1, knt_ref.shape[1]), 1)
    qs = [q_ref[:, h * dqk:(h + 1) * dqk] for h in heads]
    s_rope = _dot(jnp.concatenate([q[:, nope:] for q in qs], axis=0), krt_ref[...])
    ss = [jnp.where(kpos < n_keys,
                    _dot(qs[h][:, :nope], knt_ref[h * nope:(h + 1) * nope, :]) + s_rope[h * lq:(h + 1) * lq],
                    NEG_INF) for h in heads]
    es = [jnp.exp2(s - jnp.max(s, axis=-1, keepdims=True)) for s in ss]
    ls = [jnp.sum(e, axis=-1, keepdims=True) for e in es]
    pv = [_dot(es[h].astype(BF16), v_ref[:, h * vd:(h + 1) * vd]) for h in heads]
    for h in heads:
        o_ref[:, h * vd:(h + 1) * vd] = (pv[h] / ls[h]).astype(o_ref.dtype)


def _attention_causal(q, kt, vf, *, nb, lq, lk_pad, heads, nope, vd, bq, bk, n_meta):
    dqk = nope + LANES
    nq = lq // bq
    rt = 32 if bq % 32 == 0 else bq
    n_groups = 4 if bq % (4 * rt) == 0 else 2
    assert bq % (n_groups * rt) == 0
    return pl.pallas_call(
        functools.partial(_attn_causal_kernel, bq=bq, bk=bk, rt=rt, n_groups=n_groups, vd=vd,
                          n_meta=n_meta, lk_pad=lk_pad),
        out_shape=jax.ShapeDtypeStruct((q.shape[0], heads * vd), BF16),
        grid=(nb, heads, nq),
        in_specs=[pl.BlockSpec((bq, dqk), lambda b, h, i: (b * nq + i, h)),
                  pl.BlockSpec((dqk, lk_pad), lambda b, h, i: (h, b)),
                  pl.BlockSpec((lk_pad, vd + LANES), lambda b, h, i: (b, h))],
        out_specs=pl.BlockSpec((bq, vd), lambda b, h, i: (b * nq + i, h)),
        scratch_shapes=[pltpu.VMEM((bq, bk), F32), pltpu.VMEM((bq, bk), BF16),
                        pltpu.VMEM((bq, LANES), F32), pltpu.VMEM((bq, LANES), F32),
                        pltpu.VMEM((bq, vd + LANES), F32), pltpu.VMEM((bq, LANES), jnp.int32)],
        compiler_params=_params(("parallel", "parallel", "arbitrary")),
        name="mla_attention_causal",
    )(q, kt, vf)


def _attention_full(q, knt, lat_t, vf, o_all, *, nb, lq, lk_pad, row0, heads, hb, nope, vd, kv, n_keys):
    dqk = nope + LANES
    blk0 = row0 // lq

    def body(q_ref, knt_ref, krt_ref, v_ref, o_all_ref, o_ref):
        _attn_full_kernel(q_ref, knt_ref, krt_ref, v_ref, o_ref, hb=hb, nope=nope, vd=vd, n_keys=n_keys)

    return pl.pallas_call(
        body,
        out_shape=jax.ShapeDtypeStruct(o_all.shape, o_all.dtype),
        grid=(nb, heads // hb),
        in_specs=[pl.BlockSpec((lq, hb * dqk), lambda b, g: (blk0 + b, g)),
                  pl.BlockSpec((hb * nope, lk_pad), lambda b, g: (g, b)),
                  pl.BlockSpec((LANES, lk_pad), lambda b, g: (kv // LANES, b)),
                  pl.BlockSpec((lk_pad, hb * vd), lambda b, g: (b, g)),
                  pl.BlockSpec(memory_space=pl.ANY)],
        out_specs=pl.BlockSpec((lq, hb * vd), lambda b, g: (blk0 + b, g)),
        input_output_aliases={4: 0},
        compiler_params=_params(("parallel", "parallel")),
        name="mla_attention_full",
    )(q, knt, lat_t, vf, o_all)


def _item_info(i, n_p, cps, cs, l_valid):
    is_p = i < n_p
    j = i - n_p
    seq = jnp.where(is_p, i // cps, n_p // cps + j // cs)
    c = jnp.where(is_p, i % cps, j % cs)
    nc = jnp.where(is_p, cps, cs)
    nvalid = jnp.where(is_p, jnp.clip(l_valid - c * CHUNK, 0, CHUNK), CHUNK)
    return seq, c, nc, nvalid


def _row_sums_of_squares(xs):
    rows, n = xs[0].shape
    parts = []
    for x in xs:
        sq = x * x
        hi = sq.astype(BF16)
        parts.append(jnp.concatenate([hi, (sq - hi.astype(F32)).astype(BF16)], axis=1))
    tot = _dot(jnp.concatenate(parts, axis=0), jnp.ones((2 * n, n), BF16))
    return [tot[i * rows:(i + 1) * rows] for i in range(len(xs))]


def _tri_inverse_many(mats):
    n = mats[0].shape[0]
    r = lax.broadcasted_iota(jnp.int32, (n, n), 0)
    c = lax.broadcasted_iota(jnp.int32, (n, n), 1)
    eye = jnp.where(r == c, 1.0, 0.0)
    zero = jnp.zeros((n, n), F32)

    def split(x):
        hi = x.astype(BF16).astype(F32)
        return hi, x - hi

    def lhs_of(x):
        hi, lo = split(x)
        return jnp.concatenate([hi, lo, hi, zero], axis=1).astype(BF16)

    def rhs_of(x):
        hi, lo = split(x)
        return jnp.concatenate([hi, hi, lo, zero], axis=0).astype(BF16)

    ps = [eye - a for a in mats]
    aks = list(mats)
    rhs = [rhs_of(a) for a in mats]
    power, level = 2, 0
    while power < n:
        if level < _SPLIT_LEVELS:
            aks = [_dot(lhs_of(ak), rk) for ak, rk in zip(aks, rhs)]
            rhs = [rhs_of(ak) for ak in aks]
            ps = [p + _dot(lhs_of(p), rk) for p, rk in zip(ps, rhs)]
        else:
            akb = [ak.astype(BF16) for ak in aks]
            aks = [_dot(b, b) for b in akb]
            ps = [p + _dot(p.astype(BF16), ak.astype(BF16)) for p, ak in zip(ps, aks)]
        power *= 2
        level += 1
    return ps


def _gdn_kernel(qc_ref, kc_ref, vc_ref, qp_ref, kp_ref, vp_ref, q0_ref, k0_ref, v0_ref,
                wq_ref, wk_ref, wv_ref, ab_ref, alog_ref, dtb_ref, shift_ref, z_ref, gain_ref, s0_ref,
                o_ref, snew_ref, srun_ref, cq_ref, ck_ref, cv_ref,
                val_s, kcd_s, qdec_s, kdec_s, qk_s, eg_s, s_ref,
                *, hg, dk, n_items, n_p, cps, cs, l_valid, width):
    C = CHUNK
    t = pl.program_id(1)
    i = jnp.minimum(t, n_items - 1)
    j = jnp.maximum(t - 1, 0)
    _, c, _, nvalid = _item_info(i, n_p, cps, cs, l_valid)
    _, cj, ncj, _ = _item_info(j, n_p, cps, cs, l_valid)
    first = c == 0
    heads = range(hg)
    sls = [slice(h * dk, (h + 1) * dk) for h in heads]

    @pl.when(t == 0)
    def _():
        for ref in (val_s, kcd_s, qdec_s, kdec_s, qk_s, eg_s):
            ref[...] = jnp.zeros(ref.shape, ref.dtype)

    n_new = n_p // cps
    seq_j = _item_info(j, n_p, cps, cs, l_valid)[0]

    @pl.when(cj == 0)
    def _():
        s_ref[...] = jnp.where(seq_j < n_new, 0.0, s0_ref[0])

    sb = [s_ref[h].astype(BF16) for h in heads]
    ks_s = [_dot(kcd_s[:, sls[h]], sb[h]) for h in heads]
    qs_s = [_dot(qdec_s[:, sls[h]], sb[h]) for h in heads]

    halo = SUBLANES
    kpad = shift_ref.shape[1] // 2
    shift = shift_ref[...]

    def conv_silu(cur_ref, prev_ref, init_ref, w_ref):
        cur = cur_ref[...]
        rows = jnp.concatenate([jnp.where(first, init_ref[0], prev_ref[...]), cur,
                                jnp.zeros((kpad - halo - C, cur.shape[1]), F32)], axis=0)
        hi = rows.astype(BF16)
        lo = (rows - hi.astype(F32)).astype(BF16)
        shifted = _dot(shift, jnp.concatenate([hi, lo], axis=0))
        y = cur * w_ref[width - 1:width, :]
        for t in range(width - 1):
            y = y + shifted[t * C:(t + 1) * C] * w_ref[t:t + 1, :]
        return y * _sigmoid(y)

    qs = conv_silu(qc_ref, qp_ref, q0_ref, wq_ref)
    ks = conv_silu(kc_ref, kp_ref, k0_ref, wk_ref)
    vs = conv_silu(vc_ref, vp_ref, v0_ref, wv_ref)

    ub = [(val_s[:, sls[h]] - ks_s[h]).astype(BF16) for h in heads]
    ds_s = [_dot_tn(kdec_s[:, sls[h]], ub[h]) for h in heads]
    os_s = [qs_s[h] + _dot(qk_s[:, h * C:(h + 1) * C].astype(BF16), ub[h]) for h in heads]

    ab = ab_ref[...]
    rows = lax.broadcasted_iota(jnp.int32, (C, LANES), 0)
    valid = rows < nvalid
    xs = ab + dtb_ref[0]
    softplus = jnp.maximum(xs, 0.0) + jnp.log(1.0 + jnp.exp(-jnp.abs(xs)))
    g = jnp.where(valid, -jnp.exp(alog_ref[0]) * softplus, 0.0)
    beta = jnp.where(valid, _sigmoid(ab), 0.0)
    r = lax.broadcasted_iota(jnp.int32, (C, C), 0)
    cc = lax.broadcasted_iota(jnp.int32, (C, C), 1)
    incl = r >= cc
    strict = r > cc
    gcum = jnp.dot(jnp.where(incl, 1.0, 0.0), g, precision=_HI, preferred_element_type=F32)
    gcum_t = gcum.T

    q_ssq = _row_sums_of_squares([qs[:, sl] for sl in sls])
    k_ssq = _row_sums_of_squares([ks[:, sl] for sl in sls])
    qn = [qs[:, sls[h]] * lax.rsqrt(q_ssq[h] + NORM_EPS) * (dk ** -0.5) for h in heads]
    kn = [ks[:, sls[h]] * lax.rsqrt(k_ssq[h] + NORM_EPS) for h in heads]
    bcol = [beta[:, hg + h:hg + h + 1] for h in heads]
    gcol = [gcum[:, h:h + 1] for h in heads]
    glast = [gcum[C - 1:C, h:h + 1] for h in heads]
    eg = [jnp.exp(gcol[h]) for h in heads]
    decay = [jnp.where(incl, jnp.exp(jnp.where(incl, gcol[h] - gcum_t[h:h + 1, :], 0.0)), 0.0) for h in heads]
    kb = [kn[h] * bcol[h] for h in heads]
    knb = [kn[h].astype(BF16) for h in heads]
    a_mat = [jnp.where(strict, _dot_nt(kb[h].astype(BF16), knb[h]) * decay[h], 0.0) for h in heads]

    gain = gain_ref[...]
    for h in heads:
        s_ref[h] = s_ref[h] * eg_s[h:h + 1, :] + ds_s[h]
    for h in heads:
        o = os_s[h]
        z = z_ref[:, sls[h]]
        ms = jnp.mean(o * o, axis=-1, keepdims=True)
        o_ref[:, sls[h]] = (o * lax.rsqrt(ms + NORM_EPS) * gain * (z * _sigmoid(z))).astype(o_ref.dtype)

    t_mat = _tri_inverse_many(a_mat)
    rhs = [jnp.concatenate([vs[:, sls[h]] * bcol[h], kb[h] * eg[h]], axis=1).astype(BF16) for h in heads]
    tv = [_dot(t_mat[h].astype(BF16), rhs[h]) for h in heads]
    qk = [_dot_nt(qn[h].astype(BF16), knb[h]) * decay[h] for h in heads]
    for h in heads:
        val_s[:, sls[h]] = tv[h][:, :dk]
        kcd_s[:, sls[h]] = tv[h][:, dk:].astype(BF16)
        qdec_s[:, sls[h]] = (qn[h] * eg[h]).astype(BF16)
        kdec_s[:, sls[h]] = (kn[h] * jnp.exp(glast[h] - gcol[h])).astype(BF16)
        eg_s[h:h + 1, :] = jnp.broadcast_to(jnp.exp(glast[h]), (1, LANES))
    qk_s[...] = jnp.concatenate(qk, axis=1)

    done = (cj == ncj - 1) & (t > 0)

    @pl.when(done & (seq_j < n_new))
    def _():
        snew_ref[0] = s_ref[...]

    @pl.when(done & (seq_j >= n_new))
    def _():
        srun_ref[0] = s_ref[...]

    last_c_new = (l_valid - 1) // C
    end_new = l_valid - last_c_new * C
    is_new = i < n_p

    @pl.when(is_new & (c == last_c_new))
    def _():
        for dst, src in ((cq_ref, qc_ref), (ck_ref, kc_ref), (cv_ref, vc_ref)):
            dst[0] = src[end_new - SUBLANES:end_new, :]

    @pl.when(jnp.logical_not(is_new) & (c == cs - 1))
    def _():
        for dst, src in ((cq_ref, qc_ref), (ck_ref, kc_ref), (cv_ref, vc_ref)):
            dst[0] = src[C - SUBLANES:C, :]


def _gdn_mixer(proj, gates, conv0, s0, conv_w, a_log, dt_bias, o_gain, *, layer, prev_states, heads, dk, hg,
               n_p, cps, cs, l_valid):
    m = proj.shape[0]
    C = CHUNK
    w = heads * dk
    n_hg = heads // hg
    gw = hg * dk
    n_items = m // C
    n_layers, n_run = s0.shape[:2]
    assert (l_valid - (l_valid - 1) // CHUNK * CHUNK) % SUBLANES == 0
    n_new = n_p // cps
    width = conv_w.shape[0]
    info = dict(n_p=n_p, cps=cps, cs=cs, l_valid=l_valid)

    def seq_of(i):
        return _item_info(i, n_p, cps, cs, l_valid)[0]

    prep = lambda t: jnp.minimum(t, n_items - 1)
    scan = lambda t: jnp.maximum(t - 1, 0)
    cur = lambda off: pl.BlockSpec((C, gw), lambda g, t: (prep(t), off * n_hg + g))
    prev = lambda off: pl.BlockSpec(
        (SUBLANES, gw), lambda g, t: (jnp.maximum(prep(t) * (C // SUBLANES) - 1, 0), off * n_hg + g))
    init = lambda off: pl.BlockSpec((1, SUBLANES, gw), lambda g, t: (seq_of(prep(t)), 0, off * n_hg + g))
    wspec = lambda off: pl.BlockSpec((width, gw), lambda g, t: (0, off * n_hg + g))
    gate_vec = pl.BlockSpec((1, 1, LANES), lambda g, t: (g, 0, 0))
    state_new = pl.BlockSpec((None, 1, hg, dk, dk),
                             lambda g, t: (layer, jnp.minimum(seq_of(scan(t)), n_new - 1), g, 0, 0))
    state_run = pl.BlockSpec((None, 1, hg, dk, dk),
                             lambda g, t: (layer, jnp.maximum(seq_of(scan(t)) - n_new, 0), g, 0, 0))

    kpad = _round_up(SUBLANES + C, LANES)
    r = jnp.arange((width - 1) * C)
    src = (r % C) + SUBLANES - (width - 1) + r // C
    col = jnp.arange(2 * kpad)
    shift = ((col[None, :] == src[:, None]) | (col[None, :] == src[:, None] + kpad)).astype(BF16)

    alog_g = jnp.pad(a_log.reshape(n_hg, 1, hg).astype(F32), ((0, 0), (0, 0), (0, LANES - hg)))
    dtb_g = jnp.pad(dt_bias.reshape(n_hg, 1, hg).astype(F32), ((0, 0), (0, 0), (0, LANES - hg)))

    n_in = 19
    carried = list(prev_states) if prev_states is not None else []

    def body(*refs):
        _gdn_kernel(*refs[:n_in], *refs[n_in + len(carried):],
                    hg=hg, dk=dk, width=width, n_items=n_items, **info)

    return pl.pallas_call(
        body,
        out_shape=(jax.ShapeDtypeStruct((m, w), BF16),
                   jax.ShapeDtypeStruct((n_layers, n_new, heads, dk, dk), F32),
                   jax.ShapeDtypeStruct((n_layers, n_run, heads, dk, dk), F32))
        + (jax.ShapeDtypeStruct((n_new + n_run, SUBLANES, w), F32),) * 3,
        input_output_aliases={n_in + i: 1 + i for i in range(len(carried))},
        grid=(n_hg, n_items + 1),
        in_specs=[cur(0), cur(1), cur(2), prev(0), prev(1), prev(2), init(0), init(1), init(2),
                  wspec(0), wspec(1), wspec(2),
                  pl.BlockSpec((C, LANES), lambda g, t: (prep(t), g)),
                  gate_vec, gate_vec,
                  pl.BlockSpec(shift.shape, lambda g, t: (0, 0)),
                  pl.BlockSpec((C, gw), lambda g, t: (scan(t), 3 * n_hg + g)),
                  pl.BlockSpec((1, dk), lambda g, t: (0, 0)),
                  state_run] + [pl.BlockSpec(memory_space=pl.ANY)] * len(carried),
        out_specs=(pl.BlockSpec((C, gw), lambda g, t: (scan(t), g)), state_new, state_run)
        + (pl.BlockSpec((1, SUBLANES, gw), lambda g, t: (seq_of(prep(t)), 0, g)),) * 3,
        scratch_shapes=[pltpu.VMEM((C, gw), F32), pltpu.VMEM((C, gw), BF16), pltpu.VMEM((C, gw), BF16),
                        pltpu.VMEM((C, gw), BF16), pltpu.VMEM((C, hg * C), F32), pltpu.VMEM((hg, LANES), F32),
                        pltpu.VMEM((hg, dk, dk), F32)],
        compiler_params=_params(("parallel", "arbitrary")),
        name="gdn_mixer",
    )(proj, proj, proj, proj, proj, proj, conv0, conv0, conv0, conv_w, conv_w, conv_w,
      gates, alog_g, dtb_g, shift, proj, o_gain.reshape(1, dk).astype(F32), s0, *carried)


def _split_half_layout(x, half):
    pad = [(0, 0)] * (x.ndim - 1) + [(0, LANES // 2 - half)]
    return jnp.concatenate([jnp.pad(x[..., :half], pad), jnp.pad(x[..., half:], pad)], axis=-1)


def _rope_tables(pos, half):
    inv_freq = 1.0 / (ROPE_THETA ** (jnp.arange(half, dtype=F32) / half))
    ang = pos[:, None] * inv_freq[None, :]
    cos, sin = jnp.cos(ang), jnp.sin(ang)
    cos_t = _split_half_layout(jnp.concatenate([cos, cos], -1), half)
    sin_t = _split_half_layout(jnp.concatenate([-sin, sin], -1), half)
    return cos_t, sin_t


def kernel(x_prompt, x_sample, state_conv, state_delta, cache_ckv, cache_krope, meta_tokens,
           a_w_in, a_conv_w, a_a_log, a_dt_bias, a_o_gain, a_w_o,
           b_w_dq, b_q_gain, b_w_uq, b_w_o,
           kv_w_dkv, kv_gain, kv_w_uk, kv_w_uv,
           mlp_w_up, mlp_w_down, ln_gain, ln_bias):
    B, seq, D = x_prompt.shape
    DB, ts, _ = x_sample.shape
    n_meta = meta_tokens.shape[0]
    depth = ln_gain.shape[0]
    n_a = a_w_in.shape[0]
    H = a_a_log.shape[1]
    dk = a_o_gain.shape[1]
    W = H * dk
    width = a_conv_w.shape[1]
    past = cache_ckv.shape[1]
    KV, MH, nope = kv_w_uk.shape
    vd = kv_w_uv.shape[2]
    rope = cache_krope.shape[2]
    half = rope // 2
    alpha = (2 * depth) ** 0.25
    scale = (nope + rope) ** -0.5
    assert ts % CHUNK == 0 and ts >= width - 1 and half < LANES // 2
    assert nope % LANES == 0 and vd % LANES == 0 and KV % LANES == 0 and dk % LANES == 0

    L = n_meta + seq
    LP = _round_up(L, LANES)
    n_prompt_rows = B * LP
    M = n_prompt_rows + DB * ts
    cps, cs = LP // CHUNK, ts // CHUNK
    n_p = B * cps
    hg = min(32, H)
    n_hg = H // hg

    pieces = []
    for b in range(B):
        pieces += [meta_tokens.astype(F32), x_prompt[b], jnp.zeros((LP - L, D), F32)]
    x = jnp.concatenate(pieces + [x_sample.reshape(DB * ts, D)], axis=0)
    xb = x.astype(BF16)

    pos = jnp.concatenate([jnp.tile(jnp.arange(LP, dtype=F32), B),
                           jnp.tile(past + jnp.arange(ts, dtype=F32), DB)])
    cos_t, sin_t = _rope_tables(pos, half)

    w_in_t = jnp.swapaxes(a_w_in, 1, 2)

    conv0 = jnp.concatenate([jnp.zeros((n_a, B, width - 1, 3 * W), F32), state_conv.astype(F32)], axis=1)
    conv0 = jnp.pad(conv0, ((0, 0), (0, 0), (SUBLANES - (width - 1), 0), (0, 0)))
    s0 = state_delta.astype(F32)

    new_conv_p, new_conv_s, states = [], [], None
    k_slabs = None
    lat = None
    for layer in range(depth):
        if layer < n_a:
            w_ab = a_w_in[layer, :, 4 * W:]
            gate_w = jnp.concatenate(
                [w_ab[:, :H].reshape(D, n_hg, hg), w_ab[:, H:].reshape(D, n_hg, hg),
                 jnp.zeros((D, n_hg, LANES - 2 * hg), w_ab.dtype)], axis=-1).reshape(D, n_hg * LANES)
            proj = _matmul(xb, w_in_t, F32, layer=layer, n_use=4 * W, b_transposed=True)
            gates = _matmul(xb, gate_w.astype(BF16), F32, bm=328)
            o, s_new, s_run, *conv_tail = _gdn_mixer(
                proj, gates, conv0[layer], s0, a_conv_w[layer].astype(F32),
                a_a_log[layer], a_dt_bias[layer], a_o_gain[layer],
                layer=layer, prev_states=states, heads=H, dk=dk, hg=hg,
                n_p=n_p, cps=cps, cs=cs, l_valid=L)
            states = (s_new, s_run)
            h = _matmul(o, a_w_o, BF16, layer=layer)
            conv_new = jnp.concatenate(conv_tail, axis=-1)[:, SUBLANES - (width - 1):]
            new_conv_p.append(conv_new[:B])
            new_conv_s.append(conv_new[B:])
        else:
            j = layer - n_a
            ql = _matmul(xb, b_w_dq[j].astype(BF16), BF16, epilogue="rms", gain=b_q_gain[j])
            w_uq = b_w_uq[j].reshape(-1, MH, nope + rope)
            w_uq = jnp.concatenate([w_uq[..., :nope], _split_half_layout(w_uq[..., nope:], half)], axis=-1)
            q = _matmul(ql, w_uq.reshape(-1, MH * (nope + LANES)).astype(BF16), BF16, bn=2048, epilogue="rope",
                        rope=(cos_t, sin_t, nope // LANES + 1, nope // LANES, scale * math.log2(math.e)))
            kp, vp, ks, ls_t, vs, lkp, lks = k_slabs
            o = _attention_causal(q, kp, vp, nb=B, lq=LP, lk_pad=lkp, heads=MH, nope=nope,
                                  vd=vd, bq=_pick_block(LP, 1408, LANES), bk=512, n_meta=n_meta)
            o = _attention_full(q, ks, ls_t, vs, o, nb=DB, lq=ts, lk_pad=lks, row0=n_prompt_rows,
                                heads=MH, hb=min(8, MH), nope=nope, vd=vd, kv=KV, n_keys=past + ts)
            h = _matmul(o, b_w_o, BF16, layer=j)

        x, xb = _deepnorm(x, h, ln_gain[layer, 0], ln_bias[layer, 0], alpha)
        hid = _matmul(xb, mlp_w_up, BF16, layer=layer, epilogue="relu2")
        h = _matmul(hid, mlp_w_down, BF16, layer=layer, bn=1024, bk=2048)
        x, xb = _deepnorm(x, h, ln_gain[layer, 1], ln_bias[layer, 1], alpha)

        if layer == n_a - 1:
            w_dkv = jnp.concatenate([kv_w_dkv[:, :KV], _split_half_layout(kv_w_dkv[:, KV:], half)], axis=1)
            lat = _kv_post(_matmul(xb, w_dkv.astype(BF16), F32, bm=328, bn=640), kv_gain.astype(F32),
                           cos_t, sin_t, KV)
            lane_ids = jnp.arange(LANES)
            eye = jnp.where((lane_ids[:, None] == lane_ids[None, :]) & (lane_ids[:, None] != half), 1.0, 0.0)
            ones_col = jnp.where((lane_ids[:, None] == half) & (lane_ids[None, :] == 0), 1.0, 0.0)
            w_k = jnp.concatenate(
                [jnp.concatenate([kv_w_uk.astype(F32), jnp.zeros((KV, MH, LANES), F32)], axis=-1),
                 jnp.concatenate([jnp.zeros((LANES, MH, nope), F32),
                                  jnp.broadcast_to(eye[:, None, :], (LANES, MH, LANES))], axis=-1)],
                axis=0).reshape(KV + LANES, MH * (nope + LANES)).astype(BF16)
            w_v = jnp.concatenate(
                [jnp.concatenate([kv_w_uv.astype(F32), jnp.zeros((KV, MH, LANES), F32)], axis=-1),
                 jnp.concatenate([jnp.zeros((LANES, MH, vd), F32),
                                  jnp.broadcast_to(ones_col[:, None, :], (LANES, MH, LANES))], axis=-1)],
                axis=0).reshape(KV + LANES, MH * (vd + LANES)).astype(BF16)
            one_lane = jnp.where(jnp.arange(KV + LANES) == KV + half, 1.0, 0.0)
            lkp = _round_up(L, 512)
            lat_p = lat[:n_prompt_rows].reshape(B, LP, KV + LANES)
            if lkp >= LP:
                lat_p = jnp.pad(lat_p, ((0, 0), (0, lkp - LP), (0, 0)))
            else:
                lat_p = lat_p[:, :lkp]
            lat_p = (lat_p.reshape(B * lkp, KV + LANES) + one_lane).astype(BF16)
            lks = _round_up(past + ts, LANES)
            cache = jnp.concatenate([cache_ckv.astype(F32), _split_half_layout(cache_krope.astype(F32), half)],
                                    axis=-1)
            lat_s = jnp.concatenate([cache, lat[n_prompt_rows:].reshape(DB, ts, KV + LANES),
                                     jnp.zeros((DB, lks - past - ts, KV + LANES), F32)], axis=1)
            lat_s = (lat_s.reshape(DB * lks, KV + LANES) + one_lane).astype(BF16)
            w_kn_t = jnp.transpose(kv_w_uk, (1, 2, 0)).reshape(MH * nope, KV).astype(BF16)
            lat_s_t = lat_s.T
            big = dict(bm=2048, bn=2048)
            k_slabs = (_matmul(w_k.T, lat_p.T, BF16, bm=2048, bn=1536), _matmul(lat_p, w_v, BF16, bm=1536, bn=2048),
                       _matmul(w_kn_t, lat_s_t, BF16, **big), lat_s_t,
                       _matmul(lat_s, kv_w_uv.reshape(KV, MH * vd).astype(BF16), BF16, **big),
                       lkp, lks)

    def unsplit(r):
        return jnp.concatenate([r[..., :half], r[..., LANES // 2:LANES // 2 + half]], axis=-1)

    lat_p = lat[:n_prompt_rows].reshape(B, LP, KV + LANES)[:, :L]
    lat_s = lat[n_prompt_rows:].reshape(DB, ts, KV + LANES)
    y_prompt = jnp.stack([x[b * LP + n_meta:b * LP + L] for b in range(B)])
    y_sample = x[n_prompt_rows:].reshape(DB, ts, D)
    return (y_prompt, y_sample,
            jnp.stack(new_conv_p), states[0],
            lat_p[..., :KV], unsplit(lat_p[..., KV:]),
            jnp.stack(new_conv_s), states[1],
            lat_s[..., :KV], unsplit(lat_s[..., KV:]))
```

```python
import functools
import math

import jax
import jax.numpy as jnp
from jax import lax
from jax.experimental import pallas as pl
from jax.experimental.pallas import tpu as pltpu

CHUNK = 64
ROPE_THETA = 10000.0
NORM_EPS = 1e-6
NEG_INF = -1e30
LANES = 128
SUBLANES = 8
VMEM_LIMIT_BYTES = 56 * 1024 * 1024
_SPLIT_LEVELS = 2
_CHUNK_SHIFT = CHUNK.bit_length() - 1
assert 1 << _CHUNK_SHIFT == CHUNK

F32 = jnp.float32
BF16 = jnp.bfloat16
_HI = lax.Precision.HIGHEST


def _round_up(x, m):
    return -(-x // m) * m


def _pick_block(dim, target, align):
    best = None
    for d in range(align, min(dim, target) + 1, align):
        if dim % d == 0:
            best = d
    return best if best is not None else dim


def _dot(a, b):
    return jnp.dot(a, b, preferred_element_type=F32)


def _dot_nt(a, b):
    return lax.dot_general(a, b, (((1,), (1,)), ((), ())), preferred_element_type=F32)


def _dot_tn(a, b):
    return lax.dot_general(a, b, (((0,), (0,)), ((), ())), preferred_element_type=F32)


def _sigmoid(x):
    return 1.0 / (1.0 + jnp.exp(-x))


def _params(sem):
    return pltpu.CompilerParams(dimension_semantics=sem, vmem_limit_bytes=VMEM_LIMIT_BYTES)


def _mm_kernel(*refs, nk, epilogue, b_transposed, rope=None):
    n_extra = {"rms": 1, "rope": 2}.get(epilogue, 0)
    a_ref, b_ref = refs[:2]
    extra = refs[2:2 + n_extra]
    o_ref = refs[2 + n_extra]
    rest = refs[3 + n_extra:]

    def finish(acc):
        if epilogue == "relu2":
            r = jnp.maximum(acc, 0.0)
            acc = r * r
        elif epilogue == "rms":
            ms = jnp.mean(acc * acc, axis=-1, keepdims=True)
            acc = acc * lax.rsqrt(ms + NORM_EPS) * extra[0][...]
        elif epilogue == "rope":
            chunks_per_head, rope_chunk, scale = rope
            cos, sin = extra[0][...], extra[1][...]
            cols = []
            for c in range(acc.shape[1] // LANES):
                x = acc[:, c * LANES:(c + 1) * LANES]
                if c % chunks_per_head == rope_chunk:
                    x = x * cos + pltpu.roll(x, LANES // 2, axis=1) * sin
                cols.append(x * scale)
            acc = jnp.concatenate(cols, axis=1)
        o_ref[...] = acc.astype(o_ref.dtype)

    def product():
        b = b_ref[...].astype(BF16)
        return _dot_nt(a_ref[...], b) if b_transposed else _dot(a_ref[...], b)

    if nk == 1:
        finish(product())
    else:
        acc_ref = rest[0]
        k = pl.program_id(2)

        @pl.when(k == 0)
        def _():
            acc_ref[...] = jnp.zeros_like(acc_ref)

        acc_ref[...] += product()

        @pl.when(k == nk - 1)
        def _():
            finish(acc_ref[...])


def _matmul(a, b, out_dtype, *, layer=None, n_use=None, b_transposed=False, epilogue="none", gain=None,
            rope=None, bm=1312, bn=512, bk=4096):
    m = a.shape[0]
    kdim = min(a.shape[1], b.shape[-1 if b_transposed else -2])
    n = n_use if n_use is not None else b.shape[-2 if b_transposed else -1]
    bm = _pick_block(m, bm, 16)
    bn = n if epilogue == "rms" else _pick_block(n, bn, LANES)
    bk = _pick_block(kdim, bk, LANES)
    nk = kdim // bk
    a_mode = dict(pipeline_mode=pl.Buffered(1)) if nk == 1 and n // bn >= 4 else {}
    b_blk = (bn, bk) if b_transposed else (bk, bn)
    b_idx = (lambda k, j: (j, k)) if b_transposed else (lambda k, j: (k, j))
    if b.ndim == 3:
        b_spec = pl.BlockSpec((None,) + b_blk, lambda i, j, k: (layer,) + b_idx(k, j))
    else:
        b_spec = pl.BlockSpec(b_blk, lambda i, j, k: b_idx(k, j))
    in_specs = [pl.BlockSpec((bm, bk), lambda i, j, k: (i, k), **a_mode), b_spec]
    args = [a, b]
    if epilogue == "rms":
        in_specs.append(pl.BlockSpec((1, bn), lambda i, j, k: (0, j)))
        args.append(gain.reshape(1, n).astype(F32))
    rope_static = None
    if epilogue == "rope":
        cos_t, sin_t, *rope_static = rope
        in_specs += [pl.BlockSpec((bm, LANES), lambda i, j, k: (i, 0))] * 2
        args += [cos_t, sin_t]
        assert bn % (rope_static[0] * LANES) == 0
    scratch = [pltpu.VMEM((bm, bn), F32)] if nk > 1 else []
    return pl.pallas_call(
        functools.partial(_mm_kernel, nk=nk, epilogue=epilogue, b_transposed=b_transposed,
                          rope=tuple(rope_static) if rope_static else None),
        out_shape=jax.ShapeDtypeStruct((m, n), out_dtype),
        grid=(m // bm, n // bn, nk),
        in_specs=in_specs,
        out_specs=pl.BlockSpec((bm, bn), lambda i, j, k: (i, j)),
        scratch_shapes=scratch,
        compiler_params=_params(("parallel", "parallel", "arbitrary")),
        name="matmul_" + epilogue,
    )(*args)


def _ln_kernel(x_ref, h_ref, g_ref, b_ref, of_ref, ob_ref, *, alpha):
    y = alpha * x_ref[...] + h_ref[...].astype(F32)
    mu = jnp.mean(y, axis=-1, keepdims=True)
    d = y - mu
    var = jnp.mean(d * d, axis=-1, keepdims=True)
    out = d * lax.rsqrt(var + NORM_EPS) * g_ref[...] + b_ref[...]
    of_ref[...] = out
    ob_ref[...] = out.astype(BF16)


def _deepnorm(x, h, gain, bias, alpha):
    m, d = x.shape
    br = _pick_block(m, 256, 16)
    row = pl.BlockSpec((br, d), lambda i: (i, 0))
    vec = pl.BlockSpec((1, d), lambda i: (0, 0))
    return pl.pallas_call(
        functools.partial(_ln_kernel, alpha=alpha),
        out_shape=(jax.ShapeDtypeStruct((m, d), F32), jax.ShapeDtypeStruct((m, d), BF16)),
        grid=(m // br,),
        in_specs=[row, row, vec, vec],
        out_specs=(row, row),
        compiler_params=_params(("parallel",)),
        name="deepnorm",
    )(x, h, gain.reshape(1, d), bias.reshape(1, d))


def _kv_post_kernel(lat_ref, g_ref, cos_ref, sin_ref, o_ref, *, kv):
    c = lat_ref[:, :kv]
    ms = jnp.mean(c * c, axis=-1, keepdims=True)
    o_ref[:, :kv] = c * lax.rsqrt(ms + NORM_EPS) * g_ref[...]
    r = lat_ref[:, kv:]
    o_ref[:, kv:] = r * cos_ref[...] + pltpu.roll(r, LANES // 2, axis=1) * sin_ref[...]


def _kv_post(lat, gain, cos_t, sin_t, kv):
    m, w = lat.shape
    br = _pick_block(m, 256, 8)
    return pl.pallas_call(
        functools.partial(_kv_post_kernel, kv=kv),
        out_shape=jax.ShapeDtypeStruct((m, w), F32),
        grid=(m // br,),
        in_specs=[pl.BlockSpec((br, w), lambda i: (i, 0)),
                  pl.BlockSpec((1, kv), lambda i: (0, 0)),
                  pl.BlockSpec((br, LANES), lambda i: (i, 0)),
                  pl.BlockSpec((br, LANES), lambda i: (i, 0))],
        out_specs=pl.BlockSpec((br, w), lambda i: (i, 0)),
        compiler_params=_params(("parallel",)),
        name="kv_post",
    )(lat, gain.reshape(1, kv), cos_t, sin_t)


def _attn_causal_kernel(q_ref, kt_ref, v_ref, o_ref,
                        s_scr, p_scr, m_scr, a_scr, acc_scr, vis_scr, *, bq, bk, rt, n_groups, vd, n_meta, lk_pad):
    qi = pl.program_id(2)
    tiles = [slice(t * rt, (t + 1) * rt) for t in range(bq // rt)]
    lane_chunks = [slice(c * LANES, (c + 1) * LANES) for c in range(bk // LANES)]

    def visible_end(r):
        chunk_id = lax.shift_right_arithmetic(r - n_meta, _CHUNK_SHIFT) + 1
        return jnp.where(r < n_meta, n_meta, chunk_id * CHUNK + n_meta)

    n_full = visible_end(qi * bq) // bk
    nkb = (jnp.minimum(visible_end(qi * bq + bq - 1), lk_pad) + bk - 1) // bk

    for t, rows in enumerate(tiles):
        vis_scr[rows, :] = visible_end(qi * bq + t * rt + lax.broadcasted_iota(jnp.int32, (rt, LANES), 0))
    m_scr[...] = jnp.full(m_scr.shape, NEG_INF, F32)
    acc_scr[...] = jnp.zeros(acc_scr.shape, F32)

    gsz = bq // n_groups
    groups = [slice(g * gsz, (g + 1) * gsz) for g in range(n_groups)]
    tiles_per_group = gsz // rt
    last_block = lk_pad // bk - 1
    last_g = n_groups - 1

    def block_start(j):
        return pl.multiple_of(j * bk, bk)

    def scores(g, j):
        s_scr[groups[g], :] = _dot(q_ref[groups[g], :], kt_ref[:, pl.ds(block_start(j), bk)])

    def values(g, j):
        rows = groups[g]
        a = a_scr[rows, :]
        acc_scr[rows, :] = (jnp.concatenate([a] * (acc_scr.shape[1] // LANES), axis=1) * acc_scr[rows, :]
                            + _dot(p_scr[rows, :], v_ref[pl.ds(block_start(j), bk), :]))

    p_scr[groups[last_g], :] = jnp.zeros((gsz, bk), BF16)
    a_scr[groups[last_g], :] = jnp.ones((gsz, LANES), F32)
    scores(0, 0)

    def step(j, carry, masked):
        start = block_start(j)

        def softmax(g):
            for t in range(g * tiles_per_group, (g + 1) * tiles_per_group):
                rows = tiles[t]
                s = s_scr[rows, :]
                if masked:
                    vis = vis_scr[rows, :]
                    lane = start + lax.broadcasted_iota(jnp.int32, (1, LANES), 1)
                    s = jnp.concatenate([jnp.where(lane + c * LANES < vis, s[:, lc], NEG_INF)
                                         for c, lc in enumerate(lane_chunks)], axis=1)
                m_old = m_scr[rows, :]
                m_new = jnp.maximum(m_old, jnp.max(s, axis=-1, keepdims=True))
                for lc in lane_chunks:
                    p_scr[rows, lc] = jnp.exp2(s[:, lc] - m_new).astype(BF16)
                a_scr[rows, :] = jnp.exp2(m_old - m_new)
                m_scr[rows, :] = m_new

        values(last_g, jnp.maximum(j - 1, 0))
        for g in range(n_groups):
            if g < last_g:
                scores(g + 1, j)
            else:
                scores(0, jnp.minimum(j + 1, last_block))
            softmax(g)
            if g < last_g:
                values(g, j)
        return carry

    lax.fori_loop(0, n_full, functools.partial(step, masked=False), 0)
    lax.fori_loop(n_full, nkb, functools.partial(step, masked=True), 0)
    values(last_g, nkb - 1)
    for rows in tiles:
        o_ref[rows, :] = (acc_scr[rows, :vd] / acc_scr[rows, vd:vd + 1]).astype(o_ref.dtype)


def _attn_full_kernel(q_ref, knt_ref, krt_ref, v_ref, o_ref, *, hb, nope, vd, n_keys):
    dqk = nope + LANES
    lq = q_ref.shape[0]
    heads = range(hb)
    kpos = lax.broadcasted_iota(jnp.int32, (1, knt_ref.shape[1]), 1)
    qs = [q_ref[:, h * dqk:(h + 1) * dqk] for h in heads]
    s_rope = _dot(jnp.concatenate([q[:, nope:] for q in qs], axis=0), krt_ref[...])
    ss = [jnp.where(kpos < n_keys,
                    _dot(qs[h][:, :nope], knt_ref[h * nope:(h + 1) * nope, :]) + s_rope[h * lq:(h + 1) * lq],
                    NEG_INF) for h in heads]
    es = [jnp.exp2(s - jnp.max(s, axis=-1, keepdims=True)) for s in ss]
    ls = [jnp.sum(e, axis=-1, keepdims=True) for e in es]
    pv = [_dot(es[h].astype(BF16), v_ref[:, h * vd:(h + 1) * vd]) for h in heads]
    for h in heads:
        o_ref[:, h * vd:(h + 1) * vd] = (pv[h] / ls[h]).astype(o_ref.dtype)


def _attention_causal(q, kt, vf, *, nb, lq, lk_pad, heads, nope, vd, bq, bk, n_meta):
    dqk = nope + LANES
    nq = lq // bq
    rt = 32 if bq % 32 == 0 else bq
    n_groups = 4 if bq % (4 * rt) == 0 else 2
    assert bq % (n_groups * rt) == 0
    def body(q_ref, kt_ref, v_ref, o_all_ref, o_ref, *scratch):
        _attn_causal_kernel(q_ref, kt_ref, v_ref, o_ref, *scratch, bq=bq, bk=bk, rt=rt,
                            n_groups=n_groups, vd=vd, n_meta=n_meta, lk_pad=lk_pad)

    o_all = jnp.zeros((q.shape[0], heads * vd), BF16)
    return pl.pallas_call(
        body,
        out_shape=jax.ShapeDtypeStruct(o_all.shape, o_all.dtype),
        grid=(nb, heads, nq),
        in_specs=[pl.BlockSpec((bq, dqk), lambda b, h, i: (b * nq + i, h)),
                  pl.BlockSpec((dqk, lk_pad), lambda b, h, i: (h, b)),
                  pl.BlockSpec((lk_pad, vd + LANES), lambda b, h, i: (b, h)),
                  pl.BlockSpec(memory_space=pl.ANY)],
        out_specs=pl.BlockSpec((bq, vd), lambda b, h, i: (b * nq + i, h)),
        input_output_aliases={3: 0},
        scratch_shapes=[pltpu.VMEM((bq, bk), F32), pltpu.VMEM((bq, bk), BF16),
                        pltpu.VMEM((bq, LANES), F32), pltpu.VMEM((bq, LANES), F32),
                        pltpu.VMEM((bq, vd + LANES), F32), pltpu.VMEM((bq, LANES), jnp.int32)],
        compiler_params=_params(("parallel", "parallel", "arbitrary")),
        name="mla_attention_causal",
    )(q, kt, vf, o_all)


def _attention_full(q, knt, lat_t, vf, o_all, *, nb, lq, lk_pad, row0, heads, hb, nope, vd, kv, n_keys):
    dqk = nope + LANES
    blk0 = row0 // lq

    def body(q_ref, knt_ref, krt_ref, v_ref, o_all_ref, o_ref):
        _attn_full_kernel(q_ref, knt_ref, krt_ref, v_ref, o_ref, hb=hb, nope=nope, vd=vd, n_keys=n_keys)

    return pl.pallas_call(
        body,
        out_shape=jax.ShapeDtypeStruct(o_all.shape, o_all.dtype),
        grid=(nb, heads // hb),
        in_specs=[pl.BlockSpec((lq, hb * dqk), lambda b, g: (blk0 + b, g)),
                  pl.BlockSpec((hb * nope, lk_pad), lambda b, g: (g, b)),
                  pl.BlockSpec((LANES, lk_pad), lambda b, g: (kv // LANES, b)),
                  pl.BlockSpec((lk_pad, hb * vd), lambda b, g: (b, g)),
                  pl.BlockSpec(memory_space=pl.ANY)],
        out_specs=pl.BlockSpec((lq, hb * vd), lambda b, g: (blk0 + b, g)),
        input_output_aliases={4: 0},
        compiler_params=_params(("parallel", "parallel")),
        name="mla_attention_full",
    )(q, knt, lat_t, vf, o_all)


def _item_info(i, n_p, cps, cs, l_valid):
    is_p = i < n_p
    j = i - n_p
    seq = jnp.where(is_p, i // cps, n_p // cps + j // cs)
    c = jnp.where(is_p, i % cps, j % cs)
    nc = jnp.where(is_p, cps, cs)
    nvalid = jnp.where(is_p, jnp.clip(l_valid - c * CHUNK, 0, CHUNK), CHUNK)
    return seq, c, nc, nvalid


def _row_sums_of_squares(xs):
    rows, n = xs[0].shape
    parts = []
    for x in xs:
        sq = x * x
        hi = sq.astype(BF16)
        parts.append(jnp.concatenate([hi, (sq - hi.astype(F32)).astype(BF16)], axis=1))
    tot = _dot(jnp.concatenate(parts, axis=0), jnp.ones((2 * n, n), BF16))
    return [tot[i * rows:(i + 1) * rows] for i in range(len(xs))]


def _tri_inverse_many(mats):
    n = mats[0].shape[0]
    r = lax.broadcasted_iota(jnp.int32, (n, n), 0)
    c = lax.broadcasted_iota(jnp.int32, (n, n), 1)
    eye = jnp.where(r == c, 1.0, 0.0)
    zero = jnp.zeros((n, n), F32)

    def split(x):
        hi = x.astype(BF16).astype(F32)
        return hi, x - hi

    def lhs_of(x):
        hi, lo = split(x)
        return jnp.concatenate([hi, lo, hi, zero], axis=1).astype(BF16)

    def rhs_of(x):
        hi, lo = split(x)
        return jnp.concatenate([hi, hi, lo, zero], axis=0).astype(BF16)

    ps = [eye - a for a in mats]
    aks = list(mats)
    rhs = [rhs_of(a) for a in mats]
    power, level = 2, 0
    while power < n:
        if level < _SPLIT_LEVELS:
            aks = [_dot(lhs_of(ak), rk) for ak, rk in zip(aks, rhs)]
            rhs = [rhs_of(ak) for ak in aks]
            ps = [p + _dot(lhs_of(p), rk) for p, rk in zip(ps, rhs)]
        else:
            akb = [ak.astype(BF16) for ak in aks]
            aks = [_dot(b, b) for b in akb]
            ps = [p + _dot(p.astype(BF16), ak.astype(BF16)) for p, ak in zip(ps, aks)]
        power *= 2
        level += 1
    return ps


def _gdn_kernel(qc_ref, kc_ref, vc_ref, qp_ref, kp_ref, vp_ref, q0_ref, k0_ref, v0_ref,
                wq_ref, wk_ref, wv_ref, ab_ref, alog_ref, dtb_ref, shift_ref, z_ref, gain_ref, s0_ref,
                o_ref, snew_ref, srun_ref, cq_ref, ck_ref, cv_ref,
                val_s, kcd_s, qdec_s, kdec_s, qk_s, eg_s, s_ref,
                *, hg, dk, n_items, n_p, cps, cs, l_valid, width):
    C = CHUNK
    t = pl.program_id(1)
    i = jnp.minimum(t, n_items - 1)
    j = jnp.maximum(t - 1, 0)
    _, c, _, nvalid = _item_info(i, n_p, cps, cs, l_valid)
    _, cj, ncj, _ = _item_info(j, n_p, cps, cs, l_valid)
    first = c == 0
    heads = range(hg)
    sls = [slice(h * dk, (h + 1) * dk) for h in heads]

    @pl.when(t == 0)
    def _():
        for ref in (val_s, kcd_s, qdec_s, kdec_s, qk_s, eg_s):
            ref[...] = jnp.zeros(ref.shape, ref.dtype)

    n_new = n_p // cps
    seq_j = _item_info(j, n_p, cps, cs, l_valid)[0]

    @pl.when(cj == 0)
    def _():
        s_ref[...] = jnp.where(seq_j < n_new, 0.0, s0_ref[0])

    sb = [s_ref[h].astype(BF16) for h in heads]
    ks_s = [_dot(kcd_s[:, sls[h]], sb[h]) for h in heads]
    qs_s = [_dot(qdec_s[:, sls[h]], sb[h]) for h in heads]

    halo = SUBLANES
    kpad = shift_ref.shape[1] // 2
    shift = shift_ref[...]

    def conv_silu(cur_ref, prev_ref, init_ref, w_ref):
        cur = cur_ref[...]
        rows = jnp.concatenate([jnp.where(first, init_ref[0], prev_ref[...]), cur,
                                jnp.zeros((kpad - halo - C, cur.shape[1]), F32)], axis=0)
        hi = rows.astype(BF16)
        lo = (rows - hi.astype(F32)).astype(BF16)
        shifted = _dot(shift, jnp.concatenate([hi, lo], axis=0))
        y = cur * w_ref[width - 1:width, :]
        for t in range(width - 1):
            y = y + shifted[t * C:(t + 1) * C] * w_ref[t:t + 1, :]
        return y * _sigmoid(y)

    qs = conv_silu(qc_ref, qp_ref, q0_ref, wq_ref)
    ks = conv_silu(kc_ref, kp_ref, k0_ref, wk_ref)
    vs = conv_silu(vc_ref, vp_ref, v0_ref, wv_ref)

    ub = [(val_s[:, sls[h]] - ks_s[h]).astype(BF16) for h in heads]
    ds_s = [_dot_tn(kdec_s[:, sls[h]], ub[h]) for h in heads]
    os_s = [qs_s[h] + _dot(qk_s[:, h * C:(h + 1) * C].astype(BF16), ub[h]) for h in heads]

    ab = ab_ref[...]
    rows = lax.broadcasted_iota(jnp.int32, (C, LANES), 0)
    valid = rows < nvalid
    xs = ab + dtb_ref[0]
    softplus = jnp.maximum(xs, 0.0) + jnp.log(1.0 + jnp.exp(-jnp.abs(xs)))
    g = jnp.where(valid, -jnp.exp(alog_ref[0]) * softplus, 0.0)
    beta = jnp.where(valid, _sigmoid(ab), 0.0)
    r = lax.broadcasted_iota(jnp.int32, (C, C), 0)
    cc = lax.broadcasted_iota(jnp.int32, (C, C), 1)
    incl = r >= cc
    strict = r > cc
    gcum = jnp.dot(jnp.where(incl, 1.0, 0.0), g, precision=_HI, preferred_element_type=F32)
    gcum_t = gcum.T

    q_ssq = _row_sums_of_squares([qs[:, sl] for sl in sls])
    k_ssq = _row_sums_of_squares([ks[:, sl] for sl in sls])
    qn = [qs[:, sls[h]] * lax.rsqrt(q_ssq[h] + NORM_EPS) * (dk ** -0.5) for h in heads]
    kn = [ks[:, sls[h]] * lax.rsqrt(k_ssq[h] + NORM_EPS) for h in heads]
    bcol = [beta[:, hg + h:hg + h + 1] for h in heads]
    gcol = [gcum[:, h:h + 1] for h in heads]
    glast = [gcum[C - 1:C, h:h + 1] for h in heads]
    eg = [jnp.exp(gcol[h]) for h in heads]
    decay = [jnp.where(incl, jnp.exp(jnp.where(incl, gcol[h] - gcum_t[h:h + 1, :], 0.0)), 0.0) for h in heads]
    kb = [kn[h] * bcol[h] for h in heads]
    knb = [kn[h].astype(BF16) for h in heads]
    a_mat = [jnp.where(strict, _dot_nt(kb[h].astype(BF16), knb[h]) * decay[h], 0.0) for h in heads]

    gain = gain_ref[...]
    for h in heads:
        s_ref[h] = s_ref[h] * eg_s[h:h + 1, :] + ds_s[h]
    for h in heads:
        o = os_s[h]
        z = z_ref[:, sls[h]]
        ms = jnp.mean(o * o, axis=-1, keepdims=True)
        o_ref[:, sls[h]] = (o * lax.rsqrt(ms + NORM_EPS) * gain * (z * _sigmoid(z))).astype(o_ref.dtype)

    t_mat = _tri_inverse_many(a_mat)
    rhs = [jnp.concatenate([vs[:, sls[h]] * bcol[h], kb[h] * eg[h]], axis=1).astype(BF16) for h in heads]
    tv = [_dot(t_mat[h].astype(BF16), rhs[h]) for h in heads]
    qk = [_dot_nt(qn[h].astype(BF16), knb[h]) * decay[h] for h in heads]
    for h in heads:
        val_s[:, sls[h]] = tv[h][:, :dk]
        kcd_s[:, sls[h]] = tv[h][:, dk:].astype(BF16)
        qdec_s[:, sls[h]] = (qn[h] * eg[h]).astype(BF16)
        kdec_s[:, sls[h]] = (kn[h] * jnp.exp(glast[h] - gcol[h])).astype(BF16)
        eg_s[h:h + 1, :] = jnp.broadcast_to(jnp.exp(glast[h]), (1, LANES))
    qk_s[...] = jnp.concatenate(qk, axis=1)

    done = (cj == ncj - 1) & (t > 0)

    @pl.when(done & (seq_j < n_new))
    def _():
        snew_ref[0] = s_ref[...]

    @pl.when(done & (seq_j >= n_new))
    def _():
        srun_ref[0] = s_ref[...]

    last_c_new = (l_valid - 1) // C
    end_new = l_valid - last_c_new * C
    is_new = i < n_p

    @pl.when(is_new & (c == last_c_new))
    def _():
        for dst, src in ((cq_ref, qc_ref), (ck_ref, kc_ref), (cv_ref, vc_ref)):
            dst[0] = src[end_new - SUBLANES:end_new, :]

    @pl.when(jnp.logical_not(is_new) & (c == cs - 1))
    def _():
        for dst, src in ((cq_ref, qc_ref), (ck_ref, kc_ref), (cv_ref, vc_ref)):
            dst[0] = src[C - SUBLANES:C, :]


def _gdn_mixer(proj, gates, conv0, s0, conv_w, a_log, dt_bias, o_gain, *, layer, prev_states, heads, dk, hg,
               n_p, cps, cs, l_valid):
    m = proj.shape[0]
    C = CHUNK
    w = heads * dk
    n_hg = heads // hg
    gw = hg * dk
    n_items = m // C
    n_layers, n_run = s0.shape[:2]
    assert (l_valid - (l_valid - 1) // CHUNK * CHUNK) % SUBLANES == 0
    n_new = n_p // cps
    width = conv_w.shape[0]
    info = dict(n_p=n_p, cps=cps, cs=cs, l_valid=l_valid)

    def seq_of(i):
        return _item_info(i, n_p, cps, cs, l_valid)[0]

    prep = lambda t: jnp.minimum(t, n_items - 1)
    scan = lambda t: jnp.maximum(t - 1, 0)
    cur = lambda off: pl.BlockSpec((C, gw), lambda g, t: (prep(t), off * n_hg + g))
    prev = lambda off: pl.BlockSpec(
        (SUBLANES, gw), lambda g, t: (jnp.maximum(prep(t) * (C // SUBLANES) - 1, 0), off * n_hg + g))
    init = lambda off: pl.BlockSpec((1, SUBLANES, gw), lambda g, t: (seq_of(prep(t)), 0, off * n_hg + g))
    wspec = lambda off: pl.BlockSpec((width, gw), lambda g, t: (0, off * n_hg + g))
    gate_vec = pl.BlockSpec((1, 1, LANES), lambda g, t: (g, 0, 0))
    state_new = pl.BlockSpec((None, 1, hg, dk, dk),
                             lambda g, t: (layer, jnp.minimum(seq_of(scan(t)), n_new - 1), g, 0, 0))
    state_run = pl.BlockSpec((None, 1, hg, dk, dk),
                             lambda g, t: (layer, jnp.maximum(seq_of(scan(t)) - n_new, 0), g, 0, 0))

    kpad = _round_up(SUBLANES + C, LANES)
    r = jnp.arange((width - 1) * C)
    src = (r % C) + SUBLANES - (width - 1) + r // C
    col = jnp.arange(2 * kpad)
    shift = ((col[None, :] == src[:, None]) | (col[None, :] == src[:, None] + kpad)).astype(BF16)

    alog_g = jnp.pad(a_log.reshape(n_hg, 1, hg).astype(F32), ((0, 0), (0, 0), (0, LANES - hg)))
    dtb_g = jnp.pad(dt_bias.reshape(n_hg, 1, hg).astype(F32), ((0, 0), (0, 0), (0, LANES - hg)))

    n_in = 19
    carried = list(prev_states) if prev_states is not None else [
        jnp.zeros((n_layers, n_new, heads, dk, dk), F32), jnp.zeros((n_layers, n_run, heads, dk, dk), F32)]

    def body(*refs):
        _gdn_kernel(*refs[:n_in], *refs[n_in + len(carried):],
                    hg=hg, dk=dk, width=width, n_items=n_items, **info)

    return pl.pallas_call(
        body,
        out_shape=(jax.ShapeDtypeStruct((m, w), BF16),
                   jax.ShapeDtypeStruct((n_layers, n_new, heads, dk, dk), F32),
                   jax.ShapeDtypeStruct((n_layers, n_run, heads, dk, dk), F32))
        + (jax.ShapeDtypeStruct((n_new + n_run, SUBLANES, w), F32),) * 3,
        input_output_aliases={n_in + i: 1 + i for i in range(len(carried))},
        grid=(n_hg, n_items + 1),
        in_specs=[cur(0), cur(1), cur(2), prev(0), prev(1), prev(2), init(0), init(1), init(2),
                  wspec(0), wspec(1), wspec(2),
                  pl.BlockSpec((C, LANES), lambda g, t: (prep(t), g)),
                  gate_vec, gate_vec,
                  pl.BlockSpec(shift.shape, lambda g, t: (0, 0)),
                  pl.BlockSpec((C, gw), lambda g, t: (scan(t), 3 * n_hg + g)),
                  pl.BlockSpec((1, dk), lambda g, t: (0, 0)),
                  state_run] + [pl.BlockSpec(memory_space=pl.ANY)] * len(carried),
        out_specs=(pl.BlockSpec((C, gw), lambda g, t: (scan(t), g)), state_new, state_run)
        + (pl.BlockSpec((1, SUBLANES, gw), lambda g, t: (seq_of(prep(t)), 0, g)),) * 3,
        scratch_shapes=[pltpu.VMEM((C, gw), F32), pltpu.VMEM((C, gw), BF16), pltpu.VMEM((C, gw), BF16),
                        pltpu.VMEM((C, gw), BF16), pltpu.VMEM((C, hg * C), F32), pltpu.VMEM((hg, LANES), F32),
                        pltpu.VMEM((hg, dk, dk), F32)],
        compiler_params=_params(("parallel", "arbitrary")),
        name="gdn_mixer",
    )(proj, proj, proj, proj, proj, proj, conv0, conv0, conv0, conv_w, conv_w, conv_w,
      gates, alog_g, dtb_g, shift, proj, o_gain.reshape(1, dk).astype(F32), s0, *carried)


def _split_half_layout(x, half):
    pad = [(0, 0)] * (x.ndim - 1) + [(0, LANES // 2 - half)]
    return jnp.concatenate([jnp.pad(x[..., :half], pad), jnp.pad(x[..., half:], pad)], axis=-1)


def _rope_tables(pos, half):
    inv_freq = 1.0 / (ROPE_THETA ** (jnp.arange(half, dtype=F32) / half))
    ang = pos[:, None] * inv_freq[None, :]
    cos, sin = jnp.cos(ang), jnp.sin(ang)
    cos_t = _split_half_layout(jnp.concatenate([cos, cos], -1), half)
    sin_t = _split_half_layout(jnp.concatenate([-sin, sin], -1), half)
    return cos_t, sin_t


def kernel(x_prompt, x_sample, state_conv, state_delta, cache_ckv, cache_krope, meta_tokens,
           a_w_in, a_conv_w, a_a_log, a_dt_bias, a_o_gain, a_w_o,
           b_w_dq, b_q_gain, b_w_uq, b_w_o,
           kv_w_dkv, kv_gain, kv_w_uk, kv_w_uv,
           mlp_w_up, mlp_w_down, ln_gain, ln_bias):
    B, seq, D = x_prompt.shape
    DB, ts, _ = x_sample.shape
    n_meta = meta_tokens.shape[0]
    depth = ln_gain.shape[0]
    n_a = a_w_in.shape[0]
    H = a_a_log.shape[1]
    dk = a_o_gain.shape[1]
    W = H * dk
    width = a_conv_w.shape[1]
    past = cache_ckv.shape[1]
    KV, MH, nope = kv_w_uk.shape
    vd = kv_w_uv.shape[2]
    rope = cache_krope.shape[2]
    half = rope // 2
    alpha = (2 * depth) ** 0.25
    scale = (nope + rope) ** -0.5
    assert ts % CHUNK == 0 and ts >= width - 1 and half < LANES // 2
    assert nope % LANES == 0 and vd % LANES == 0 and KV % LANES == 0 and dk % LANES == 0

    L = n_meta + seq
    LP = _round_up(L, LANES)
    n_prompt_rows = B * LP
    M = n_prompt_rows + DB * ts
    cps, cs = LP // CHUNK, ts // CHUNK
    n_p = B * cps
    hg = min(32, H)
    n_hg = H // hg

    pieces = []
    for b in range(B):
        pieces += [meta_tokens.astype(F32), x_prompt[b], jnp.zeros((LP - L, D), F32)]
    x = jnp.concatenate(pieces + [x_sample.reshape(DB * ts, D)], axis=0)
    xb = x.astype(BF16)

    pos = jnp.concatenate([jnp.tile(jnp.arange(LP, dtype=F32), B),
                           jnp.tile(past + jnp.arange(ts, dtype=F32), DB)])
    cos_t, sin_t = _rope_tables(pos, half)

    w_in_t = jnp.swapaxes(a_w_in, 1, 2)

    conv0 = jnp.concatenate([jnp.zeros((n_a, B, width - 1, 3 * W), F32), state_conv.astype(F32)], axis=1)
    conv0 = jnp.pad(conv0, ((0, 0), (0, 0), (SUBLANES - (width - 1), 0), (0, 0)))
    s0 = state_delta.astype(F32)

    new_conv_p, new_conv_s, states = [], [], None
    k_slabs = None
    lat = None
    for layer in range(depth):
        if layer < n_a:
            w_ab = a_w_in[layer, :, 4 * W:]
            gate_w = jnp.concatenate(
                [w_ab[:, :H].reshape(D, n_hg, hg), w_ab[:, H:].reshape(D, n_hg, hg),
                 jnp.zeros((D, n_hg, LANES - 2 * hg), w_ab.dtype)], axis=-1).reshape(D, n_hg * LANES)
            proj = _matmul(xb, w_in_t, F32, layer=layer, n_use=4 * W, b_transposed=True)
            gates = _matmul(xb, gate_w.astype(BF16), F32, bm=328)
            o, s_new, s_run, *conv_tail = _gdn_mixer(
                proj, gates, conv0[layer], s0, a_conv_w[layer].astype(F32),
                a_a_log[layer], a_dt_bias[layer], a_o_gain[layer],
                layer=layer, prev_states=states, heads=H, dk=dk, hg=hg,
                n_p=n_p, cps=cps, cs=cs, l_valid=L)
            states = (s_new, s_run)
            h = _matmul(o, a_w_o, BF16, layer=layer)
            conv_new = jnp.concatenate(conv_tail, axis=-1)[:, SUBLANES - (width - 1):]
            new_conv_p.append(conv_new[:B])
            new_conv_s.append(conv_new[B:])
        else:
            j = layer - n_a
            ql = _matmul(xb, b_w_dq[j].astype(BF16), BF16, epilogue="rms", gain=b_q_gain[j])
            w_uq = b_w_uq[j].reshape(-1, MH, nope + rope)
            w_uq = jnp.concatenate([w_uq[..., :nope], _split_half_layout(w_uq[..., nope:], half)], axis=-1)
            q = _matmul(ql, w_uq.reshape(-1, MH * (nope + LANES)).astype(BF16), BF16, bn=2048, epilogue="rope",
                        rope=(cos_t, sin_t, nope // LANES + 1, nope // LANES, scale * math.log2(math.e)))
            kp, vp, ks, ls_t, vs, lkp, lks = k_slabs
            o = _attention_causal(q, kp, vp, nb=B, lq=LP, lk_pad=lkp, heads=MH, nope=nope,
                                  vd=vd, bq=_pick_block(LP, 1408, LANES), bk=512, n_meta=n_meta)
            o = _attention_full(q, ks, ls_t, vs, o, nb=DB, lq=ts, lk_pad=lks, row0=n_prompt_rows,
                                heads=MH, hb=min(8, MH), nope=nope, vd=vd, kv=KV, n_keys=past + ts)
            h = _matmul(o, b_w_o, BF16, layer=j)

        x, xb = _deepnorm(x, h, ln_gain[layer, 0], ln_bias[layer, 0], alpha)
        hid = _matmul(xb, mlp_w_up, BF16, layer=layer, epilogue="relu2")
        h = _matmul(hid, mlp_w_down, BF16, layer=layer, bn=1024, bk=2048)
        x, xb = _deepnorm(x, h, ln_gain[layer, 1], ln_bias[layer, 1], alpha)

        if layer == n_a - 1:
            w_dkv = jnp.concatenate([kv_w_dkv[:, :KV], _split_half_layout(kv_w_dkv[:, KV:], half)], axis=1)
            lat = _kv_post(_matmul(xb, w_dkv.astype(BF16), F32, bm=328, bn=640), kv_gain.astype(F32),
                           cos_t, sin_t, KV)
            lane_ids = jnp.arange(LANES)
            eye = jnp.where((lane_ids[:, None] == lane_ids[None, :]) & (lane_ids[:, None] != half), 1.0, 0.0)
            ones_col = jnp.where((lane_ids[:, None] == half) & (lane_ids[None, :] == 0), 1.0, 0.0)
            w_k = jnp.concatenate(
                [jnp.concatenate([kv_w_uk.astype(F32), jnp.zeros((KV, MH, LANES), F32)], axis=-1),
                 jnp.concatenate([jnp.zeros((LANES, MH, nope), F32),
                                  jnp.broadcast_to(eye[:, None, :], (LANES, MH, LANES))], axis=-1)],
                axis=0).reshape(KV + LANES, MH * (nope + LANES)).astype(BF16)
            w_v = jnp.concatenate(
                [jnp.concatenate([kv_w_uv.astype(F32), jnp.zeros((KV, MH, LANES), F32)], axis=-1),
                 jnp.concatenate([jnp.zeros((LANES, MH, vd), F32),
                                  jnp.broadcast_to(ones_col[:, None, :], (LANES, MH, LANES))], axis=-1)],
                axis=0).reshape(KV + LANES, MH * (vd + LANES)).astype(BF16)
            one_lane = jnp.where(jnp.arange(KV + LANES) == KV + half, 1.0, 0.0)
            lkp = _round_up(L, 512)
            lat_p = lat[:n_prompt_rows].reshape(B, LP, KV + LANES)
            if lkp >= LP:
                lat_p = jnp.pad(lat_p, ((0, 0), (0, lkp - LP), (0, 0)))
            else:
                lat_p = lat_p[:, :lkp]
            lat_p = (lat_p.reshape(B * lkp, KV + LANES) + one_lane).astype(BF16)
            lks = _round_up(past + ts, LANES)
            cache = jnp.concatenate([cache_ckv.astype(F32), _split_half_layout(cache_krope.astype(F32), half)],
                                    axis=-1)
            lat_s = jnp.concatenate([cache, lat[n_prompt_rows:].reshape(DB, ts, KV + LANES),
                                     jnp.zeros((DB, lks - past - ts, KV + LANES), F32)], axis=1)
            lat_s = (lat_s.reshape(DB * lks, KV + LANES) + one_lane).astype(BF16)
            w_kn_t = jnp.transpose(kv_w_uk, (1, 2, 0)).reshape(MH * nope, KV).astype(BF16)
            lat_s_t = lat_s.T
            big = dict(bm=2048, bn=2048)
            k_slabs = (_matmul(w_k.T, lat_p.T, BF16, bm=2048, bn=1536), _matmul(lat_p, w_v, BF16, bm=1536, bn=2048),
                       _matmul(w_kn_t, lat_s_t, BF16, **big), lat_s_t,
                       _matmul(lat_s, kv_w_uv.reshape(KV, MH * vd).astype(BF16), BF16, **big),
                       lkp, lks)

    def unsplit(r):
        return jnp.concatenate([r[..., :half], r[..., LANES // 2:LANES // 2 + half]], axis=-1)

    lat_p = lat[:n_prompt_rows].reshape(B, LP, KV + LANES)[:, :L]
    lat_s = lat[n_prompt_rows:].reshape(DB, ts, KV + LANES)
    y_prompt = jnp.stack([x[b * LP + n_meta:b * LP + L] for b in range(B)])
    y_sample = x[n_prompt_rows:].reshape(DB, ts, D)
    return (y_prompt, y_sample,
            jnp.stack(new_conv_p), states[0],
            lat_p[..., :KV], unsplit(lat_p[..., KV:]),
            jnp.stack(new_conv_s), states[1],
            lat_s[..., :KV], unsplit(lat_s[..., KV:]))
```

```python
import functools
import math

import jax
import jax.numpy as jnp
from jax import lax
from jax.experimental import pallas as pl
from jax.experimental.pallas import tpu as pltpu

CHUNK = 64
ROPE_THETA = 10000.0
NORM_EPS = 1e-6
NEG_INF = -1e30
LANES = 128
SUBLANES = 8
VMEM_LIMIT_BYTES = 56 * 1024 * 1024
_SPLIT_LEVELS = 2
_CHUNK_SHIFT = CHUNK.bit_length() - 1
assert 1 << _CHUNK_SHIFT == CHUNK

F32 = jnp.float32
BF16 = jnp.bfloat16
_HI = lax.Precision.HIGHEST


def _round_up(x, m):
    return -(-x // m) * m


def _pick_block(dim, target, align):
    best = None
    for d in range(align, min(dim, target) + 1, align):
        if dim % d == 0:
            best = d
    return best if best is not None else dim


def _dot(a, b):
    return jnp.dot(a, b, preferred_element_type=F32)


def _dot_nt(a, b):
    return lax.dot_general(a, b, (((1,), (1,)), ((), ())), preferred_element_type=F32)


def _dot_tn(a, b):
    return lax.dot_general(a, b, (((0,), (0,)), ((), ())), preferred_element_type=F32)


def _sigmoid(x):
    return 1.0 / (1.0 + jnp.exp(-x))


def _params(sem):
    return pltpu.CompilerParams(dimension_semantics=sem, vmem_limit_bytes=VMEM_LIMIT_BYTES)


def _mm_kernel(*refs, nk, epilogue, b_transposed, rope=None):
    n_extra = {"rms": 1, "rope": 2}.get(epilogue, 0)
    a_ref, b_ref = refs[:2]
    extra = refs[2:2 + n_extra]
    o_ref = refs[2 + n_extra]
    rest = refs[3 + n_extra:]

    def finish(acc):
        if epilogue == "relu2":
            r = jnp.maximum(acc, 0.0)
            acc = r * r
        elif epilogue == "rms":
            ms = jnp.mean(acc * acc, axis=-1, keepdims=True)
            acc = acc * lax.rsqrt(ms + NORM_EPS) * extra[0][...]
        elif epilogue == "rope":
            chunks_per_head, rope_chunk, scale = rope
            cos, sin = extra[0][...], extra[1][...]
            cols = []
            for c in range(acc.shape[1] // LANES):
                x = acc[:, c * LANES:(c + 1) * LANES]
                if c % chunks_per_head == rope_chunk:
                    x = x * cos + pltpu.roll(x, LANES // 2, axis=1) * sin
                cols.append(x * scale)
            acc = jnp.concatenate(cols, axis=1)
        o_ref[...] = acc.astype(o_ref.dtype)

    def product():
        b = b_ref[...].astype(BF16)
        return _dot_nt(a_ref[...], b) if b_transposed else _dot(a_ref[...], b)

    if nk == 1:
        finish(product())
    else:
        acc_ref = rest[0]
        k = pl.program_id(2)

        @pl.when(k == 0)
        def _():
            acc_ref[...] = jnp.zeros_like(acc_ref)

        acc_ref[...] += product()

        @pl.when(k == nk - 1)
        def _():
            finish(acc_ref[...])


def _matmul(a, b, out_dtype, *, layer=None, n_use=None, b_transposed=False, epilogue="none", gain=None,
            rope=None, bm=1312, bn=512, bk=4096):
    m = a.shape[0]
    kdim = min(a.shape[1], b.shape[-1 if b_transposed else -2])
    n = n_use if n_use is not None else b.shape[-2 if b_transposed else -1]
    bm = _pick_block(m, bm, 16)
    bn = n if epilogue == "rms" else _pick_block(n, bn, LANES)
    bk = _pick_block(kdim, bk, LANES)
    nk = kdim // bk
    a_mode = dict(pipeline_mode=pl.Buffered(1)) if nk == 1 and n // bn >= 4 else {}
    b_blk = (bn, bk) if b_transposed else (bk, bn)
    b_idx = (lambda k, j: (j, k)) if b_transposed else (lambda k, j: (k, j))
    if b.ndim == 3:
        b_spec = pl.BlockSpec((None,) + b_blk, lambda i, j, k: (layer,) + b_idx(k, j))
    else:
        b_spec = pl.BlockSpec(b_blk, lambda i, j, k: b_idx(k, j))
    in_specs = [pl.BlockSpec((bm, bk), lambda i, j, k: (i, k), **a_mode), b_spec]
    args = [a, b]
    if epilogue == "rms":
        in_specs.append(pl.BlockSpec((1, bn), lambda i, j, k: (0, j)))
        args.append(gain.reshape(1, n).astype(F32))
    rope_static = None
    if epilogue == "rope":
        cos_t, sin_t, *rope_static = rope
        in_specs += [pl.BlockSpec((bm, LANES), lambda i, j, k: (i, 0))] * 2
        args += [cos_t, sin_t]
        assert bn % (rope_static[0] * LANES) == 0
    scratch = [pltpu.VMEM((bm, bn), F32)] if nk > 1 else []
    return pl.pallas_call(
        functools.partial(_mm_kernel, nk=nk, epilogue=epilogue, b_transposed=b_transposed,
                          rope=tuple(rope_static) if rope_static else None),
        out_shape=jax.ShapeDtypeStruct((m, n), out_dtype),
        grid=(m // bm, n // bn, nk),
        in_specs=in_specs,
        out_specs=pl.BlockSpec((bm, bn), lambda i, j, k: (i, j)),
        scratch_shapes=scratch,
        compiler_params=_params(("parallel", "parallel", "arbitrary")),
        name="matmul_" + epilogue,
    )(*args)


def _ln_kernel(x_ref, h_ref, g_ref, b_ref, of_ref, ob_ref, *, alpha):
    y = alpha * x_ref[...] + h_ref[...].astype(F32)
    mu = jnp.mean(y, axis=-1, keepdims=True)
    d = y - mu
    var = jnp.mean(d * d, axis=-1, keepdims=True)
    out = d * lax.rsqrt(var + NORM_EPS) * g_ref[...] + b_ref[...]
    of_ref[...] = out
    ob_ref[...] = out.astype(BF16)


def _deepnorm(x, h, gain, bias, alpha):
    m, d = x.shape
    br = _pick_block(m, 256, 16)
    row = pl.BlockSpec((br, d), lambda i: (i, 0))
    vec = pl.BlockSpec((1, d), lambda i: (0, 0))
    return pl.pallas_call(
        functools.partial(_ln_kernel, alpha=alpha),
        out_shape=(jax.ShapeDtypeStruct((m, d), F32), jax.ShapeDtypeStruct((m, d), BF16)),
        grid=(m // br,),
        in_specs=[row, row, vec, vec],
        out_specs=(row, row),
        compiler_params=_params(("parallel",)),
        name="deepnorm",
    )(x, h, gain.reshape(1, d), bias.reshape(1, d))


def _kv_post_kernel(lat_ref, g_ref, cos_ref, sin_ref, o_ref, *, kv):
    c = lat_ref[:, :kv]
    ms = jnp.mean(c * c, axis=-1, keepdims=True)
    o_ref[:, :kv] = c * lax.rsqrt(ms + NORM_EPS) * g_ref[...]
    r = lat_ref[:, kv:]
    o_ref[:, kv:] = r * cos_ref[...] + pltpu.roll(r, LANES // 2, axis=1) * sin_ref[...]


def _kv_post(lat, gain, cos_t, sin_t, kv):
    m, w = lat.shape
    br = _pick_block(m, 256, 8)
    return pl.pallas_call(
        functools.partial(_kv_post_kernel, kv=kv),
        out_shape=jax.ShapeDtypeStruct((m, w), F32),
        grid=(m // br,),
        in_specs=[pl.BlockSpec((br, w), lambda i: (i, 0)),
                  pl.BlockSpec((1, kv), lambda i: (0, 0)),
                  pl.BlockSpec((br, LANES), lambda i: (i, 0)),
                  pl.BlockSpec((br, LANES), lambda i: (i, 0))],
        out_specs=pl.BlockSpec((br, w), lambda i: (i, 0)),
        compiler_params=_params(("parallel",)),
        name="kv_post",
    )(lat, gain.reshape(1, kv), cos_t, sin_t)


def _attn_causal_kernel(q_ref, kt_ref, v_ref, o_ref,
                        s_scr, p_scr, m_scr, a_scr, acc_scr, vis_scr, *, bq, bk, rt, n_groups, vd, n_meta, n_keys,
                        lk_pad):
    qi = pl.program_id(2)
    tiles = [slice(t * rt, (t + 1) * rt) for t in range(bq // rt)]
    lane_chunks = [slice(c * LANES, (c + 1) * LANES) for c in range(bk // LANES)]

    def visible_end(r):
        chunk_id = lax.shift_right_arithmetic(r - n_meta, _CHUNK_SHIFT) + 1
        return jnp.minimum(jnp.where(r < n_meta, n_meta, chunk_id * CHUNK + n_meta), n_keys)

    n_full = visible_end(qi * bq) // bk
    nkb = (jnp.minimum(visible_end(qi * bq + bq - 1), lk_pad) + bk - 1) // bk

    for t, rows in enumerate(tiles):
        vis_scr[rows, :] = visible_end(qi * bq + t * rt + lax.broadcasted_iota(jnp.int32, (rt, LANES), 0))
    m_scr[...] = jnp.full(m_scr.shape, NEG_INF, F32)
    acc_scr[...] = jnp.zeros(acc_scr.shape, F32)

    gsz = bq // n_groups
    groups = [slice(g * gsz, (g + 1) * gsz) for g in range(n_groups)]
    tiles_per_group = gsz // rt
    last_block = lk_pad // bk - 1
    last_g = n_groups - 1

    def block_start(j):
        return pl.multiple_of(j * bk, bk)

    def scores(g, j):
        s_scr[groups[g], :] = _dot(q_ref[groups[g], :], kt_ref[:, pl.ds(block_start(j), bk)])

    def values(g, j):
        rows = groups[g]
        a = a_scr[rows, :]
        acc_scr[rows, :] = (jnp.concatenate([a] * (acc_scr.shape[1] // LANES), axis=1) * acc_scr[rows, :]
                            + _dot(p_scr[rows, :], v_ref[pl.ds(block_start(j), bk), :]))

    p_scr[groups[last_g], :] = jnp.zeros((gsz, bk), BF16)
    a_scr[groups[last_g], :] = jnp.ones((gsz, LANES), F32)
    scores(0, 0)

    def step(j, carry, masked):
        start = block_start(j)

        def softmax(g):
            for t in range(g * tiles_per_group, (g + 1) * tiles_per_group):
                rows = tiles[t]
                s = s_scr[rows, :]
                if masked:
                    vis = vis_scr[rows, :]
                    lane = start + lax.broadcasted_iota(jnp.int32, (1, LANES), 1)
                    s = jnp.concatenate([jnp.where(lane + c * LANES < vis, s[:, lc], NEG_INF)
                                         for c, lc in enumerate(lane_chunks)], axis=1)
                m_old = m_scr[rows, :]
                m_new = jnp.maximum(m_old, jnp.max(s, axis=-1, keepdims=True))
                for lc in lane_chunks:
                    p_scr[rows, lc] = jnp.exp2(s[:, lc] - m_new).astype(BF16)
                a_scr[rows, :] = jnp.exp2(m_old - m_new)
                m_scr[rows, :] = m_new

        values(last_g, jnp.maximum(j - 1, 0))
        for g in range(n_groups):
            if g < last_g:
                scores(g + 1, j)
            else:
                scores(0, jnp.minimum(j + 1, last_block))
            softmax(g)
            if g < last_g:
                values(g, j)
        return carry

    lax.fori_loop(0, n_full, functools.partial(step, masked=False), 0)
    lax.fori_loop(n_full, nkb, functools.partial(step, masked=True), 0)
    values(last_g, nkb - 1)
    for rows in tiles:
        o_ref[rows, :] = (acc_scr[rows, :vd] / acc_scr[rows, vd:vd + 1]).astype(o_ref.dtype)


def _attn_full_kernel(q_ref, knt_ref, krt_ref, v_ref, o_ref, *, hb, nope, vd, n_keys):
    dqk = nope + LANES
    lq = q_ref.shape[0]
    heads = range(hb)
    kpos = lax.broadcasted_iota(jnp.int32, (1, knt_ref.shape[1]), 1)
    qs = [q_ref[:, h * dqk:(h + 1) * dqk] for h in heads]
    s_rope = _dot(jnp.concatenate([q[:, nope:] for q in qs], axis=0), krt_ref[...])
    ss = [jnp.where(kpos < n_keys,
                    _dot(qs[h][:, :nope], knt_ref[h * nope:(h + 1) * nope, :]) + s_rope[h * lq:(h + 1) * lq],
                    NEG_INF) for h in heads]
    es = [jnp.exp2(s - jnp.max(s, axis=-1, keepdims=True)) for s in ss]
    ls = [jnp.sum(e, axis=-1, keepdims=True) for e in es]
    pv = [_dot(es[h].astype(BF16), v_ref[:, h * vd:(h + 1) * vd]) for h in heads]
    for h in heads:
        o_ref[:, h * vd:(h + 1) * vd] = (pv[h] / ls[h]).astype(o_ref.dtype)


def _attention_causal(q, kt, vf, *, nb, lq, lk_pad, heads, nope, vd, bq, bk, n_meta, n_keys):
    dqk = nope + LANES
    nq = lq // bq
    rt = 32 if bq % 32 == 0 else bq
    n_groups = 4 if bq % (4 * rt) == 0 else 2
    assert bq % (n_groups * rt) == 0
    def body(q_ref, kt_ref, v_ref, o_all_ref, o_ref, *scratch):
        _attn_causal_kernel(q_ref, kt_ref, v_ref, o_ref, *scratch, bq=bq, bk=bk, rt=rt,
                            n_groups=n_groups, vd=vd, n_meta=n_meta, n_keys=n_keys, lk_pad=lk_pad)

    o_all = jnp.zeros((q.shape[0], heads * vd), BF16)
    return pl.pallas_call(
        body,
        out_shape=jax.ShapeDtypeStruct(o_all.shape, o_all.dtype),
        grid=(nb, heads, nq),
        in_specs=[pl.BlockSpec((bq, dqk), lambda b, h, i: (b * nq + i, h)),
                  pl.BlockSpec((dqk, lk_pad), lambda b, h, i: (h, b)),
                  pl.BlockSpec((lk_pad, vd + LANES), lambda b, h, i: (b, h)),
                  pl.BlockSpec(memory_space=pl.ANY)],
        out_specs=pl.BlockSpec((bq, vd), lambda b, h, i: (b * nq + i, h)),
        input_output_aliases={3: 0},
        scratch_shapes=[pltpu.VMEM((bq, bk), F32), pltpu.VMEM((bq, bk), BF16),
                        pltpu.VMEM((bq, LANES), F32), pltpu.VMEM((bq, LANES), F32),
                        pltpu.VMEM((bq, vd + LANES), F32), pltpu.VMEM((bq, LANES), jnp.int32)],
        compiler_params=_params(("parallel", "parallel", "arbitrary")),
        name="mla_attention_causal",
    )(q, kt, vf, o_all)


def _attention_full(q, knt, lat_t, vf, o_all, *, nb, lq, lk_pad, row0, heads, hb, nope, vd, kv, n_keys):
    dqk = nope + LANES
    blk0 = row0 // lq

    def body(q_ref, knt_ref, krt_ref, v_ref, o_all_ref, o_ref):
        _attn_full_kernel(q_ref, knt_ref, krt_ref, v_ref, o_ref, hb=hb, nope=nope, vd=vd, n_keys=n_keys)

    return pl.pallas_call(
        body,
        out_shape=jax.ShapeDtypeStruct(o_all.shape, o_all.dtype),
        grid=(nb, heads // hb),
        in_specs=[pl.BlockSpec((lq, hb * dqk), lambda b, g: (blk0 + b, g)),
                  pl.BlockSpec((hb * nope, lk_pad), lambda b, g: (g, b)),
                  pl.BlockSpec((LANES, lk_pad), lambda b, g: (kv // LANES, b)),
                  pl.BlockSpec((lk_pad, hb * vd), lambda b, g: (b, g)),
                  pl.BlockSpec(memory_space=pl.ANY)],
        out_specs=pl.BlockSpec((lq, hb * vd), lambda b, g: (blk0 + b, g)),
        input_output_aliases={4: 0},
        compiler_params=_params(("parallel", "parallel")),
        name="mla_attention_full",
    )(q, knt, lat_t, vf, o_all)


def _item_info(i, n_p, cps, cs, l_valid):
    is_p = i < n_p
    j = i - n_p
    seq = jnp.where(is_p, i // cps, n_p // cps + j // cs)
    c = jnp.where(is_p, i % cps, j % cs)
    nc = jnp.where(is_p, cps, cs)
    nvalid = jnp.where(is_p, jnp.clip(l_valid - c * CHUNK, 0, CHUNK), CHUNK)
    return seq, c, nc, nvalid


def _row_sums_of_squares(xs):
    rows, n = xs[0].shape
    parts = []
    for x in xs:
        sq = x * x
        hi = sq.astype(BF16)
        parts.append(jnp.concatenate([hi, (sq - hi.astype(F32)).astype(BF16)], axis=1))
    tot = _dot(jnp.concatenate(parts, axis=0), jnp.ones((2 * n, n), BF16))
    return [tot[i * rows:(i + 1) * rows] for i in range(len(xs))]


def _tri_inverse_many(mats):
    n = mats[0].shape[0]
    r = lax.broadcasted_iota(jnp.int32, (n, n), 0)
    c = lax.broadcasted_iota(jnp.int32, (n, n), 1)
    eye = jnp.where(r == c, 1.0, 0.0)
    zero = jnp.zeros((n, n), F32)

    def split(x):
        hi = x.astype(BF16).astype(F32)
        return hi, x - hi

    def lhs_of(x):
        hi, lo = split(x)
        return jnp.concatenate([hi, lo, hi, zero], axis=1).astype(BF16)

    def rhs_of(x):
        hi, lo = split(x)
        return jnp.concatenate([hi, hi, lo, zero], axis=0).astype(BF16)

    ps = [eye - a for a in mats]
    aks = list(mats)
    rhs = [rhs_of(a) for a in mats]
    power, level = 2, 0
    while power < n:
        if level < _SPLIT_LEVELS:
            aks = [_dot(lhs_of(ak), rk) for ak, rk in zip(aks, rhs)]
            rhs = [rhs_of(ak) for ak in aks]
            ps = [p + _dot(lhs_of(p), rk) for p, rk in zip(ps, rhs)]
        else:
            akb = [ak.astype(BF16) for ak in aks]
            aks = [_dot(b, b) for b in akb]
            ps = [p + _dot(p.astype(BF16), ak.astype(BF16)) for p, ak in zip(ps, aks)]
        power *= 2
        level += 1
    return ps


def _gdn_kernel(qc_ref, kc_ref, vc_ref, qp_ref, kp_ref, vp_ref, q0_ref, k0_ref, v0_ref,
                wq_ref, wk_ref, wv_ref, ab_ref, alog_ref, dtb_ref, shift_ref, z_ref, gain_ref, s0_ref,
                o_ref, snew_ref, srun_ref, cq_ref, ck_ref, cv_ref,
                val_s, kcd_s, qdec_s, kdec_s, qk_s, eg_s, s_ref,
                *, hg, dk, n_items, n_p, cps, cs, l_valid, width):
    C = CHUNK
    t = pl.program_id(1)
    i = jnp.minimum(t, n_items - 1)
    j = jnp.maximum(t - 1, 0)
    _, c, _, nvalid = _item_info(i, n_p, cps, cs, l_valid)
    _, cj, ncj, _ = _item_info(j, n_p, cps, cs, l_valid)
    first = c == 0
    heads = range(hg)
    sls = [slice(h * dk, (h + 1) * dk) for h in heads]

    @pl.when(t == 0)
    def _():
        for ref in (val_s, kcd_s, qdec_s, kdec_s, qk_s, eg_s):
            ref[...] = jnp.zeros(ref.shape, ref.dtype)

    n_new = n_p // cps
    seq_j = _item_info(j, n_p, cps, cs, l_valid)[0]

    @pl.when(cj == 0)
    def _():
        s_ref[...] = jnp.where(seq_j < n_new, 0.0, s0_ref[0])

    sb = [s_ref[h].astype(BF16) for h in heads]
    ks_s = [_dot(kcd_s[:, sls[h]], sb[h]) for h in heads]
    qs_s = [_dot(qdec_s[:, sls[h]], sb[h]) for h in heads]

    halo = SUBLANES
    kpad = shift_ref.shape[1] // 2
    shift = shift_ref[...]

    def conv_silu(cur_ref, prev_ref, init_ref, w_ref):
        cur = cur_ref[...]
        rows = jnp.concatenate([jnp.where(first, init_ref[0], prev_ref[...]), cur,
                                jnp.zeros((kpad - halo - C, cur.shape[1]), F32)], axis=0)
        hi = rows.astype(BF16)
        lo = (rows - hi.astype(F32)).astype(BF16)
        shifted = _dot(shift, jnp.concatenate([hi, lo], axis=0))
        y = cur * w_ref[width - 1:width, :]
        for t in range(width - 1):
            y = y + shifted[t * C:(t + 1) * C] * w_ref[t:t + 1, :]
        return y * _sigmoid(y)

    qs = conv_silu(qc_ref, qp_ref, q0_ref, wq_ref)
    ks = conv_silu(kc_ref, kp_ref, k0_ref, wk_ref)
    vs = conv_silu(vc_ref, vp_ref, v0_ref, wv_ref)

    ub = [(val_s[:, sls[h]] - ks_s[h]).astype(BF16) for h in heads]
    ds_s = [_dot_tn(kdec_s[:, sls[h]], ub[h]) for h in heads]
    os_s = [qs_s[h] + _dot(qk_s[:, h * C:(h + 1) * C].astype(BF16), ub[h]) for h in heads]

    ab = ab_ref[...]
    rows = lax.broadcasted_iota(jnp.int32, (C, LANES), 0)
    valid = rows < nvalid
    xs = ab + dtb_ref[0]
    softplus = jnp.maximum(xs, 0.0) + jnp.log(1.0 + jnp.exp(-jnp.abs(xs)))
    g = jnp.where(valid, -jnp.exp(alog_ref[0]) * softplus, 0.0)
    beta = jnp.where(valid, _sigmoid(ab), 0.0)
    r = lax.broadcasted_iota(jnp.int32, (C, C), 0)
    cc = lax.broadcasted_iota(jnp.int32, (C, C), 1)
    incl = r >= cc
    strict = r > cc
    gcum = jnp.dot(jnp.where(incl, 1.0, 0.0), g, precision=_HI, preferred_element_type=F32)
    gcum_t = gcum.T

    q_ssq = _row_sums_of_squares([qs[:, sl] for sl in sls])
    k_ssq = _row_sums_of_squares([ks[:, sl] for sl in sls])
    qn = [qs[:, sls[h]] * lax.rsqrt(q_ssq[h] + NORM_EPS) * (dk ** -0.5) for h in heads]
    kn = [ks[:, sls[h]] * lax.rsqrt(k_ssq[h] + NORM_EPS) for h in heads]
    bcol = [beta[:, hg + h:hg + h + 1] for h in heads]
    gcol = [gcum[:, h:h + 1] for h in heads]
    glast = [gcum[C - 1:C, h:h + 1] for h in heads]
    eg = [jnp.exp(gcol[h]) for h in heads]
    decay = [jnp.where(incl, jnp.exp(jnp.where(incl, gcol[h] - gcum_t[h:h + 1, :], 0.0)), 0.0) for h in heads]
    kb = [kn[h] * bcol[h] for h in heads]
    knb = [kn[h].astype(BF16) for h in heads]
    a_mat = [jnp.where(strict, _dot_nt(kb[h].astype(BF16), knb[h]) * decay[h], 0.0) for h in heads]

    gain = gain_ref[...]
    for h in heads:
        s_ref[h] = s_ref[h] * eg_s[h:h + 1, :] + ds_s[h]
    for h in heads:
        o = os_s[h]
        z = z_ref[:, sls[h]]
        ms = jnp.mean(o * o, axis=-1, keepdims=True)
        o_ref[:, sls[h]] = (o * lax.rsqrt(ms + NORM_EPS) * gain * (z * _sigmoid(z))).astype(o_ref.dtype)

    t_mat = _tri_inverse_many(a_mat)
    rhs = [jnp.concatenate([vs[:, sls[h]] * bcol[h], kb[h] * eg[h]], axis=1).astype(BF16) for h in heads]
    tv = [_dot(t_mat[h].astype(BF16), rhs[h]) for h in heads]
    qk = [_dot_nt(qn[h].astype(BF16), knb[h]) * decay[h] for h in heads]
    for h in heads:
        val_s[:, sls[h]] = tv[h][:, :dk]
        kcd_s[:, sls[h]] = tv[h][:, dk:].astype(BF16)
        qdec_s[:, sls[h]] = (qn[h] * eg[h]).astype(BF16)
        kdec_s[:, sls[h]] = (kn[h] * jnp.exp(glast[h] - gcol[h])).astype(BF16)
        eg_s[h:h + 1, :] = jnp.broadcast_to(jnp.exp(glast[h]), (1, LANES))
    qk_s[...] = jnp.concatenate(qk, axis=1)

    done = (cj == ncj - 1) & (t > 0)

    @pl.when(done & (seq_j < n_new))
    def _():
        snew_ref[0] = s_ref[...]

    @pl.when(done & (seq_j >= n_new))
    def _():
        srun_ref[0] = s_ref[...]

    last_c_new = (l_valid - 1) // C
    end_new = l_valid - last_c_new * C
    is_new = i < n_p

    @pl.when(is_new & (c == last_c_new))
    def _():
        for dst, src in ((cq_ref, qc_ref), (ck_ref, kc_ref), (cv_ref, vc_ref)):
            dst[0] = src[end_new - SUBLANES:end_new, :]

    @pl.when(jnp.logical_not(is_new) & (c == cs - 1))
    def _():
        for dst, src in ((cq_ref, qc_ref), (ck_ref, kc_ref), (cv_ref, vc_ref)):
            dst[0] = src[C - SUBLANES:C, :]


def _gdn_mixer(proj, gates, conv0, s0, conv_w, a_log, dt_bias, o_gain, *, layer, prev_states, heads, dk, hg,
               n_p, cps, cs, l_valid):
    m = proj.shape[0]
    C = CHUNK
    w = heads * dk
    n_hg = heads // hg
    gw = hg * dk
    n_items = m // C
    n_layers, n_run = s0.shape[:2]
    assert (l_valid - (l_valid - 1) // CHUNK * CHUNK) % SUBLANES == 0
    n_new = n_p // cps
    width = conv_w.shape[0]
    info = dict(n_p=n_p, cps=cps, cs=cs, l_valid=l_valid)

    def seq_of(i):
        return _item_info(i, n_p, cps, cs, l_valid)[0]

    prep = lambda t: jnp.minimum(t, n_items - 1)
    scan = lambda t: jnp.maximum(t - 1, 0)
    cur = lambda off: pl.BlockSpec((C, gw), lambda g, t: (prep(t), off * n_hg + g))
    prev = lambda off: pl.BlockSpec(
        (SUBLANES, gw), lambda g, t: (jnp.maximum(prep(t) * (C // SUBLANES) - 1, 0), off * n_hg + g))
    init = lambda off: pl.BlockSpec((1, SUBLANES, gw), lambda g, t: (seq_of(prep(t)), 0, off * n_hg + g))
    wspec = lambda off: pl.BlockSpec((width, gw), lambda g, t: (0, off * n_hg + g))
    gate_vec = pl.BlockSpec((1, 1, LANES), lambda g, t: (g, 0, 0))
    state_new = pl.BlockSpec((None, 1, hg, dk, dk),
                             lambda g, t: (layer, jnp.minimum(seq_of(scan(t)), n_new - 1), g, 0, 0))
    state_run = pl.BlockSpec((None, 1, hg, dk, dk),
                             lambda g, t: (layer, jnp.maximum(seq_of(scan(t)) - n_new, 0), g, 0, 0))

    kpad = _round_up(SUBLANES + C, LANES)
    r = jnp.arange((width - 1) * C)
    src = (r % C) + SUBLANES - (width - 1) + r // C
    col = jnp.arange(2 * kpad)
    shift = ((col[None, :] == src[:, None]) | (col[None, :] == src[:, None] + kpad)).astype(BF16)

    alog_g = jnp.pad(a_log.reshape(n_hg, 1, hg).astype(F32), ((0, 0), (0, 0), (0, LANES - hg)))
    dtb_g = jnp.pad(dt_bias.reshape(n_hg, 1, hg).astype(F32), ((0, 0), (0, 0), (0, LANES - hg)))

    n_in = 19
    carried = list(prev_states) if prev_states is not None else [
        jnp.zeros((n_layers, n_new, heads, dk, dk), F32), jnp.zeros((n_layers, n_run, heads, dk, dk), F32)]

    def body(*refs):
        _gdn_kernel(*refs[:n_in], *refs[n_in + len(carried):],
                    hg=hg, dk=dk, width=width, n_items=n_items, **info)

    return pl.pallas_call(
        body,
        out_shape=(jax.ShapeDtypeStruct((m, w), BF16),
                   jax.ShapeDtypeStruct((n_layers, n_new, heads, dk, dk), F32),
                   jax.ShapeDtypeStruct((n_layers, n_run, heads, dk, dk), F32))
        + (jax.ShapeDtypeStruct((n_new + n_run, SUBLANES, w), F32),) * 3,
        input_output_aliases={n_in + i: 1 + i for i in range(len(carried))},
        grid=(n_hg, n_items + 1),
        in_specs=[cur(0), cur(1), cur(2), prev(0), prev(1), prev(2), init(0), init(1), init(2),
                  wspec(0), wspec(1), wspec(2),
                  pl.BlockSpec((C, LANES), lambda g, t: (prep(t), g)),
                  gate_vec, gate_vec,
                  pl.BlockSpec(shift.shape, lambda g, t: (0, 0)),
                  pl.BlockSpec((C, gw), lambda g, t: (scan(t), 3 * n_hg + g)),
                  pl.BlockSpec((1, dk), lambda g, t: (0, 0)),
                  state_run] + [pl.BlockSpec(memory_space=pl.ANY)] * len(carried),
        out_specs=(pl.BlockSpec((C, gw), lambda g, t: (scan(t), g)), state_new, state_run)
        + (pl.BlockSpec((1, SUBLANES, gw), lambda g, t: (seq_of(prep(t)), 0, g)),) * 3,
        scratch_shapes=[pltpu.VMEM((C, gw), F32), pltpu.VMEM((C, gw), BF16), pltpu.VMEM((C, gw), BF16),
                        pltpu.VMEM((C, gw), BF16), pltpu.VMEM((C, hg * C), F32), pltpu.VMEM((hg, LANES), F32),
                        pltpu.VMEM((hg, dk, dk), F32)],
        compiler_params=_params(("parallel", "arbitrary")),
        name="gdn_mixer",
    )(proj, proj, proj, proj, proj, proj, conv0, conv0, conv0, conv_w, conv_w, conv_w,
      gates, alog_g, dtb_g, shift, proj, o_gain.reshape(1, dk).astype(F32), s0, *carried)


def _split_half_layout(x, half):
    pad = [(0, 0)] * (x.ndim - 1) + [(0, LANES // 2 - half)]
    return jnp.concatenate([jnp.pad(x[..., :half], pad), jnp.pad(x[..., half:], pad)], axis=-1)


def _rope_tables(pos, half):
    inv_freq = 1.0 / (ROPE_THETA ** (jnp.arange(half, dtype=F32) / half))
    ang = pos[:, None] * inv_freq[None, :]
    cos, sin = jnp.cos(ang), jnp.sin(ang)
    cos_t = _split_half_layout(jnp.concatenate([cos, cos], -1), half)
    sin_t = _split_half_layout(jnp.concatenate([-sin, sin], -1), half)
    return cos_t, sin_t


def kernel(x_prompt, x_sample, state_conv, state_delta, cache_ckv, cache_krope, meta_tokens,
           a_w_in, a_conv_w, a_a_log, a_dt_bias, a_o_gain, a_w_o,
           b_w_dq, b_q_gain, b_w_uq, b_w_o,
           kv_w_dkv, kv_gain, kv_w_uk, kv_w_uv,
           mlp_w_up, mlp_w_down, ln_gain, ln_bias):
    B, seq, D = x_prompt.shape
    DB, ts, _ = x_sample.shape
    n_meta = meta_tokens.shape[0]
    depth = ln_gain.shape[0]
    n_a = a_w_in.shape[0]
    H = a_a_log.shape[1]
    dk = a_o_gain.shape[1]
    W = H * dk
    width = a_conv_w.shape[1]
    past = cache_ckv.shape[1]
    KV, MH, nope = kv_w_uk.shape
    vd = kv_w_uv.shape[2]
    rope = cache_krope.shape[2]
    half = rope // 2
    alpha = (2 * depth) ** 0.25
    scale = (nope + rope) ** -0.5
    assert ts % CHUNK == 0 and ts >= width - 1 and half < LANES // 2
    assert nope % LANES == 0 and vd % LANES == 0 and KV % LANES == 0 and dk % LANES == 0

    L = n_meta + seq
    LP = _round_up(L, LANES)
    n_prompt_rows = B * LP
    M = n_prompt_rows + DB * ts
    cps, cs = LP // CHUNK, ts // CHUNK
    n_p = B * cps
    hg = min(32, H)
    n_hg = H // hg

    pieces = []
    for b in range(B):
        pieces += [meta_tokens.astype(F32), x_prompt[b], jnp.zeros((LP - L, D), F32)]
    x = jnp.concatenate(pieces + [x_sample.reshape(DB * ts, D)], axis=0)
    xb = x.astype(BF16)

    pos = jnp.concatenate([jnp.tile(jnp.arange(LP, dtype=F32), B),
                           jnp.tile(past + jnp.arange(ts, dtype=F32), DB)])
    cos_t, sin_t = _rope_tables(pos, half)

    w_in_t = jnp.swapaxes(a_w_in, 1, 2)

    conv0 = jnp.concatenate([jnp.zeros((n_a, B, width - 1, 3 * W), F32), state_conv.astype(F32)], axis=1)
    conv0 = jnp.pad(conv0, ((0, 0), (0, 0), (SUBLANES - (width - 1), 0), (0, 0)))
    s0 = state_delta.astype(F32)

    new_conv_p, new_conv_s, states = [], [], None
    k_slabs = None
    lat = None
    for layer in range(depth):
        if layer < n_a:
            w_ab = a_w_in[layer, :, 4 * W:]
            gate_w = jnp.concatenate(
                [w_ab[:, :H].reshape(D, n_hg, hg), w_ab[:, H:].reshape(D, n_hg, hg),
                 jnp.zeros((D, n_hg, LANES - 2 * hg), w_ab.dtype)], axis=-1).reshape(D, n_hg * LANES)
            proj = _matmul(xb, w_in_t, F32, layer=layer, n_use=4 * W, b_transposed=True)
            gates = _matmul(xb, gate_w.astype(BF16), F32, bm=328)
            o, s_new, s_run, *conv_tail = _gdn_mixer(
                proj, gates, conv0[layer], s0, a_conv_w[layer].astype(F32),
                a_a_log[layer], a_dt_bias[layer], a_o_gain[layer],
                layer=layer, prev_states=states, heads=H, dk=dk, hg=hg,
                n_p=n_p, cps=cps, cs=cs, l_valid=L)
            states = (s_new, s_run)
            h = _matmul(o, a_w_o, BF16, layer=layer)
            conv_new = jnp.concatenate(conv_tail, axis=-1)[:, SUBLANES - (width - 1):]
            new_conv_p.append(conv_new[:B])
            new_conv_s.append(conv_new[B:])
        else:
            j = layer - n_a
            ql = _matmul(xb, b_w_dq[j].astype(BF16), BF16, epilogue="rms", gain=b_q_gain[j])
            w_uq = b_w_uq[j].reshape(-1, MH, nope + rope)
            w_uq = jnp.concatenate([w_uq[..., :nope], _split_half_layout(w_uq[..., nope:], half)], axis=-1)
            q = _matmul(ql, w_uq.reshape(-1, MH * (nope + LANES)).astype(BF16), BF16, bn=2048, epilogue="rope",
                        rope=(cos_t, sin_t, nope // LANES + 1, nope // LANES, scale * math.log2(math.e)))
            kp, vp, ks, ls_t, vs, lkp, lks = k_slabs
            o = _attention_causal(q, kp, vp, nb=B, lq=LP, lk_pad=lkp, heads=MH, nope=nope,
                                  vd=vd, bq=_pick_block(LP, 1408, LANES), bk=512, n_meta=n_meta, n_keys=L)
            o = _attention_full(q, ks, ls_t, vs, o, nb=DB, lq=ts, lk_pad=lks, row0=n_prompt_rows,
                                heads=MH, hb=min(8, MH), nope=nope, vd=vd, kv=KV, n_keys=past + ts)
            h = _matmul(o, b_w_o, BF16, layer=j)

        x, xb = _deepnorm(x, h, ln_gain[layer, 0], ln_bias[layer, 0], alpha)
        hid = _matmul(xb, mlp_w_up, BF16, layer=layer, epilogue="relu2")
        h = _matmul(hid, mlp_w_down, BF16, layer=layer, bn=1024, bk=2048)
        x, xb = _deepnorm(x, h, ln_gain[layer, 1], ln_bias[layer, 1], alpha)

        if layer == n_a - 1:
            w_dkv = jnp.concatenate([kv_w_dkv[:, :KV], _split_half_layout(kv_w_dkv[:, KV:], half)], axis=1)
            lat = _kv_post(_matmul(xb, w_dkv.astype(BF16), F32, bm=328, bn=640), kv_gain.astype(F32),
                           cos_t, sin_t, KV)
            lane_ids = jnp.arange(LANES)
            eye = jnp.where((lane_ids[:, None] == lane_ids[None, :]) & (lane_ids[:, None] != half), 1.0, 0.0)
            ones_col = jnp.where((lane_ids[:, None] == half) & (lane_ids[None, :] == 0), 1.0, 0.0)
            w_k = jnp.concatenate(
                [jnp.concatenate([kv_w_uk.astype(F32), jnp.zeros((KV, MH, LANES), F32)], axis=-1),
                 jnp.concatenate([jnp.zeros((LANES, MH, nope), F32),
                                  jnp.broadcast_to(eye[:, None, :], (LANES, MH, LANES))], axis=-1)],
                axis=0).reshape(KV + LANES, MH * (nope + LANES)).astype(BF16)
            w_v = jnp.concatenate(
                [jnp.concatenate([kv_w_uv.astype(F32), jnp.zeros((KV, MH, LANES), F32)], axis=-1),
                 jnp.concatenate([jnp.zeros((LANES, MH, vd), F32),
                                  jnp.broadcast_to(ones_col[:, None, :], (LANES, MH, LANES))], axis=-1)],
                axis=0).reshape(KV + LANES, MH * (vd + LANES)).astype(BF16)
            one_lane = jnp.where(jnp.arange(KV + LANES) == KV + half, 1.0, 0.0)
            lkp = _round_up(L, 512)
            lat_p = lat[:n_prompt_rows].reshape(B, LP, KV + LANES)
            if lkp >= LP:
                lat_p = jnp.pad(lat_p, ((0, 0), (0, lkp - LP), (0, 0)))
            else:
                lat_p = lat_p[:, :lkp]
            lat_p = (lat_p.reshape(B * lkp, KV + LANES) + one_lane).astype(BF16)
            lks = _round_up(past + ts, LANES)
            cache = jnp.concatenate([cache_ckv.astype(F32), _split_half_layout(cache_krope.astype(F32), half)],
                                    axis=-1)
            lat_s = jnp.concatenate([cache, lat[n_prompt_rows:].reshape(DB, ts, KV + LANES),
                                     jnp.zeros((DB, lks - past - ts, KV + LANES), F32)], axis=1)
            lat_s = (lat_s.reshape(DB * lks, KV + LANES) + one_lane).astype(BF16)
            w_kn_t = jnp.transpose(kv_w_uk, (1, 2, 0)).reshape(MH * nope, KV).astype(BF16)
            lat_s_t = lat_s.T
            big = dict(bm=2048, bn=2048)
            k_slabs = (_matmul(w_k.T, lat_p.T, BF16, bm=2048, bn=1536), _matmul(lat_p, w_v, BF16, bm=1536, bn=2048),
                       _matmul(w_kn_t, lat_s_t, BF16, **big), lat_s_t,
                       _matmul(lat_s, kv_w_uv.reshape(KV, MH * vd).astype(BF16), BF16, **big),
                       lkp, lks)

    def unsplit(r):
        return jnp.concatenate([r[..., :half], r[..., LANES // 2:LANES // 2 + half]], axis=-1)

    lat_p = lat[:n_prompt_rows].reshape(B, LP, KV + LANES)[:, :L]
    lat_s = lat[n_prompt_rows:].reshape(DB, ts, KV + LANES)
    y_prompt = jnp.stack([x[b * LP + n_meta:b * LP + L] for b in range(B)])
    y_sample = x[n_prompt_rows:].reshape(DB, ts, D)
    return (y_prompt, y_sample,
            jnp.stack(new_conv_p), states[0],
            lat_p[..., :KV], unsplit(lat_p[..., KV:]),
            jnp.stack(new_conv_s), states[1],
            lat_s[..., :KV], unsplit(lat_s[..., KV:]))
```
